```python
import math
import jax, jax.numpy as jnp
from jax import lax
import numpy as np

D_MODEL = 2048
BATCH = 4
SEQ = 4096
DEPTH = 2

D_MIX = D_MODEL
GROUP_W = D_MIX // 4
HEAD_DIM = 128
N_HEADS = GROUP_W // HEAD_DIM
RMS_EPS = 1e-6
ROPE_THETA = 10000.0
Q_BLOCK = 128
NEG = -1e30

GDN_CONV = 4
GDN_CHUNK = 64
MLA_Q_RANK = 384
MLA_KV_RANK = 256
MLA_NOPE = 128
MLA_ROPE = 64
MLA_V = HEAD_DIM
LRU_CONV = 4
LRU_C = 8.0
NSA_KV_HEADS = 1
CMP_LEN = 32
CMP_STRIDE = 16
CMP_HIDDEN = 256
SEL_LEN = 64
SEL_TOPK = 16
WINDOW = 512
FORCE = 1e9

IN_SIZES = (
    3 * GROUP_W, N_HEADS, N_HEADS, GROUP_W,
    MLA_Q_RANK, MLA_KV_RANK, MLA_ROPE, GROUP_W,
    GROUP_W, GROUP_W,
    GROUP_W, 6 * NSA_KV_HEADS * HEAD_DIM, 3 * N_HEADS, GROUP_W,
)
D_IN = sum(IN_SIZES)

kernel_name = "hybrid_parallel_heads_gdn_mla_rglru_nsa"


def rmsnorm(x, g):
    xf = x.astype(jnp.float32)
    y = xf * lax.rsqrt(jnp.mean(xf * xf, axis=-1, keepdims=True) + RMS_EPS)
    return (y * g.astype(jnp.float32)).astype(x.dtype)


def l2norm(x):
    return x * lax.rsqrt(jnp.sum(x * x, axis=-1, keepdims=True) + RMS_EPS)


def rope(x, pos):
    d = x.shape[-1]
    inv_freq = ROPE_THETA ** (-jnp.arange(0, d, 2, dtype=jnp.float32) / d)
    ang = pos.astype(jnp.float32)[..., None, None] * inv_freq
    cos, sin = jnp.cos(ang), jnp.sin(ang)
    x1, x2 = jnp.split(x.astype(jnp.float32), 2, axis=-1)
    return jnp.concatenate([x1 * cos - x2 * sin, x2 * cos + x1 * sin], axis=-1).astype(x.dtype)


def causal_dwconv(x, w):
    k_w, c = w.shape
    return lax.conv_general_dilated(
        x, w.astype(x.dtype)[:, None, :], window_strides=(1,), padding=[(k_w - 1, 0)],
        dimension_numbers=("NWC", "WIO", "NWC"), feature_group_count=c)


def _linear_combine(left, right):
    a_l, b_l = left
    a_r, b_r = right
    return a_l * a_r, a_r * b_l + b_r


def chunk_gated_delta_rule(q, k, v, g, beta):
    B, S, H, D = q.shape
    C = GDN_CHUNK
    N = S // C

    def chunks(t):
        return jnp.moveaxis(t.reshape((B, N, C, H) + t.shape[3:]), 3, 1)

    q, k, v, beta = chunks(q), chunks(k), chunks(v), chunks(beta)
    gc = jnp.cumsum(chunks(g), axis=-1)
    causal = jnp.tril(jnp.ones((C, C), bool))
    strict = jnp.tril(jnp.ones((C, C), bool), -1)
    decay = jnp.exp(jnp.where(causal, gc[..., :, None] - gc[..., None, :], -jnp.inf))
    kb = k * beta[..., None]
    low = jnp.where(strict, jnp.einsum("bhnid,bhnjd->bhnij", kb, k) * decay, 0.0)
    rhs = jnp.concatenate([v * beta[..., None], kb * jnp.exp(gc)[..., None]], axis=-1)
    uw = lax.linalg.triangular_solve(low + jnp.eye(C, dtype=low.dtype), rhs,
                                     left_side=True, lower=True, unit_diagonal=True)
    u, w = jnp.split(uw, 2, axis=-1)
    qk = jnp.einsum("bhnid,bhnjd->bhnij", q, k) * decay
    q_dec = q * jnp.exp(gc)[..., None]
    k_end = k * jnp.exp(gc[..., -1:] - gc)[..., None]
    c_dec = jnp.exp(gc[..., -1])

    def step(state, xs):
        qk_n, qd_n, u_n, w_n, ke_n, cd_n = xs
        v_new = u_n - jnp.einsum("bhcd,bhde->bhce", w_n, state)
        o_n = jnp.einsum("bhcd,bhde->bhce", qd_n, state) + jnp.einsum("bhij,bhje->bhie", qk_n, v_new)
        state = state * cd_n[..., None, None] + jnp.einsum("bhcd,bhce->bhde", ke_n, v_new)
        return state, o_n

    xs = tuple(jnp.moveaxis(t, 2, 0) for t in (qk, q_dec, u, w, k_end, c_dec))
    _, o = lax.scan(step, jnp.zeros((B, H, D, D), jnp.float32), xs)
    return jnp.moveaxis(o, 0, 2).transpose(0, 2, 3, 1, 4).reshape(B, S, H, D)


def gated_deltanet(qkv, a_logit, b_logit, gate, conv_w, a_log, dt_bias, norm_g):
    B, S, _ = qkv.shape
    out_dtype = qkv.dtype
    h = jax.nn.silu(causal_dwconv(qkv, conv_w)).astype(jnp.float32)
    q, k, v = [t.reshape(B, S, N_HEADS, HEAD_DIM) for t in jnp.split(h, 3, axis=-1)]
    q = l2norm(q) * (HEAD_DIM ** -0.5)
    k = l2norm(k)
    beta = jax.nn.sigmoid(b_logit.astype(jnp.float32))
    g = -jnp.exp(a_log.astype(jnp.float32)) * jax.nn.softplus(
        a_logit.astype(jnp.float32) + dt_bias.astype(jnp.float32))
    o = chunk_gated_delta_rule(q, k, v, g, beta)
    o = rmsnorm(o, norm_g) * jax.nn.silu(gate.astype(jnp.float32).reshape(B, S, N_HEADS, HEAD_DIM))
    return o.reshape(B, S, GROUP_W).astype(out_dtype)


def causal_attention(q, k, v, scale):
    B, S, H, Dq = q.shape
    Dv = v.shape[-1]
    nb = S // Q_BLOCK
    q_blk = jnp.moveaxis(q.reshape(B, nb, Q_BLOCK, H, Dq), 1, 0)
    key_pos = jnp.arange(S)

    def block(args):
        i, q_i = args
        t = i * Q_BLOCK + jnp.arange(Q_BLOCK)
        s = jnp.einsum("bqhd,bkhd->bhqk", q_i, k, preferred_element_type=jnp.float32) * scale
        s = jnp.where(key_pos[None, :] <= t[:, None], s, NEG)
        p = jax.nn.softmax(s, axis=-1)
        return jnp.einsum("bhqk,bkhd->bqhd", p.astype(v.dtype), v)

    o = lax.map(block, (jnp.arange(nb), q_blk))
    return jnp.moveaxis(o, 0, 1).reshape(B, S, H, Dv)


def mla(c_q, c_kv, k_pe, gate, pos, q_norm_g, w_uq, kv_norm_g, w_ukv, qk_norm_g):
    B, S, _ = c_q.shape
    q = jnp.einsum("bsr,rhd->bshd", rmsnorm(c_q, q_norm_g), w_uq)
    kv = jnp.einsum("bsr,rhd->bshd", rmsnorm(c_kv, kv_norm_g), w_ukv)
    qg, kg = qk_norm_g[0], qk_norm_g[1]
    q_nope = rmsnorm(q[..., :MLA_NOPE], qg[:MLA_NOPE])
    q_pe = rope(rmsnorm(q[..., MLA_NOPE:], qg[MLA_NOPE:]), pos)
    k_nope = rmsnorm(kv[..., :MLA_NOPE], kg[:MLA_NOPE])
    v = kv[..., MLA_NOPE:]
    k_pe = rope(rmsnorm(k_pe, kg[MLA_NOPE:])[:, :, None, :], pos)
    q = jnp.concatenate([q_nope, q_pe], axis=-1)
    k = jnp.concatenate([k_nope, jnp.broadcast_to(k_pe, (B, S, N_HEADS, MLA_ROPE))], axis=-1)
    o = causal_attention(q, k, v, (MLA_NOPE + MLA_ROPE) ** -0.5)
    o = o.astype(jnp.float32) * jax.nn.silu(gate.astype(jnp.float32).reshape(B, S, N_HEADS, MLA_V))
    return o.reshape(B, S, GROUP_W).astype(c_q.dtype)


def rglru(xb, gate, conv_w, conv_b, w_r, b_r, w_i, b_i, lam):
    B, S, W = xb.shape
    xc = (causal_dwconv(xb, conv_w) + conv_b).astype(jnp.float32)
    xh = xc.reshape(B, S, N_HEADS, W // N_HEADS)
    r = jax.nn.sigmoid(jnp.einsum("bshi,hij->bshj", xh, w_r.astype(jnp.float32)).reshape(B, S, W)
                       + b_r.astype(jnp.float32))
    i_g = jax.nn.sigmoid(jnp.einsum("bshi,hij->bshj", xh, w_i.astype(jnp.float32)).reshape(B, S, W)
                         + b_i.astype(jnp.float32))
    log_a = -LRU_C * r * jax.nn.softplus(-lam.astype(jnp.float32))
    a = jnp.exp(log_a)
    b = jnp.sqrt(-jnp.expm1(2.0 * log_a)) * (i_g * xc)
    _, h = lax.associative_scan(_linear_combine, (a, b), axis=1)
    return (h * jax.nn.silu(gate.astype(jnp.float32))).astype(xb.dtype)


def nsa(q, kv, gate_logits, gate, pos, q_norm_g, k_norm_g, cmp_pe, cmp_w1, cmp_b1, cmp_w2, cmp_b2):
    B, S, _ = q.shape
    H, D, G = N_HEADS, HEAD_DIM, NSA_KV_HEADS
    R = H // G
    scale = D ** -0.5
    q = rope(rmsnorm(q.reshape(B, S, H, D), q_norm_g), pos)
    k_c_raw, v_c_raw, k_s, v_s, k_w, v_w = [t.reshape(B, S, G, D) for t in jnp.split(kv, 6, axis=-1)]

    n_cmp = (S - CMP_LEN) // CMP_STRIDE + 1
    cmp_idx = jnp.arange(n_cmp)[:, None] * CMP_STRIDE + jnp.arange(CMP_LEN)[None, :]
    cmp_end = cmp_idx[:, -1]

    def compress(t, j):
        blk = t[:, cmp_idx] + cmp_pe[j][None, None, :, None, :]
        flat = blk.transpose(0, 1, 3, 2, 4).reshape(B, n_cmp, G, CMP_LEN * D)
        hid = jax.nn.silu(flat @ cmp_w1[j] + cmp_b1[j])
        return hid @ cmp_w2[j] + cmp_b2[j]

    k_c = rope(rmsnorm(compress(k_c_raw, 0), k_norm_g[0]), pos[:, cmp_end])
    v_c = compress(v_c_raw, 1)
    k_s = rope(rmsnorm(k_s, k_norm_g[1]), pos)
    k_w = rope(rmsnorm(k_w, k_norm_g[2]), pos)

    n_sel = S // SEL_LEN
    top_k = min(SEL_TOPK, n_sel)
    sel_start = jnp.arange(n_sel) * SEL_LEN
    blk_ids = jnp.arange(n_sel)
    overlap = ((cmp_idx[:, :1] < sel_start[None, :] + SEL_LEN)
               & (cmp_end[:, None] >= sel_start[None, :])).astype(jnp.float32)
    k_sb = k_s.reshape(B, n_sel, SEL_LEN, G, D).transpose(0, 3, 1, 2, 4)
    v_sb = v_s.reshape(B, n_sel, SEL_LEN, G, D).transpose(0, 3, 1, 2, 4)
    k_wp = jnp.pad(k_w, ((0, 0), (WINDOW, 0), (0, 0), (0, 0)))
    v_wp = jnp.pad(v_w, ((0, 0), (WINDOW, 0), (0, 0), (0, 0)))
    gates = jax.nn.sigmoid(gate_logits.astype(jnp.float32)).reshape(B, S, H, 3)

    nb = S // Q_BLOCK
    q_blk = jnp.moveaxis(q.reshape(B, nb, Q_BLOCK, G, R, D), 1, 0)
    g_blk = jnp.moveaxis(gates.reshape(B, nb, Q_BLOCK, H, 3), 1, 0)
    b_ix = jnp.arange(B)[:, None, None, None]
    g_ix = jnp.arange(G)[None, :, None, None]

    def block(args):
        i, q_i, g_i = args
        t = i * Q_BLOCK + jnp.arange(Q_BLOCK)
        s_c = jnp.einsum("bqgrd,bcgd->bgrqc", q_i, k_c, preferred_element_type=jnp.float32) * scale
        valid_c = cmp_end[None, :] <= t[:, None]
        p_c = jax.nn.softmax(jnp.where(valid_c, s_c, NEG), axis=-1)
        p_c = jnp.where(jnp.any(valid_c, axis=-1)[:, None], p_c, 0.0)
        o_c = jnp.einsum("bgrqc,bcgd->bqgrd", p_c.astype(v_c.dtype), v_c)
        imp = jnp.einsum("bgrqc,cn->bgqn", p_c, overlap)
        cur = t // SEL_LEN
        valid_s = sel_start[None, :] <= t[:, None]
        forced = (blk_ids[None, :] == 0) | (blk_ids[None, :] == cur[:, None]) | (blk_ids[None, :] == cur[:, None] - 1)
        imp = jnp.where(valid_s, imp, -1.0)
        imp = jnp.where(forced & valid_s, FORCE, imp)
        _, sel = lax.top_k(imp, top_k)
        k_g = k_sb[b_ix, g_ix, sel]
        v_g = v_sb[b_ix, g_ix, sel]
        tok = sel[..., None] * SEL_LEN + jnp.arange(SEL_LEN)
        valid_t = (tok <= t[:, None, None]) & (sel_start[sel] <= t[:, None])[..., None]
        s_s = jnp.einsum("bqgrd,bgqkld->bgrqkl", q_i, k_g, preferred_element_type=jnp.float32) * scale
        s_s = jnp.where(valid_t[:, :, None], s_s, NEG).reshape(B, G, R, Q_BLOCK, top_k * SEL_LEN)
        p_s = jax.nn.softmax(s_s, axis=-1).reshape(B, G, R, Q_BLOCK, top_k, SEL_LEN)
        o_s = jnp.einsum("bgrqkl,bgqkld->bqgrd", p_s.astype(v_g.dtype), v_g)
        k_i = lax.dynamic_slice_in_dim(k_wp, i * Q_BLOCK, WINDOW + Q_BLOCK, axis=1)
        v_i = lax.dynamic_slice_in_dim(v_wp, i * Q_BLOCK, WINDOW + Q_BLOCK, axis=1)
        kpos = i * Q_BLOCK - WINDOW + jnp.arange(WINDOW + Q_BLOCK)
        valid_w = ((kpos[None, :] <= t[:, None]) & (kpos[None, :] > t[:, None] - WINDOW)
                   & (kpos[None, :] >= 0))
        s_w = jnp.einsum("bqgrd,bkgd->bgrqk", q_i, k_i, preferred_element_type=jnp.float32) * scale
        p_w = jax.nn.softmax(jnp.where(valid_w, s_w, NEG), axis=-1)
        o_w = jnp.einsum("bgrqk,bkgd->bqgrd", p_w.astype(v_i.dtype), v_i)
        shape = (B, Q_BLOCK, H, D)
        return (g_i[..., 0:1] * o_c.reshape(shape) + g_i[..., 1:2] * o_s.reshape(shape)
                + g_i[..., 2:3] * o_w.reshape(shape))

    o = lax.map(block, (jnp.arange(nb), q_blk, g_blk))
    o = jnp.moveaxis(o, 0, 1).reshape(B, S, GROUP_W)
    return (o * jax.nn.silu(gate.astype(jnp.float32))).astype(q.dtype)


def setup_inputs(seed: int = 0) -> dict:
    key = jax.random.key(seed)
    keys = iter(jax.random.split(key, 40))
    f32 = jnp.float32
    L = DEPTH

    def nrm(shape, scale):
        return jax.random.normal(next(keys), shape, f32) * scale

    def gain(shape):
        return 1.0 + nrm(shape, 0.02)

    def unif(shape, lo, hi):
        return jax.random.uniform(next(keys), shape, f32, lo, hi)

    x = jax.random.normal(next(keys), (BATCH, SEQ, D_MODEL), f32)
    positions = jnp.broadcast_to(jnp.arange(SEQ, dtype=jnp.int32), (BATCH, SEQ))
    dt = jnp.exp(unif((L, N_HEADS), math.log(1e-3), math.log(1e-1)))
    a_lru = unif((L, GROUP_W), 0.9, 0.999) ** (1.0 / LRU_C)
    return {
        "x": x,
        "positions": positions,
        "norm_g": gain((L, D_MODEL)),
        "w_in": nrm((L, D_MODEL, D_IN), D_MODEL ** -0.5),
        "w_out": nrm((L, D_MIX, D_MODEL), D_MIX ** -0.5),
        "gdn_conv_w": nrm((L, GDN_CONV, 3 * GROUP_W), GDN_CONV ** -0.5),
        "gdn_a_log": jnp.log(unif((L, N_HEADS), 1.0, 16.0)),
        "gdn_dt_bias": dt + jnp.log(-jnp.expm1(-dt)),
        "gdn_norm_g": gain((L, HEAD_DIM)),
        "mla_q_norm_g": gain((L, MLA_Q_RANK)),
        "mla_w_uq": nrm((L, MLA_Q_RANK, N_HEADS, MLA_NOPE + MLA_ROPE), MLA_Q_RANK ** -0.5),
        "mla_kv_norm_g": gain((L, MLA_KV_RANK)),
        "mla_w_ukv": nrm((L, MLA_KV_RANK, N_HEADS, MLA_NOPE + MLA_V), MLA_KV_RANK ** -0.5),
        "mla_qk_norm_g": gain((L, 2, MLA_NOPE + MLA_ROPE)),
        "lru_conv_w": nrm((L, LRU_CONV, GROUP_W), LRU_CONV ** -0.5),
        "lru_conv_b": nrm((L, GROUP_W), 0.02),
        "lru_w_r": nrm((L, N_HEADS, HEAD_DIM, HEAD_DIM), HEAD_DIM ** -0.5),
        "lru_b_r": nrm((L, GROUP_W), 0.02),
        "lru_w_i": nrm((L, N_HEADS, HEAD_DIM, HEAD_DIM), HEAD_DIM ** -0.5),
        "lru_b_i": nrm((L, GROUP_W), 0.02),
        "lru_lambda": jnp.log(a_lru) - jnp.log1p(-a_lru),
        "nsa_q_norm_g": gain((L, HEAD_DIM)),
        "nsa_k_norm_g": gain((L, 3, HEAD_DIM)),
        "nsa_cmp_pe": nrm((L, 2, CMP_LEN, HEAD_DIM), 0.02),
        "nsa_cmp_w1": nrm((L, 2, CMP_LEN * HEAD_DIM, CMP_HIDDEN), (CMP_LEN * HEAD_DIM) ** -0.5),
        "nsa_cmp_b1": nrm((L, 2, CMP_HIDDEN), 0.02),
        "nsa_cmp_w2": nrm((L, 2, CMP_HIDDEN, HEAD_DIM), CMP_HIDDEN ** -0.5),
        "nsa_cmp_b2": nrm((L, 2, HEAD_DIM), 0.02),
    }


def reference(x, positions, norm_g, w_in, w_out, gdn_conv_w, gdn_a_log, gdn_dt_bias, gdn_norm_g,
              mla_q_norm_g, mla_w_uq, mla_kv_norm_g, mla_w_ukv, mla_qk_norm_g,
              lru_conv_w, lru_conv_b, lru_w_r, lru_b_r, lru_w_i, lru_b_i, lru_lambda,
              nsa_q_norm_g, nsa_k_norm_g, nsa_cmp_pe, nsa_cmp_w1, nsa_cmp_b1, nsa_cmp_w2, nsa_cmp_b2):
    splits = np.cumsum(IN_SIZES)[:-1].tolist()
    for l in range(DEPTH):
        h = rmsnorm(x, norm_g[l])
        proj = jnp.einsum("bsd,de->bse", h, w_in[l])
        (a_qkv, a_decay, a_beta, a_gate,
         b_cq, b_ckv, b_kpe, b_gate,
         c_x, c_gate,
         d_q, d_kv, d_gl, d_gate) = jnp.split(proj, splits, axis=-1)
        y_a = gated_deltanet(a_qkv, a_decay, a_beta, a_gate, gdn_conv_w[l], gdn_a_log[l],
                             gdn_dt_bias[l], gdn_norm_g[l])
        y_b = mla(b_cq, b_ckv, b_kpe, b_gate, positions, mla_q_norm_g[l], mla_w_uq[l],
                  mla_kv_norm_g[l], mla_w_ukv[l], mla_qk_norm_g[l])
        y_c = rglru(c_x, c_gate, lru_conv_w[l], lru_conv_b[l], lru_w_r[l], lru_b_r[l],
                    lru_w_i[l], lru_b_i[l], lru_lambda[l])
        y_d = nsa(d_q, d_kv, d_gl, d_gate, positions, nsa_q_norm_g[l], nsa_k_norm_g[l],
                  nsa_cmp_pe[l], nsa_cmp_w1[l], nsa_cmp_b1[l], nsa_cmp_w2[l], nsa_cmp_b2[l])
        y = jnp.concatenate([y_a, y_b, y_c, y_d], axis=-1)
        x = x + jnp.einsum("bse,ed->bsd", y, w_out[l])
    return x
```

```python
import functools
import math

import numpy as np
import jax
import jax.numpy as jnp
from jax import lax
from jax.experimental import pallas as pl
from jax.experimental.pallas import tpu as pltpu

F32 = jnp.float32
BF16 = jnp.bfloat16
HIGHEST = lax.Precision.HIGHEST

D_MODEL = 2048
GROUP_W = 512
HEAD_DIM = 128
N_HEADS = 4
RMS_EPS = 1e-6
ROPE_THETA = 10000.0
NEG = -1e30
GDN_CONV = 4
GDN_CHUNK = 128
MLA_Q_RANK = 384
MLA_KV_RANK = 256
MLA_NOPE = 128
MLA_ROPE = 64
LRU_CONV = 4
LRU_C = 8.0
CMP_LEN = 32
CMP_STRIDE = 16
CMP_HIDDEN = 256
SEL_LEN = 64
SEL_TOPK = 16
WINDOW = 512
FORCE = 1e9
Q_BLOCK = 128

IN_SIZES = (3 * GROUP_W, N_HEADS, N_HEADS, GROUP_W,
            MLA_Q_RANK, MLA_KV_RANK, MLA_ROPE, GROUP_W,
            GROUP_W, GROUP_W,
            GROUP_W, 6 * HEAD_DIM, 3 * N_HEADS, GROUP_W)

OFF_AQ, OFF_AK, OFF_AV, OFF_AG = 0, 512, 1024, 1536
OFF_BG, OFF_CX, OFF_CG, OFF_DQ, OFF_DG = 2048, 2560, 3072, 3584, 4096
OFF_DKV, OFF_BCQ, OFF_MISC, OFF_BCKV = 4608, 5376, 5760, 5888
D_PROJ = 6144
MISC_KPE, MISC_DECAY, MISC_BETA, MISC_GL = 0, 64, 68, 72

VMEM_LIMIT = 56 * 1024 * 1024


def _cparams(sem):
    return pltpu.CompilerParams(dimension_semantics=sem, vmem_limit_bytes=VMEM_LIMIT)


def _dot(a, b, precision=None):
    return lax.dot_general(a, b, (((1,), (0,)), ((), ())), precision=precision,
                           preferred_element_type=F32)


def _dot_nt(a, b, precision=None):
    return lax.dot_general(a, b, (((1,), (1,)), ((), ())), precision=precision,
                           preferred_element_type=F32)


def _bdot(a, b):
    return _dot(a.astype(BF16), b.astype(BF16))


def _bdot_nt(a, b):
    return _dot_nt(a.astype(BF16), b.astype(BF16))


def _silu(x):
    return x * jax.nn.sigmoid(x)


def _softplus(x):
    return jnp.maximum(x, 0.0) + jnp.log1p(jnp.exp(-jnp.abs(x)))


def _rms(x, n):
    return x * lax.rsqrt(jnp.sum(x * x, axis=-1, keepdims=True) * (1.0 / n) + RMS_EPS)


def _inproj_kernel(x_ref, g_ref, w_ref, o_ref, h_ref, *, tm):
    @pl.when(pl.program_id(1) == 0)
    def _():
        rows = min(256, tm)
        for r in range(tm // rows):
            x = x_ref[r * rows:(r + 1) * rows, :]
            h_ref[r * rows:(r + 1) * rows, :] = (_rms(x, D_MODEL) * g_ref[...]).astype(BF16)

    o_ref[...] = _dot(h_ref[...], w_ref[...])


def _inproj(x2d, g, w):
    m = x2d.shape[0]
    tm = min(1024, m)
    tn = 512
    return pl.pallas_call(
        functools.partial(_inproj_kernel, tm=tm),
        grid=(m // tm, D_PROJ // tn),
        in_specs=[pl.BlockSpec((tm, D_MODEL), lambda i, j: (i, 0)),
                  pl.BlockSpec((1, D_MODEL), lambda i, j: (0, 0)),
                  pl.BlockSpec((D_MODEL, tn), lambda i, j: (0, j))],
        out_specs=pl.BlockSpec((tm, tn), lambda i, j: (i, j)),
        out_shape=jax.ShapeDtypeStruct((m, D_PROJ), F32),
        scratch_shapes=[pltpu.VMEM((tm, D_MODEL), BF16)],
        compiler_params=_cparams(("parallel", "arbitrary")),
        name="inproj",
    )(x2d, g, w)


def _outproj_kernel(ya_ref, yb_ref, yc_ref, yd_ref, wa_ref, wb_ref, wc_ref, wd_ref, x_ref, o_ref):
    acc = x_ref[...]
    for y_ref, w_ref in ((ya_ref, wa_ref), (yb_ref, wb_ref), (yc_ref, wc_ref), (yd_ref, wd_ref)):
        acc = acc + _dot(y_ref[...].astype(BF16), w_ref[...])
    o_ref[...] = acc


def _outproj(ys, w, x2d):
    m = x2d.shape[0]
    tm = min(1024, m)
    tn = 512
    y_specs = [pl.BlockSpec((tm, GROUP_W), lambda i, j: (i, 0)) for _ in range(4)]
    w_specs = [pl.BlockSpec((GROUP_W, tn), functools.partial(lambda i, j, g: (g, j), g=g))
               for g in range(4)]
    return pl.pallas_call(
        _outproj_kernel,
        grid=(m // tm, D_MODEL // tn),
        in_specs=y_specs + w_specs + [pl.BlockSpec((tm, tn), lambda i, j: (i, j))],
        out_specs=pl.BlockSpec((tm, tn), lambda i, j: (i, j)),
        out_shape=jax.ShapeDtypeStruct((m, D_MODEL), F32),
        compiler_params=_cparams(("parallel", "arbitrary")),
        name="outproj",
    )(*ys, w, w, w, w, x2d)


def _causal_conv4(xin, hist, w):
    t = xin.shape[0]
    ext = jnp.concatenate([hist, xin], axis=0)
    acc = ext[5:5 + t] * w[0:1]
    for j in range(1, 4):
        acc = acc + ext[5 + j:5 + j + t] * w[j:j + 1]
    return acc


def _gdn_kernel(q_ref, k_ref, v_ref, gate_ref, misc_ref, cw_ref, alog_ref, dtb_ref, ng_ref,
                o_ref, hist_ref, state_ref, *, T):
    C = GDN_CHUNK

    @pl.when(pl.program_id(1) == 0)
    def _():
        hist_ref[...] = jnp.zeros_like(hist_ref)
        state_ref[...] = jnp.zeros_like(state_ref)

    cw = cw_ref[...]
    conv = []
    for idx, ref in enumerate((q_ref, k_ref, v_ref)):
        xin = ref[0]
        acc = _causal_conv4(xin, hist_ref[idx], cw[:, idx * GROUP_W:(idx + 1) * GROUP_W])
        hist_ref[idx] = xin[T - 8:T]
        conv.append(_silu(acc))
    qc, kc, vc = conv

    misc = misc_ref[0]
    gfull = -jnp.exp(alog_ref[...]) * _softplus(misc + dtb_ref[...])
    betafull = jax.nn.sigmoid(misc)
    gate = gate_ref[0]

    row = lax.broadcasted_iota(jnp.int32, (C, C), 0)
    col = lax.broadcasted_iota(jnp.int32, (C, C), 1)
    lower = row >= col
    strict = row > col
    ltri = lower.astype(F32)
    upper = row <= col
    ones = jnp.ones((C, C), F32)
    eye = (row == col).astype(F32)
    n_doublings = int(math.log2(C)) - 1

    for c in range(T // C):
        r0 = c * C
        for h in range(N_HEADS):
            l0 = h * HEAD_DIM
            q = qc[r0:r0 + C, l0:l0 + HEAD_DIM]
            k = kc[r0:r0 + C, l0:l0 + HEAD_DIM]
            v = vc[r0:r0 + C, l0:l0 + HEAD_DIM]
            q = q * lax.rsqrt(jnp.sum(q * q, axis=-1, keepdims=True) + RMS_EPS) * (HEAD_DIM ** -0.5)
            k = k * lax.rsqrt(jnp.sum(k * k, axis=-1, keepdims=True) + RMS_EPS)
            g_col = gfull[r0:r0 + C, MISC_DECAY + h:MISC_DECAY + h + 1]
            beta = betafull[r0:r0 + C, MISC_BETA + h:MISC_BETA + h + 1]
            g_b = jnp.broadcast_to(g_col, (C, C))
            gc = _dot(ltri, g_b, HIGHEST)
            gr = _dot(ones, jnp.where(upper, g_b, 0.0), HIGHEST)
            decay = jnp.exp(jnp.where(lower, gc - gr, NEG))
            eg = jnp.exp(gc)
            kb = k * beta
            a = jnp.where(strict, _bdot_nt(kb, k) * decay, 0.0)
            p = -a
            tinv = eye + p
            for _ in range(n_doublings):
                p = _dot(p, p, HIGHEST)
                tinv = tinv + _dot(tinv, p, HIGHEST)
            rhs = jnp.concatenate([v * beta, kb * eg], axis=-1)
            uw = _dot(tinv, rhs, HIGHEST)
            u = uw[:, :HEAD_DIM]
            w = uw[:, HEAD_DIM:]
            qk = _bdot_nt(q, k) * decay
            q_dec = q * eg
            g_last = gc[C - 1:C, :]
            k_end = k * jnp.exp(g_last - gc)
            c_dec = jnp.exp(g_last)
            state = state_ref[h]
            v_new = u - _bdot(w, state)
            o = _bdot(q_dec, state) + _bdot(qk, v_new)
            state_ref[h] = state * c_dec + lax.dot_general(
                k_end.astype(BF16), v_new.astype(BF16), (((0,), (0,)), ((), ())),
                preferred_element_type=F32)
            o = _rms(o, HEAD_DIM) * ng_ref[...]
            o_ref[0, r0:r0 + C, l0:l0 + HEAD_DIM] = o * _silu(gate[r0:r0 + C, l0:l0 + HEAD_DIM])


def _seg_spec(t, width, off):
    blk = off // width
    assert blk * width == off
    return pl.BlockSpec((1, t, width), lambda b, i: (b, i, blk))


def _gdn(proj, conv_w, a_log, dt_bias, norm_g):
    b, s, _ = proj.shape
    t = min(256, s)
    pad = lambda vec, off: jnp.zeros((1, 128), F32).at[0, off:off + N_HEADS].set(vec)
    full = lambda shape: pl.BlockSpec(shape, lambda bb, i: (0,) * len(shape))
    return pl.pallas_call(
        functools.partial(_gdn_kernel, T=t),
        grid=(b, s // t),
        in_specs=[_seg_spec(t, GROUP_W, OFF_AQ), _seg_spec(t, GROUP_W, OFF_AK),
                  _seg_spec(t, GROUP_W, OFF_AV), _seg_spec(t, GROUP_W, OFF_AG),
                  _seg_spec(t, 128, OFF_MISC),
                  full((GDN_CONV, 3 * GROUP_W)), full((1, 128)), full((1, 128)), full((1, 128))],
        out_specs=pl.BlockSpec((1, t, GROUP_W), lambda bb, i: (bb, i, 0)),
        out_shape=jax.ShapeDtypeStruct((b, s, GROUP_W), F32),
        scratch_shapes=[pltpu.VMEM((3, 8, GROUP_W), F32),
                        pltpu.VMEM((N_HEADS, HEAD_DIM, HEAD_DIM), F32)],
        compiler_params=_cparams(("parallel", "arbitrary")),
        name="gdn",
    )(proj, proj, proj, proj, proj, conv_w, pad(a_log, MISC_DECAY), pad(dt_bias, MISC_DECAY),
      norm_g.reshape(1, HEAD_DIM))


def _lru_kernel(x_ref, gate_ref, cw_ref, cb_ref, wr_ref, br_ref, wi_ref, bi_ref, lam_ref,
                o_ref, hist_ref, h_ref, *, T):
    @pl.when(pl.program_id(1) == 0)
    def _():
        hist_ref[...] = jnp.zeros_like(hist_ref)
        h_ref[...] = jnp.zeros_like(h_ref)

    xin = x_ref[0]
    xc = _causal_conv4(xin, hist_ref[...], cw_ref[...]) + cb_ref[...]
    hist_ref[...] = xin[T - 8:T]
    rs, igs = [], []
    for h in range(N_HEADS):
        xh = xc[:, h * HEAD_DIM:(h + 1) * HEAD_DIM].astype(BF16)
        rs.append(_dot(xh, wr_ref[h]))
        igs.append(_dot(xh, wi_ref[h]))
    r = jax.nn.sigmoid(jnp.concatenate(rs, axis=-1) + br_ref[...])
    ig = jax.nn.sigmoid(jnp.concatenate(igs, axis=-1) + bi_ref[...])
    log_a = -LRU_C * r * _softplus(-lam_ref[...])
    a = jnp.exp(log_a)
    bb = jnp.sqrt(-jnp.tanh(log_a) * (a * a + 1.0)) * (ig * xc)
    rowi = lax.broadcasted_iota(jnp.int32, (T, GROUP_W), 0)
    d = 1
    while d < T:
        keep = rowi >= d
        a_sh = jnp.where(keep, pltpu.roll(a, d, 0), 1.0)
        b_sh = jnp.where(keep, pltpu.roll(bb, d, 0), 0.0)
        bb = a * b_sh + bb
        a = a * a_sh
        d *= 2
    hseq = bb + a * h_ref[...]
    h_ref[...] = hseq[T - 1:T]
    o_ref[0] = hseq * _silu(gate_ref[0])


def _lru(proj, conv_w, conv_b, w_r, b_r, w_i, b_i, lam):
    b, s, _ = proj.shape
    t = min(256, s)
    full = lambda shape: pl.BlockSpec(shape, lambda bb, i: (0,) * len(shape))
    vec = lambda v: v.reshape(1, GROUP_W)
    return pl.pallas_call(
        functools.partial(_lru_kernel, T=t),
        grid=(b, s // t),
        in_specs=[_seg_spec(t, GROUP_W, OFF_CX), _seg_spec(t, GROUP_W, OFF_CG),
                  full((LRU_CONV, GROUP_W)), full((1, GROUP_W)),
                  full((N_HEADS, HEAD_DIM, HEAD_DIM)), full((1, GROUP_W)),
                  full((N_HEADS, HEAD_DIM, HEAD_DIM)), full((1, GROUP_W)), full((1, GROUP_W))],
        out_specs=pl.BlockSpec((1, t, GROUP_W), lambda bb, i: (bb, i, 0)),
        out_shape=jax.ShapeDtypeStruct((b, s, GROUP_W), F32),
        scratch_shapes=[pltpu.VMEM((8, GROUP_W), F32), pltpu.VMEM((1, GROUP_W), F32)],
        compiler_params=_cparams(("parallel", "arbitrary")),
        name="rglru",
    )(proj, proj, conv_w, vec(conv_b), w_r.astype(BF16), vec(b_r), w_i.astype(BF16), vec(b_i),
      vec(lam))


def _inv_freq_row(d):
    i = np.arange(128) % (d // 2)
    return jnp.asarray((ROPE_THETA ** (-(2.0 * i) / d)).astype(np.float32).reshape(1, 128))


def _rope128(x, cos, sin_signed):
    return x * cos + pltpu.roll(x, 64, 1) * sin_signed


def _rope64(x, cos, sin_masked, lane):
    rot = jnp.where(lane < 32, -pltpu.roll(x, 96, 1), pltpu.roll(x, 32, 1))
    return x * cos + rot * sin_masked


def _mla_prep_kernel(cq_ref, ckv_ref, misc_ref, pos_ref, qng_ref, wuq_ref, kvng_ref, wukv_ref,
                     qgn_ref, qgp_ref, kgn_ref, kgp_ref, invf_ref,
                     q_ref, k_ref, v_ref, *, T):
    scale = (MLA_NOPE + MLA_ROPE) ** -0.5
    lane = lax.broadcasted_iota(jnp.int32, (T, 128), 1)
    ang = pos_ref[0] * invf_ref[...]
    cos = jnp.cos(ang)
    sin = jnp.where(lane < MLA_ROPE, jnp.sin(ang), 0.0)

    cq = _rms(cq_ref[0], MLA_Q_RANK) * qng_ref[...]
    qf = _dot(cq.astype(BF16), wuq_ref[...])
    ckv = _rms(ckv_ref[0], MLA_KV_RANK) * kvng_ref[...]
    kvf = _dot(ckv.astype(BF16), wukv_ref[...])
    kpe = jnp.where(lane < MLA_ROPE, misc_ref[0], 0.0)
    kpe = _rope64(_rms(kpe, MLA_ROPE) * kgp_ref[...], cos, sin, lane)
    for h in range(N_HEADS):
        o = h * 256
        qn = _rms(qf[:, o:o + 128], MLA_NOPE) * qgn_ref[...]
        qp = _rope64(_rms(qf[:, o + 128:o + 256], MLA_ROPE) * qgp_ref[...], cos, sin, lane)
        q_ref[0, :, o:o + 128] = (qn * scale).astype(BF16)
        q_ref[0, :, o + 128:o + 256] = (qp * scale).astype(BF16)
        kn = _rms(kvf[:, o:o + 128], MLA_NOPE) * kgn_ref[...]
        k_ref[0, :, o:o + 128] = kn.astype(BF16)
        k_ref[0, :, o + 128:o + 256] = kpe.astype(BF16)
        v_ref[0, :, h * 128:(h + 1) * 128] = kvf[:, o + 128:o + 256].astype(BF16)


def _mla_prep(proj, posb, q_norm_g, w_uq, kv_norm_g, w_ukv, qk_norm_g):
    b, s, _ = proj.shape
    t = min(512, s)
    wuq = jnp.pad(w_uq, ((0, 0), (0, 0), (0, 64))).reshape(MLA_Q_RANK, N_HEADS * 256).astype(BF16)
    wukv = w_ukv.reshape(MLA_KV_RANK, N_HEADS * 256).astype(BF16)
    pad64 = lambda v: jnp.pad(v, (0, 64)).reshape(1, 128)
    full = lambda shape: pl.BlockSpec(shape, lambda bb, i: (0,) * len(shape))
    tok = lambda w: pl.BlockSpec((1, t, w), lambda bb, i: (bb, i, 0))
    return pl.pallas_call(
        functools.partial(_mla_prep_kernel, T=t),
        grid=(b, s // t),
        in_specs=[_seg_spec(t, MLA_Q_RANK, OFF_BCQ), _seg_spec(t, MLA_KV_RANK, OFF_BCKV),
                  _seg_spec(t, 128, OFF_MISC), tok(128),
                  full((1, MLA_Q_RANK)), full((MLA_Q_RANK, 1024)),
                  full((1, MLA_KV_RANK)), full((MLA_KV_RANK, 1024)),
                  full((1, 128)), full((1, 128)), full((1, 128)), full((1, 128)), full((1, 128))],
        out_specs=[tok(1024), tok(1024), tok(512)],
        out_shape=[jax.ShapeDtypeStruct((b, s, 1024), BF16),
                   jax.ShapeDtypeStruct((b, s, 1024), BF16),
                   jax.ShapeDtypeStruct((b, s, 512), BF16)],
        compiler_params=_cparams(("parallel", "parallel")),
        name="mla_prep",
    )(proj, proj, proj, posb, q_norm_g.reshape(1, -1), wuq, kv_norm_g.reshape(1, -1), wukv,
      qk_norm_g[0, :MLA_NOPE].reshape(1, 128), pad64(qk_norm_g[0, MLA_NOPE:]),
      qk_norm_g[1, :MLA_NOPE].reshape(1, 128), pad64(qk_norm_g[1, MLA_NOPE:]),
      _inv_freq_row(MLA_ROPE))


def _mla_flash_kernel(q_ref, k_ref, v_ref, gate_ref, o_ref, m_ref, l_ref, acc_ref, *, tq, tk):
    i = pl.program_id(2)
    j = pl.program_id(3)

    @pl.when(j == 0)
    def _():
        m_ref[...] = jnp.full_like(m_ref, NEG)
        l_ref[...] = jnp.zeros_like(l_ref)
        acc_ref[...] = jnp.zeros_like(acc_ref)

    @pl.when(j <= i)
    def _():
        s = _dot_nt(q_ref[0], k_ref[0])
        row = lax.broadcasted_iota(jnp.int32, (tq, tk), 0) + i * tq
        col = lax.broadcasted_iota(jnp.int32, (tq, tk), 1) + j * tk
        s = jnp.where(col <= row, s, NEG)
        m_old = m_ref[...]
        m_new = jnp.maximum(m_old, jnp.max(s, axis=-1, keepdims=True))
        alpha = jnp.exp(m_old - m_new)
        p = jnp.exp(s - m_new)
        l_ref[...] = alpha * l_ref[...] + jnp.sum(p, axis=-1, keepdims=True)
        acc_ref[...] = alpha * acc_ref[...] + _dot(p.astype(BF16), v_ref[0])
        m_ref[...] = m_new

    @pl.when(j == pl.num_programs(3) - 1)
    def _():
        o_ref[0] = acc_ref[...] / l_ref[...] * _silu(gate_ref[0])


def _mla_flash(q, k, v, proj):
    b, s, _ = q.shape
    t = min(512, s)
    n = s // t
    gate_blk = OFF_BG // 128
    return pl.pallas_call(
        functools.partial(_mla_flash_kernel, tq=t, tk=t),
        grid=(b, N_HEADS, n, n),
        in_specs=[pl.BlockSpec((1, t, 256), lambda bb, h, i, j: (bb, i, h)),
                  pl.BlockSpec((1, t, 256), lambda bb, h, i, j: (bb, jnp.minimum(j, i), h)),
                  pl.BlockSpec((1, t, 128), lambda bb, h, i, j: (bb, jnp.minimum(j, i), h)),
                  pl.BlockSpec((1, t, 128), lambda bb, h, i, j: (bb, i, gate_blk + h))],
        out_specs=pl.BlockSpec((1, t, 128), lambda bb, h, i, j: (bb, i, h)),
        out_shape=jax.ShapeDtypeStruct((b, s, GROUP_W), F32),
        scratch_shapes=[pltpu.VMEM((t, 1), F32), pltpu.VMEM((t, 1), F32),
                        pltpu.VMEM((t, 128), F32)],
        compiler_params=_cparams(("parallel", "parallel", "parallel", "arbitrary")),
        name="mla_flash",
    )(q, k, v, proj)


def _nsa_prep_kernel(dq_ref, dkv_ref, pos_ref, qg_ref, kg_ref, invf_ref,
                     q_ref, ks_ref, vs_ref, kw_ref, vw_ref, kc_ref, vc_ref, *, T):
    scale = HEAD_DIM ** -0.5
    lane = lax.broadcasted_iota(jnp.int32, (T, 128), 1)
    ang = pos_ref[0] * invf_ref[...]
    cos = jnp.cos(ang)
    sin = jnp.sin(ang)
    sin = jnp.where(lane < 64, -sin, sin)
    dq = dq_ref[0]
    for h in range(N_HEADS):
        qh = _rms(dq[:, h * 128:(h + 1) * 128], HEAD_DIM) * qg_ref[...]
        q_ref[0, :, h * 128:(h + 1) * 128] = _rope128(qh, cos, sin) * scale
    kv = dkv_ref[0]
    kc_ref[0] = kv[:, 0:128]
    vc_ref[0] = kv[:, 128:256]
    ks_ref[0] = _rope128(_rms(kv[:, 256:384], HEAD_DIM) * kg_ref[1:2], cos, sin).astype(BF16)
    vs_ref[0] = kv[:, 384:512].astype(BF16)
    kw_ref[0] = _rope128(_rms(kv[:, 512:640], HEAD_DIM) * kg_ref[2:3], cos, sin).astype(BF16)
    vw_ref[0] = kv[:, 640:768].astype(BF16)


def _nsa_prep(proj, posb, q_norm_g, k_norm_g):
    b, s, _ = proj.shape
    t = min(512, s)
    full = lambda shape: pl.BlockSpec(shape, lambda bb, i: (0,) * len(shape))
    tok = lambda w: pl.BlockSpec((1, t, w), lambda bb, i: (bb, i, 0))
    sds = lambda w, dt: jax.ShapeDtypeStruct((b, s, w), dt)
    return pl.pallas_call(
        functools.partial(_nsa_prep_kernel, T=t),
        grid=(b, s // t),
        in_specs=[_seg_spec(t, GROUP_W, OFF_DQ), _seg_spec(t, 768, OFF_DKV), tok(128),
                  full((1, 128)), full((3, 128)), full((1, 128))],
        out_specs=[tok(512)] + [tok(128)] * 6,
        out_shape=[sds(512, F32), sds(128, BF16), sds(128, BF16), sds(128, BF16), sds(128, BF16),
                   sds(128, F32), sds(128, F32)],
        compiler_params=_cparams(("parallel", "parallel")),
        name="nsa_prep",
    )(proj, proj, posb, q_norm_g.reshape(1, HEAD_DIM), k_norm_g, _inv_freq_row(HEAD_DIM))


def _nsa_cmp_kernel(kt_ref, vt_ref, pe_ref, w1_ref, b1_ref, w2_ref, b2_ref, kg_ref, pos_ref,
                    invf_ref, kc_ref, vc_ref, *, NC):
    half = CMP_STRIDE * HEAD_DIM
    outs = []
    for j, t_ref in enumerate((kt_ref, vt_ref)):
        t2 = t_ref[0].astype(BF16)
        first = _dot(t2, w1_ref[j, :half, :])
        second = pltpu.roll(_dot(t2, w1_ref[j, half:, :]), NC - 1, 0)
        pe8 = jnp.broadcast_to(pe_ref[j], (8, CMP_LEN * HEAD_DIM)).astype(BF16)
        bias = _dot(pe8, w1_ref[j])[0:1] + b1_ref[j]
        hid = _silu(first + second + bias)
        outs.append(_dot(hid.astype(BF16), w2_ref[j]) + b2_ref[j])
    k_c, v_c = outs
    lane = lax.broadcasted_iota(jnp.int32, (NC, 128), 1)
    ang = pos_ref[0] * invf_ref[...]
    sin = jnp.sin(ang)
    sin = jnp.where(lane < 64, -sin, sin)
    kc_ref[0] = _rope128(_rms(k_c, HEAD_DIM) * kg_ref[...], jnp.cos(ang), sin)
    vc_ref[0] = v_c


def _nsa_compress(kc_raw, vc_raw, posc, cmp_pe, cmp_w1, cmp_b1, cmp_w2, cmp_b2, kg0):
    b, s, _ = kc_raw.shape
    nc = s // CMP_STRIDE
    kt = kc_raw.reshape(b, nc, CMP_STRIDE * HEAD_DIM)
    vt = vc_raw.reshape(b, nc, CMP_STRIDE * HEAD_DIM)
    full = lambda shape: pl.BlockSpec(shape, lambda bb: (0,) * len(shape))
    per_b = lambda shape: pl.BlockSpec((1,) + shape, lambda bb: (bb, 0, 0))
    return pl.pallas_call(
        functools.partial(_nsa_cmp_kernel, NC=nc),
        grid=(b,),
        in_specs=[per_b((nc, CMP_STRIDE * HEAD_DIM)), per_b((nc, CMP_STRIDE * HEAD_DIM)),
                  full((2, 1, CMP_LEN * HEAD_DIM)), full((2, CMP_LEN * HEAD_DIM, CMP_HIDDEN)),
                  full((2, 1, CMP_HIDDEN)), full((2, CMP_HIDDEN, HEAD_DIM)),
                  full((2, 1, HEAD_DIM)), full((1, HEAD_DIM)), per_b((nc, 128)), full((1, 128))],
        out_specs=[per_b((nc, HEAD_DIM)), per_b((nc, HEAD_DIM))],
        out_shape=[jax.ShapeDtypeStruct((b, nc, HEAD_DIM), F32),
                   jax.ShapeDtypeStruct((b, nc, HEAD_DIM), F32)],
        compiler_params=_cparams(("parallel",)),
        name="nsa_compress",
    )(kt, vt, cmp_pe.reshape(2, 1, CMP_LEN * HEAD_DIM), cmp_w1.astype(BF16),
      cmp_b1.reshape(2, 1, CMP_HIDDEN), cmp_w2.astype(BF16), cmp_b2.reshape(2, 1, HEAD_DIM),
      kg0.reshape(1, HEAD_DIM), posc, _inv_freq_row(HEAD_DIM))


def _nsa_attn_kernel(q_ref, kc_ref, vc_ref, ks_ref, vs_ref, kw_ref, vw_ref, misc_ref, gate_ref,
                     o_ref, m_ref, l_ref, acc_ref, *, S, NC, KC, WK):
    Q = Q_BLOCK
    R = N_HEADS
    i = pl.program_id(1)
    n_sel = S // SEL_LEN
    n_cmp = (S - CMP_LEN) // CMP_STRIDE + 1
    top_k = min(SEL_TOPK, n_sel)

    qf = q_ref[0]
    q4 = jnp.concatenate([qf[:, r * 128:(r + 1) * 128] for r in range(R)], axis=0)
    q4b = q4.astype(BF16)

    s_c = _dot_nt(q4, kc_ref[0], HIGHEST)
    t_c = i * Q + (lax.broadcasted_iota(jnp.int32, (R * Q, NC), 0) & (Q - 1))
    c_ix = lax.broadcasted_iota(jnp.int32, (R * Q, NC), 1)
    valid_c = (c_ix * CMP_STRIDE + (CMP_LEN - 1) <= t_c) & (c_ix < n_cmp)
    s_c = jnp.where(valid_c, s_c, NEG)
    p_c = jnp.where(valid_c, jnp.exp(s_c - jnp.max(s_c, axis=-1, keepdims=True)), 0.0)
    p_c = p_c / jnp.maximum(jnp.sum(p_c, axis=-1, keepdims=True), 1e-30)
    o_c = _bdot(p_c, vc_ref[0])
    c_o = lax.broadcasted_iota(jnp.int32, (NC, 128), 0) * CMP_STRIDE
    n_o = lax.broadcasted_iota(jnp.int32, (NC, 128), 1) * SEL_LEN
    overlap = ((c_o < n_o + SEL_LEN) & (c_o + (CMP_LEN - 1) >= n_o)).astype(F32)
    imp4 = _dot(p_c, overlap, HIGHEST)
    imp = imp4[0:Q] + imp4[Q:2 * Q] + imp4[2 * Q:3 * Q] + imp4[3 * Q:4 * Q]

    t_q = i * Q + lax.broadcasted_iota(jnp.int32, (Q, 128), 0)
    n_ix = lax.broadcasted_iota(jnp.int32, (Q, 128), 1)
    cur = t_q >> 6
    valid_s = (n_ix * SEL_LEN <= t_q) & (n_ix < n_sel)
    forced = (n_ix == 0) | (n_ix == cur) | (n_ix == cur - 1)
    val = jnp.where(valid_s, imp, -1.0)
    val = jnp.where(forced & valid_s, FORCE, val)
    val = jnp.where(n_ix < n_sel, val, -2.0)
    sel = jnp.zeros((Q, 128), F32)
    n_f = n_ix.astype(F32)
    for _ in range(top_k):
        mx = jnp.max(val, axis=-1, keepdims=True)
        first = jnp.min(jnp.where(val == mx, n_f, 1e9), axis=-1, keepdims=True)
        hit = n_f == first
        sel = jnp.where(hit, 1.0, sel)
        val = jnp.where(hit, -3.0, val)
    sel_b = jnp.where(valid_s, sel, 0.0).astype(BF16)

    m_ref[...] = jnp.full_like(m_ref, NEG)
    l_ref[...] = jnp.zeros_like(l_ref)
    acc_ref[...] = jnp.zeros_like(acc_ref)
    t_k = i * Q + lax.broadcasted_iota(jnp.int32, (Q, KC), 0)
    col_k = lax.broadcasted_iota(jnp.int32, (Q, KC), 1)
    blk_row = lax.broadcasted_iota(jnp.int32, (128, KC), 0)
    blk_col = lax.broadcasted_iota(jnp.int32, (128, KC), 1) >> 6

    def sel_chunk(c, carry):
        k0 = pl.multiple_of(c * KC, KC)
        expand = (blk_row == blk_col + c * (KC // SEL_LEN)).astype(BF16)
        picked = _dot(sel_b, expand)
        ok = (picked > 0.5) & (col_k + k0 <= t_k)
        bias = jnp.where(ok, 0.0, NEG)
        s = _dot_nt(q4b, ks_ref[0, pl.ds(k0, KC), :]) + jnp.concatenate([bias] * R, axis=0)
        m_old = m_ref[...]
        m_new = jnp.maximum(m_old, jnp.max(s, axis=-1, keepdims=True))
        alpha = jnp.exp(m_old - m_new)
        p = jnp.exp(s - m_new)
        l_ref[...] = alpha * l_ref[...] + jnp.sum(p, axis=-1, keepdims=True)
        acc_ref[...] = alpha * acc_ref[...] + _dot(p.astype(BF16), vs_ref[0, pl.ds(k0, KC), :])
        m_ref[...] = m_new
        return carry

    n_chunks = ((i + 1) * Q + KC - 1) // KC
    lax.fori_loop(0, n_chunks, sel_chunk, 0)
    o_s = acc_ref[...] / l_ref[...]

    w0 = pl.multiple_of(jnp.maximum(i * Q - WINDOW, 0), Q)
    s_w = _dot_nt(q4b, kw_ref[0, pl.ds(w0, WK), :])
    t_w = i * Q + (lax.broadcasted_iota(jnp.int32, (R * Q, WK), 0) & (Q - 1))
    kpos = w0 + lax.broadcasted_iota(jnp.int32, (R * Q, WK), 1)
    s_w = jnp.where((kpos <= t_w) & (kpos > t_w - WINDOW), s_w, NEG)
    p_w = jnp.exp(s_w - jnp.max(s_w, axis=-1, keepdims=True))
    p_w = p_w / jnp.sum(p_w, axis=-1, keepdims=True)
    o_w = _dot(p_w.astype(BF16), vw_ref[0, pl.ds(w0, WK), :])

    gates = jax.nn.sigmoid(misc_ref[0])
    gate = gate_ref[0]
    for r in range(R):
        g0 = gates[:, MISC_GL + 3 * r:MISC_GL + 3 * r + 1]
        g1 = gates[:, MISC_GL + 3 * r + 1:MISC_GL + 3 * r + 2]
        g2 = gates[:, MISC_GL + 3 * r + 2:MISC_GL + 3 * r + 3]
        rows = slice(r * Q, (r + 1) * Q)
        o = g0 * o_c[rows] + g1 * o_s[rows] + g2 * o_w[rows]
        o_ref[0, :, r * 128:(r + 1) * 128] = o * _silu(gate[:, r * 128:(r + 1) * 128])


def _nsa_attn(qf, k_c, v_c, ks, vs, kw, vw, proj):
    b, s, _ = qf.shape
    nc = s // CMP_STRIDE
    kc_len = min(512, s)
    wk = min(WINDOW + Q_BLOCK, s)
    per_b = lambda n, w: pl.BlockSpec((1, n, w), lambda bb, i: (bb, 0, 0))
    return pl.pallas_call(
        functools.partial(_nsa_attn_kernel, S=s, NC=nc, KC=kc_len, WK=wk),
        grid=(b, s // Q_BLOCK),
        in_specs=[pl.BlockSpec((1, Q_BLOCK, GROUP_W), lambda bb, i: (bb, i, 0)),
                  per_b(nc, 128), per_b(nc, 128),
                  per_b(s, 128), per_b(s, 128), per_b(s, 128), per_b(s, 128),
                  _seg_spec(Q_BLOCK, 128, OFF_MISC), _seg_spec(Q_BLOCK, GROUP_W, OFF_DG)],
        out_specs=pl.BlockSpec((1, Q_BLOCK, GROUP_W), lambda bb, i: (bb, i, 0)),
        out_shape=jax.ShapeDtypeStruct((b, s, GROUP_W), F32),
        scratch_shapes=[pltpu.VMEM((N_HEADS * Q_BLOCK, 1), F32),
                        pltpu.VMEM((N_HEADS * Q_BLOCK, 1), F32),
                        pltpu.VMEM((N_HEADS * Q_BLOCK, 128), F32)],
        compiler_params=_cparams(("parallel", "arbitrary")),
        name="nsa_attn",
    )(qf, k_c, v_c, ks, vs, kw, vw, proj, proj)


def _reorder_w_in(w_in):
    o = np.concatenate([[0], np.cumsum(IN_SIZES)]).tolist()
    seg = lambda k: w_in[:, :, o[k]:o[k + 1]]
    (a_qkv, a_decay, a_beta, a_gate, b_cq, b_ckv, b_kpe, b_gate, c_x, c_gate,
     d_q, d_kv, d_gl, d_gate) = [seg(k) for k in range(len(IN_SIZES))]
    zeros = jnp.zeros(w_in.shape[:2] + (128 - MLA_ROPE - 2 * N_HEADS - 3 * N_HEADS,), w_in.dtype)
    out = jnp.concatenate([a_qkv, a_gate, b_gate, c_x, c_gate, d_q, d_gate, d_kv, b_cq,
                           b_kpe, a_decay, a_beta, d_gl, zeros, b_ckv], axis=-1)
    assert out.shape[-1] == D_PROJ
    return out.astype(BF16)


def kernel(x, positions, norm_g, w_in, w_out, gdn_conv_w, gdn_a_log, gdn_dt_bias, gdn_norm_g,
           mla_q_norm_g, mla_w_uq, mla_kv_norm_g, mla_w_ukv, mla_qk_norm_g,
           lru_conv_w, lru_conv_b, lru_w_r, lru_b_r, lru_w_i, lru_b_i, lru_lambda,
           nsa_q_norm_g, nsa_k_norm_g, nsa_cmp_pe, nsa_cmp_w1, nsa_cmp_b1, nsa_cmp_w2, nsa_cmp_b2):
    b, s, d = x.shape
    depth = w_in.shape[0]
    w_in_r = _reorder_w_in(w_in)
    w_out_b = w_out.astype(BF16)
    posf = positions.astype(F32)
    posb = jnp.broadcast_to(posf[:, :, None], (b, s, 128))
    nc = s // CMP_STRIDE
    pos_end = jnp.pad(posf.reshape(b, nc, CMP_STRIDE)[:, 1:, CMP_STRIDE - 1], ((0, 0), (0, 1)))
    posc = jnp.broadcast_to(pos_end[:, :, None], (b, nc, 128))

    x2d = x.reshape(b * s, d)
    for l in range(depth):
        proj = _inproj(x2d, norm_g[l].reshape(1, d), w_in_r[l]).reshape(b, s, D_PROJ)
        y_a = _gdn(proj, gdn_conv_w[l], gdn_a_log[l], gdn_dt_bias[l], gdn_norm_g[l])
        q_b, k_b, v_b = _mla_prep(proj, posb, mla_q_norm_g[l], mla_w_uq[l], mla_kv_norm_g[l],
                                  mla_w_ukv[l], mla_qk_norm_g[l])
        y_b = _mla_flash(q_b, k_b, v_b, proj)
        y_c = _lru(proj, lru_conv_w[l], lru_conv_b[l], lru_w_r[l], lru_b_r[l], lru_w_i[l],
                   lru_b_i[l], lru_lambda[l])
        q_d, ks, vs, kw, vw, kc_raw, vc_raw = _nsa_prep(proj, posb, nsa_q_norm_g[l],
                                                        nsa_k_norm_g[l])
        k_c, v_c = _nsa_compress(kc_raw, vc_raw, posc, nsa_cmp_pe[l], nsa_cmp_w1[l],
                                 nsa_cmp_b1[l], nsa_cmp_w2[l], nsa_cmp_b2[l], nsa_k_norm_g[l, 0])
        y_d = _nsa_attn(q_d, k_c, v_c, ks, vs, kw, vw, proj)
        ys = [y.reshape(b * s, GROUP_W) for y in (y_a, y_b, y_c, y_d)]
        x2d = _outproj(ys, w_out_b[l], x2d)
    return x2d.reshape(b, s, d)
```

```python
import functools
import math

import numpy as np
import jax
import jax.numpy as jnp
from jax import lax
from jax.experimental import pallas as pl
from jax.experimental.pallas import tpu as pltpu

F32 = jnp.float32
BF16 = jnp.bfloat16
HIGHEST = lax.Precision.HIGHEST

D_MODEL = 2048
GROUP_W = 512
HEAD_DIM = 128
N_HEADS = 4
RMS_EPS = 1e-6
ROPE_THETA = 10000.0
NEG = -1e30
GDN_CONV = 4
GDN_CHUNK = 128
MLA_Q_RANK = 384
MLA_KV_RANK = 256
MLA_NOPE = 128
MLA_ROPE = 64
LRU_CONV = 4
LRU_C = 8.0
CMP_LEN = 32
CMP_STRIDE = 16
CMP_HIDDEN = 256
SEL_LEN = 64
SEL_TOPK = 16
WINDOW = 512
FORCE = 1e9
Q_BLOCK = 128

IN_SIZES = (3 * GROUP_W, N_HEADS, N_HEADS, GROUP_W,
            MLA_Q_RANK, MLA_KV_RANK, MLA_ROPE, GROUP_W,
            GROUP_W, GROUP_W,
            GROUP_W, 6 * HEAD_DIM, 3 * N_HEADS, GROUP_W)

OFF_AQ, OFF_AK, OFF_AV, OFF_AG = 0, 512, 1024, 1536
OFF_BG, OFF_CX, OFF_CG, OFF_DQ, OFF_DG = 2048, 2560, 3072, 3584, 4096
OFF_DKV, OFF_BCQ, OFF_MISC, OFF_BCKV = 4608, 5376, 5760, 5888
D_PROJ = 6144
MISC_KPE, MISC_DECAY, MISC_BETA, MISC_GL = 0, 64, 68, 72

VMEM_LIMIT = 56 * 1024 * 1024


def _cparams(sem):
    return pltpu.CompilerParams(dimension_semantics=sem, vmem_limit_bytes=VMEM_LIMIT)


def _dot(a, b, precision=None):
    return lax.dot_general(a, b, (((1,), (0,)), ((), ())), precision=precision,
                           preferred_element_type=F32)


def _dot_nt(a, b, precision=None):
    return lax.dot_general(a, b, (((1,), (1,)), ((), ())), precision=precision,
                           preferred_element_type=F32)


def _bdot(a, b):
    return _dot(a.astype(BF16), b.astype(BF16))


def _bdot_nt(a, b):
    return _dot_nt(a.astype(BF16), b.astype(BF16))


def _silu(x):
    return x * jax.nn.sigmoid(x)


def _softplus(x):
    return jnp.maximum(x, 0.0) + jnp.log1p(jnp.exp(-jnp.abs(x)))


def _rms(x, n):
    return x * lax.rsqrt(jnp.sum(x * x, axis=-1, keepdims=True) * (1.0 / n) + RMS_EPS)


def _inproj_kernel(x_ref, g_ref, w_ref, o_ref, h_ref, *, tm):
    @pl.when(pl.program_id(1) == 0)
    def _():
        rows = min(256, tm)
        for r in range(tm // rows):
            x = x_ref[r * rows:(r + 1) * rows, :]
            h_ref[r * rows:(r + 1) * rows, :] = (_rms(x, D_MODEL) * g_ref[...]).astype(BF16)

    o_ref[...] = _dot(h_ref[...], w_ref[...])


def _inproj(x2d, g, w):
    m = x2d.shape[0]
    tm = min(1024, m)
    tn = 512
    return pl.pallas_call(
        functools.partial(_inproj_kernel, tm=tm),
        grid=(m // tm, D_PROJ // tn),
        in_specs=[pl.BlockSpec((tm, D_MODEL), lambda i, j: (i, 0)),
                  pl.BlockSpec((1, D_MODEL), lambda i, j: (0, 0)),
                  pl.BlockSpec((D_MODEL, tn), lambda i, j: (0, j))],
        out_specs=pl.BlockSpec((tm, tn), lambda i, j: (i, j)),
        out_shape=jax.ShapeDtypeStruct((m, D_PROJ), F32),
        scratch_shapes=[pltpu.VMEM((tm, D_MODEL), BF16)],
        compiler_params=_cparams(("parallel", "arbitrary")),
        name="inproj",
    )(x2d, g, w)


def _outproj_kernel(ya_ref, yb_ref, yc_ref, yd_ref, wa_ref, wb_ref, wc_ref, wd_ref, x_ref, o_ref):
    acc = x_ref[...]
    for y_ref, w_ref in ((ya_ref, wa_ref), (yb_ref, wb_ref), (yc_ref, wc_ref), (yd_ref, wd_ref)):
        acc = acc + _dot(y_ref[...].astype(BF16), w_ref[...])
    o_ref[...] = acc


def _outproj(ys, w, x2d):
    m = x2d.shape[0]
    tm = min(1024, m)
    tn = 512
    y_specs = [pl.BlockSpec((tm, GROUP_W), lambda i, j: (i, 0)) for _ in range(4)]
    w_specs = [pl.BlockSpec((GROUP_W, tn), functools.partial(lambda i, j, g: (g, j), g=g))
               for g in range(4)]
    return pl.pallas_call(
        _outproj_kernel,
        grid=(m // tm, D_MODEL // tn),
        in_specs=y_specs + w_specs + [pl.BlockSpec((tm, tn), lambda i, j: (i, j))],
        out_specs=pl.BlockSpec((tm, tn), lambda i, j: (i, j)),
        out_shape=jax.ShapeDtypeStruct((m, D_MODEL), F32),
        compiler_params=_cparams(("parallel", "arbitrary")),
        name="outproj",
    )(*ys, w, w, w, w, x2d)


def _causal_conv4(xin, hist, w):
    t = xin.shape[0]
    ext = jnp.concatenate([hist, xin], axis=0)
    acc = ext[5:5 + t] * w[0:1]
    for j in range(1, 4):
        acc = acc + ext[5 + j:5 + j + t] * w[j:j + 1]
    return acc


def _gdn_kernel(q_ref, k_ref, v_ref, gate_ref, misc_ref, cw_ref, alog_ref, dtb_ref, ng_ref,
                o_ref, hist_ref, state_ref, *, T):
    C = GDN_CHUNK

    @pl.when(pl.program_id(1) == 0)
    def _():
        hist_ref[...] = jnp.zeros_like(hist_ref)
        state_ref[...] = jnp.zeros_like(state_ref)

    cw = cw_ref[...]
    conv = []
    for idx, ref in enumerate((q_ref, k_ref, v_ref)):
        xin = ref[0]
        acc = _causal_conv4(xin, hist_ref[idx], cw[:, idx * GROUP_W:(idx + 1) * GROUP_W])
        hist_ref[idx] = xin[T - 8:T]
        conv.append(_silu(acc))
    qc, kc, vc = conv

    misc = misc_ref[0]
    gfull = -jnp.exp(alog_ref[...]) * _softplus(misc + dtb_ref[...])
    betafull = jax.nn.sigmoid(misc)
    gate = gate_ref[0]

    row = lax.broadcasted_iota(jnp.int32, (C, C), 0)
    col = lax.broadcasted_iota(jnp.int32, (C, C), 1)
    lower = row >= col
    strict = row > col
    ltri = lower.astype(F32)
    eye = (row == col).astype(F32)
    eye2 = jnp.concatenate([eye, eye], axis=1)
    left = lax.broadcasted_iota(jnp.int32, (C, 2 * C), 1) < C
    n_doublings = int(math.log2(C)) - 1

    def blockdiag(x):
        return jnp.concatenate([jnp.where(left, x, 0.0), jnp.where(left, 0.0, x)], axis=0)

    for c in range(T // C):
        r0 = c * C
        gall = _dot(ltri, gfull[r0:r0 + C], HIGHEST)
        hd = []
        for h in range(N_HEADS):
            l0 = h * HEAD_DIM
            q = qc[r0:r0 + C, l0:l0 + HEAD_DIM]
            k = kc[r0:r0 + C, l0:l0 + HEAD_DIM]
            v = vc[r0:r0 + C, l0:l0 + HEAD_DIM]
            q = q * lax.rsqrt(jnp.sum(q * q, axis=-1, keepdims=True) + RMS_EPS) * (HEAD_DIM ** -0.5)
            k = k * lax.rsqrt(jnp.sum(k * k, axis=-1, keepdims=True) + RMS_EPS)
            beta = betafull[r0:r0 + C, MISC_BETA + h:MISC_BETA + h + 1]
            gc = jnp.broadcast_to(gall[:, MISC_DECAY + h:MISC_DECAY + h + 1], (C, C))
            decay = jnp.exp(jnp.where(lower, gc - gc.T, NEG))
            kb = k * beta
            a = jnp.where(strict, _bdot_nt(kb, k) * decay, 0.0)
            hd.append((q, k, v, beta, gc, decay, kb, a))

        tinvs = []
        for h0 in range(0, N_HEADS, 2):
            x = jnp.concatenate([-hd[h0][7], -hd[h0 + 1][7]], axis=1)
            tinv = eye2 + x
            for _ in range(n_doublings):
                x = _dot(x, blockdiag(x), HIGHEST)
                tinv = tinv + _dot(tinv, blockdiag(x), HIGHEST)
            tinvs += [tinv[:, :C], tinv[:, C:]]

        for h in range(N_HEADS):
            l0 = h * HEAD_DIM
            q, k, v, beta, gc, decay, kb, _ = hd[h]
            eg = jnp.exp(gc)
            rhs = jnp.concatenate([v * beta, kb * eg], axis=-1)
            uw = _dot(tinvs[h], rhs, HIGHEST)
            u = uw[:, :HEAD_DIM]
            w = uw[:, HEAD_DIM:]
            qk = _bdot_nt(q, k) * decay
            q_dec = q * eg
            g_last = gc[C - 1:C, :]
            k_end = k * jnp.exp(g_last - gc)
            c_dec = jnp.exp(g_last)
            state = state_ref[h]
            v_new = u - _bdot(w, state)
            o = _bdot(q_dec, state) + _bdot(qk, v_new)
            state_ref[h] = state * c_dec + lax.dot_general(
                k_end.astype(BF16), v_new.astype(BF16), (((0,), (0,)), ((), ())),
                preferred_element_type=F32)
            o = _rms(o, HEAD_DIM) * ng_ref[...]
            o_ref[0, r0:r0 + C, l0:l0 + HEAD_DIM] = o * _silu(gate[r0:r0 + C, l0:l0 + HEAD_DIM])


def _seg_spec(t, width, off):
    blk = off // width
    assert blk * width == off
    return pl.BlockSpec((1, t, width), lambda b, i: (b, i, blk))


def _gdn(proj, conv_w, a_log, dt_bias, norm_g):
    b, s, _ = proj.shape
    t = min(256, s)
    pad = lambda vec, off: jnp.zeros((1, 128), F32).at[0, off:off + N_HEADS].set(vec)
    full = lambda shape: pl.BlockSpec(shape, lambda bb, i: (0,) * len(shape))
    return pl.pallas_call(
        functools.partial(_gdn_kernel, T=t),
        grid=(b, s // t),
        in_specs=[_seg_spec(t, GROUP_W, OFF_AQ), _seg_spec(t, GROUP_W, OFF_AK),
                  _seg_spec(t, GROUP_W, OFF_AV), _seg_spec(t, GROUP_W, OFF_AG),
                  _seg_spec(t, 128, OFF_MISC),
                  full((GDN_CONV, 3 * GROUP_W)), full((1, 128)), full((1, 128)), full((1, 128))],
        out_specs=pl.BlockSpec((1, t, GROUP_W), lambda bb, i: (bb, i, 0)),
        out_shape=jax.ShapeDtypeStruct((b, s, GROUP_W), F32),
        scratch_shapes=[pltpu.VMEM((3, 8, GROUP_W), F32),
                        pltpu.VMEM((N_HEADS, HEAD_DIM, HEAD_DIM), F32)],
        compiler_params=_cparams(("parallel", "arbitrary")),
        name="gdn",
    )(proj, proj, proj, proj, proj, conv_w, pad(a_log, MISC_DECAY), pad(dt_bias, MISC_DECAY),
      norm_g.reshape(1, HEAD_DIM))


def _lru_kernel(x_ref, gate_ref, cw_ref, cb_ref, wr_ref, br_ref, wi_ref, bi_ref, lam_ref,
                o_ref, hist_ref, h_ref, *, T):
    @pl.when(pl.program_id(1) == 0)
    def _():
        hist_ref[...] = jnp.zeros_like(hist_ref)
        h_ref[...] = jnp.zeros_like(h_ref)

    xin = x_ref[0]
    xc = _causal_conv4(xin, hist_ref[...], cw_ref[...]) + cb_ref[...]
    hist_ref[...] = xin[T - 8:T]
    rs, igs = [], []
    for h in range(N_HEADS):
        xh = xc[:, h * HEAD_DIM:(h + 1) * HEAD_DIM].astype(BF16)
        rs.append(_dot(xh, wr_ref[h]))
        igs.append(_dot(xh, wi_ref[h]))
    r = jax.nn.sigmoid(jnp.concatenate(rs, axis=-1) + br_ref[...])
    ig = jax.nn.sigmoid(jnp.concatenate(igs, axis=-1) + bi_ref[...])
    log_a = -LRU_C * r * _softplus(-lam_ref[...])
    a = jnp.exp(log_a)
    bb = jnp.sqrt(-jnp.tanh(log_a) * (a * a + 1.0)) * (ig * xc)
    rowi = lax.broadcasted_iota(jnp.int32, (T, GROUP_W), 0)
    d = 1
    while d < T:
        keep = rowi >= d
        a_sh = jnp.where(keep, pltpu.roll(a, d, 0), 1.0)
        b_sh = jnp.where(keep, pltpu.roll(bb, d, 0), 0.0)
        bb = a * b_sh + bb
        a = a * a_sh
        d *= 2
    hseq = bb + a * h_ref[...]
    h_ref[...] = hseq[T - 1:T]
    o_ref[0] = hseq * _silu(gate_ref[0])


def _lru(proj, conv_w, conv_b, w_r, b_r, w_i, b_i, lam):
    b, s, _ = proj.shape
    t = min(256, s)
    full = lambda shape: pl.BlockSpec(shape, lambda bb, i: (0,) * len(shape))
    vec = lambda v: v.reshape(1, GROUP_W)
    return pl.pallas_call(
        functools.partial(_lru_kernel, T=t),
        grid=(b, s // t),
        in_specs=[_seg_spec(t, GROUP_W, OFF_CX), _seg_spec(t, GROUP_W, OFF_CG),
                  full((LRU_CONV, GROUP_W)), full((1, GROUP_W)),
                  full((N_HEADS, HEAD_DIM, HEAD_DIM)), full((1, GROUP_W)),
                  full((N_HEADS, HEAD_DIM, HEAD_DIM)), full((1, GROUP_W)), full((1, GROUP_W))],
        out_specs=pl.BlockSpec((1, t, GROUP_W), lambda bb, i: (bb, i, 0)),
        out_shape=jax.ShapeDtypeStruct((b, s, GROUP_W), F32),
        scratch_shapes=[pltpu.VMEM((8, GROUP_W), F32), pltpu.VMEM((1, GROUP_W), F32)],
        compiler_params=_cparams(("parallel", "arbitrary")),
        name="rglru",
    )(proj, proj, conv_w, vec(conv_b), w_r.astype(BF16), vec(b_r), w_i.astype(BF16), vec(b_i),
      vec(lam))


def _inv_freq_row(d):
    i = np.arange(128) % (d // 2)
    return jnp.asarray((ROPE_THETA ** (-(2.0 * i) / d)).astype(np.float32).reshape(1, 128))


def _rope128(x, cos, sin_signed):
    return x * cos + pltpu.roll(x, 64, 1) * sin_signed


def _rope64(x, cos, sin_masked, lane):
    rot = jnp.where(lane < 32, -pltpu.roll(x, 96, 1), pltpu.roll(x, 32, 1))
    return x * cos + rot * sin_masked


def _mla_prep_kernel(cq_ref, ckv_ref, misc_ref, pos_ref, qng_ref, wuq_ref, kvng_ref, wukv_ref,
                     qgn_ref, qgp_ref, kgn_ref, kgp_ref, invf_ref,
                     q_ref, k_ref, v_ref, *, T):
    scale = (MLA_NOPE + MLA_ROPE) ** -0.5
    lane = lax.broadcasted_iota(jnp.int32, (T, 128), 1)
    ang = pos_ref[0] * invf_ref[...]
    cos = jnp.cos(ang)
    sin = jnp.where(lane < MLA_ROPE, jnp.sin(ang), 0.0)

    cq = _rms(cq_ref[0], MLA_Q_RANK) * qng_ref[...]
    qf = _dot(cq.astype(BF16), wuq_ref[...])
    ckv = _rms(ckv_ref[0], MLA_KV_RANK) * kvng_ref[...]
    kvf = _dot(ckv.astype(BF16), wukv_ref[...])
    kpe = jnp.where(lane < MLA_ROPE, misc_ref[0], 0.0)
    kpe = _rope64(_rms(kpe, MLA_ROPE) * kgp_ref[...], cos, sin, lane)
    for h in range(N_HEADS):
        o = h * 256
        qn = _rms(qf[:, o:o + 128], MLA_NOPE) * qgn_ref[...]
        qp = _rope64(_rms(qf[:, o + 128:o + 256], MLA_ROPE) * qgp_ref[...], cos, sin, lane)
        q_ref[0, :, o:o + 128] = (qn * scale).astype(BF16)
        q_ref[0, :, o + 128:o + 256] = (qp * scale).astype(BF16)
        kn = _rms(kvf[:, o:o + 128], MLA_NOPE) * kgn_ref[...]
        k_ref[0, :, o:o + 128] = kn.astype(BF16)
        k_ref[0, :, o + 128:o + 256] = kpe.astype(BF16)
        v_ref[0, :, h * 128:(h + 1) * 128] = kvf[:, o + 128:o + 256].astype(BF16)


def _mla_prep(proj, posb, q_norm_g, w_uq, kv_norm_g, w_ukv, qk_norm_g):
    b, s, _ = proj.shape
    t = min(512, s)
    wuq = jnp.pad(w_uq, ((0, 0), (0, 0), (0, 64))).reshape(MLA_Q_RANK, N_HEADS * 256).astype(BF16)
    wukv = w_ukv.reshape(MLA_KV_RANK, N_HEADS * 256).astype(BF16)
    pad64 = lambda v: jnp.pad(v, (0, 64)).reshape(1, 128)
    full = lambda shape: pl.BlockSpec(shape, lambda bb, i: (0,) * len(shape))
    tok = lambda w: pl.BlockSpec((1, t, w), lambda bb, i: (bb, i, 0))
    return pl.pallas_call(
        functools.partial(_mla_prep_kernel, T=t),
        grid=(b, s // t),
        in_specs=[_seg_spec(t, MLA_Q_RANK, OFF_BCQ), _seg_spec(t, MLA_KV_RANK, OFF_BCKV),
                  _seg_spec(t, 128, OFF_MISC), tok(128),
                  full((1, MLA_Q_RANK)), full((MLA_Q_RANK, 1024)),
                  full((1, MLA_KV_RANK)), full((MLA_KV_RANK, 1024)),
                  full((1, 128)), full((1, 128)), full((1, 128)), full((1, 128)), full((1, 128))],
        out_specs=[tok(1024), tok(1024), tok(512)],
        out_shape=[jax.ShapeDtypeStruct((b, s, 1024), BF16),
                   jax.ShapeDtypeStruct((b, s, 1024), BF16),
                   jax.ShapeDtypeStruct((b, s, 512), BF16)],
        compiler_params=_cparams(("parallel", "parallel")),
        name="mla_prep",
    )(proj, proj, proj, posb, q_norm_g.reshape(1, -1), wuq, kv_norm_g.reshape(1, -1), wukv,
      qk_norm_g[0, :MLA_NOPE].reshape(1, 128), pad64(qk_norm_g[0, MLA_NOPE:]),
      qk_norm_g[1, :MLA_NOPE].reshape(1, 128), pad64(qk_norm_g[1, MLA_NOPE:]),
      _inv_freq_row(MLA_ROPE))


def _softmax_init(m_ref, l_ref, acc_ref):
    m_ref[...] = jnp.full_like(m_ref, NEG)
    l_ref[...] = jnp.zeros_like(l_ref)
    acc_ref[...] = jnp.zeros_like(acc_ref)


def _softmax_accumulate(s, v_blk, m_ref, l_ref, acc_ref):
    m_old = m_ref[...]
    m_new = jnp.maximum(m_old, jnp.max(s, axis=-1, keepdims=True))
    alpha = jnp.exp(m_old - m_new)
    p = jnp.exp(s - m_new)
    part = p[:, 0:128]
    for g in range(1, s.shape[1] // 128):
        part = part + p[:, g * 128:(g + 1) * 128]
    l_ref[...] = alpha * l_ref[...] + part
    acc_ref[...] = alpha * acc_ref[...] + _dot(p.astype(BF16), v_blk)
    m_ref[...] = m_new


def _softmax_result(l_ref, acc_ref):
    return acc_ref[...] / jnp.sum(l_ref[...], axis=-1, keepdims=True)


def _mla_flash_kernel(q_ref, k_ref, v_ref, gate_ref, o_ref, m_ref, l_ref, acc_ref, *, tq, kc):
    i = pl.program_id(2)
    _softmax_init(m_ref, l_ref, acc_ref)
    q = q_ref[0]

    def chunk(c, masked):
        k0 = pl.multiple_of(c * kc, kc)
        s = _dot_nt(q, k_ref[0, pl.ds(k0, kc), :])
        if masked:
            row = lax.broadcasted_iota(jnp.int32, (tq, kc), 0) + i * tq
            col = lax.broadcasted_iota(jnp.int32, (tq, kc), 1) + k0
            s = jnp.where(col <= row, s, NEG)
        _softmax_accumulate(s, v_ref[0, pl.ds(k0, kc), :], m_ref, l_ref, acc_ref)

    def full_chunk(c, carry):
        chunk(c, False)
        return carry

    n_full = (i * tq) // kc
    lax.fori_loop(0, n_full, full_chunk, 0)
    chunk(n_full, True)
    o_ref[0] = _softmax_result(l_ref, acc_ref) * _silu(gate_ref[0])


def _mla_flash(q, k, v, proj):
    b, s, _ = q.shape
    tq = min(512, s)
    kc = min(1024, s)
    gate_blk = OFF_BG // 128
    return pl.pallas_call(
        functools.partial(_mla_flash_kernel, tq=tq, kc=kc),
        grid=(b, N_HEADS, s // tq),
        in_specs=[pl.BlockSpec((1, tq, 256), lambda bb, h, i: (bb, i, h)),
                  pl.BlockSpec((1, s, 256), lambda bb, h, i: (bb, 0, h)),
                  pl.BlockSpec((1, s, 128), lambda bb, h, i: (bb, 0, h)),
                  pl.BlockSpec((1, tq, 128), lambda bb, h, i: (bb, i, gate_blk + h))],
        out_specs=pl.BlockSpec((1, tq, 128), lambda bb, h, i: (bb, i, h)),
        out_shape=jax.ShapeDtypeStruct((b, s, GROUP_W), F32),
        scratch_shapes=[pltpu.VMEM((tq, 1), F32), pltpu.VMEM((tq, 128), F32),
                        pltpu.VMEM((tq, 128), F32)],
        compiler_params=_cparams(("parallel", "parallel", "arbitrary")),
        name="mla_flash",
    )(q, k, v, proj)


def _nsa_prep_kernel(dq_ref, dkv_ref, pos_ref, qg_ref, kg_ref, invf_ref,
                     q_ref, ks_ref, vs_ref, kw_ref, vw_ref, kc_ref, vc_ref, *, T):
    scale = HEAD_DIM ** -0.5
    lane = lax.broadcasted_iota(jnp.int32, (T, 128), 1)
    ang = pos_ref[0] * invf_ref[...]
    cos = jnp.cos(ang)
    sin = jnp.sin(ang)
    sin = jnp.where(lane < 64, -sin, sin)
    dq = dq_ref[0]
    for h in range(N_HEADS):
        qh = _rms(dq[:, h * 128:(h + 1) * 128], HEAD_DIM) * qg_ref[...]
        q_ref[0, :, h * 128:(h + 1) * 128] = _rope128(qh, cos, sin) * scale
    kv = dkv_ref[0]
    kc_ref[0] = kv[:, 0:128]
    vc_ref[0] = kv[:, 128:256]
    ks_ref[0] = _rope128(_rms(kv[:, 256:384], HEAD_DIM) * kg_ref[1:2], cos, sin).astype(BF16)
    vs_ref[0] = kv[:, 384:512].astype(BF16)
    kw_ref[0] = _rope128(_rms(kv[:, 512:640], HEAD_DIM) * kg_ref[2:3], cos, sin).astype(BF16)
    vw_ref[0] = kv[:, 640:768].astype(BF16)


def _nsa_prep(proj, posb, q_norm_g, k_norm_g):
    b, s, _ = proj.shape
    t = min(512, s)
    full = lambda shape: pl.BlockSpec(shape, lambda bb, i: (0,) * len(shape))
    tok = lambda w: pl.BlockSpec((1, t, w), lambda bb, i: (bb, i, 0))
    sds = lambda w, dt: jax.ShapeDtypeStruct((b, s, w), dt)
    return pl.pallas_call(
        functools.partial(_nsa_prep_kernel, T=t),
        grid=(b, s // t),
        in_specs=[_seg_spec(t, GROUP_W, OFF_DQ), _seg_spec(t, 768, OFF_DKV), tok(128),
                  full((1, 128)), full((3, 128)), full((1, 128))],
        out_specs=[tok(512)] + [tok(128)] * 6,
        out_shape=[sds(512, F32), sds(128, BF16), sds(128, BF16), sds(128, BF16), sds(128, BF16),
                   sds(128, F32), sds(128, F32)],
        compiler_params=_cparams(("parallel", "parallel")),
        name="nsa_prep",
    )(proj, proj, posb, q_norm_g.reshape(1, HEAD_DIM), k_norm_g, _inv_freq_row(HEAD_DIM))


def _nsa_cmp_kernel(kt_ref, vt_ref, pe_ref, w1_ref, b1_ref, w2_ref, b2_ref, kg_ref, pos_ref,
                    invf_ref, kc_ref, vc_ref, *, NC):
    half = CMP_STRIDE * HEAD_DIM
    outs = []
    for j, t_ref in enumerate((kt_ref, vt_ref)):
        t2 = t_ref[0].astype(BF16)
        first = _dot(t2, w1_ref[j, :half, :])
        second = pltpu.roll(_dot(t2, w1_ref[j, half:, :]), NC - 1, 0)
        pe8 = jnp.broadcast_to(pe_ref[j], (8, CMP_LEN * HEAD_DIM)).astype(BF16)
        bias = _dot(pe8, w1_ref[j])[0:1] + b1_ref[j]
        hid = _silu(first + second + bias)
        outs.append(_dot(hid.astype(BF16), w2_ref[j]) + b2_ref[j])
    k_c, v_c = outs
    lane = lax.broadcasted_iota(jnp.int32, (NC, 128), 1)
    ang = pos_ref[0] * invf_ref[...]
    sin = jnp.sin(ang)
    sin = jnp.where(lane < 64, -sin, sin)
    kc_ref[0] = _rope128(_rms(k_c, HEAD_DIM) * kg_ref[...], jnp.cos(ang), sin)
    vc_ref[0] = v_c


def _nsa_compress(kc_raw, vc_raw, posc, cmp_pe, cmp_w1, cmp_b1, cmp_w2, cmp_b2, kg0):
    b, s, _ = kc_raw.shape
    nc = s // CMP_STRIDE
    kt = kc_raw.reshape(b, nc, CMP_STRIDE * HEAD_DIM)
    vt = vc_raw.reshape(b, nc, CMP_STRIDE * HEAD_DIM)
    full = lambda shape: pl.BlockSpec(shape, lambda bb: (0,) * len(shape))
    per_b = lambda shape: pl.BlockSpec((1,) + shape, lambda bb: (bb, 0, 0))
    return pl.pallas_call(
        functools.partial(_nsa_cmp_kernel, NC=nc),
        grid=(b,),
        in_specs=[per_b((nc, CMP_STRIDE * HEAD_DIM)), per_b((nc, CMP_STRIDE * HEAD_DIM)),
                  full((2, 1, CMP_LEN * HEAD_DIM)), full((2, CMP_LEN * HEAD_DIM, CMP_HIDDEN)),
                  full((2, 1, CMP_HIDDEN)), full((2, CMP_HIDDEN, HEAD_DIM)),
                  full((2, 1, HEAD_DIM)), full((1, HEAD_DIM)), per_b((nc, 128)), full((1, 128))],
        out_specs=[per_b((nc, HEAD_DIM)), per_b((nc, HEAD_DIM))],
        out_shape=[jax.ShapeDtypeStruct((b, nc, HEAD_DIM), F32),
                   jax.ShapeDtypeStruct((b, nc, HEAD_DIM), F32)],
        compiler_params=_cparams(("parallel",)),
        name="nsa_compress",
    )(kt, vt, cmp_pe.reshape(2, 1, CMP_LEN * HEAD_DIM), cmp_w1.astype(BF16),
      cmp_b1.reshape(2, 1, CMP_HIDDEN), cmp_w2.astype(BF16), cmp_b2.reshape(2, 1, HEAD_DIM),
      kg0.reshape(1, HEAD_DIM), posc, _inv_freq_row(HEAD_DIM))


def _nsa_attn_kernel(q_ref, kc_ref, vc_ref, ks_ref, vs_ref, kw_ref, vw_ref, misc_ref, gate_ref,
                     o_ref, m_ref, l_ref, acc_ref, *, S, NC, KC, WK):
    Q = Q_BLOCK
    R = N_HEADS
    i = pl.program_id(1)
    n_sel = S // SEL_LEN
    n_cmp = (S - CMP_LEN) // CMP_STRIDE + 1
    top_k = min(SEL_TOPK, n_sel)

    qf = q_ref[0]
    q4 = jnp.concatenate([qf[:, r * 128:(r + 1) * 128] for r in range(R)], axis=0)
    q4b = q4.astype(BF16)

    s_c = _dot_nt(q4, kc_ref[0], HIGHEST)
    t_c = i * Q + (lax.broadcasted_iota(jnp.int32, (R * Q, NC), 0) & (Q - 1))
    c_ix = lax.broadcasted_iota(jnp.int32, (R * Q, NC), 1)
    valid_c = (c_ix * CMP_STRIDE + (CMP_LEN - 1) <= t_c) & (c_ix < n_cmp)
    s_c = jnp.where(valid_c, s_c, NEG)
    p_c = jnp.where(valid_c, jnp.exp(s_c - jnp.max(s_c, axis=-1, keepdims=True)), 0.0)
    p_c = p_c / jnp.maximum(jnp.sum(p_c, axis=-1, keepdims=True), 1e-30)
    o_c = _bdot(p_c, vc_ref[0])
    n_o = lax.broadcasted_iota(jnp.int32, (128, NC), 0) * SEL_LEN
    c_o = lax.broadcasted_iota(jnp.int32, (128, NC), 1) * CMP_STRIDE
    overlap_t = ((c_o < n_o + SEL_LEN) & (c_o + (CMP_LEN - 1) >= n_o)).astype(F32)
    imp4 = _dot_nt(overlap_t, p_c, HIGHEST)
    imp = imp4[:, 0:Q] + imp4[:, Q:2 * Q] + imp4[:, 2 * Q:3 * Q] + imp4[:, 3 * Q:4 * Q]

    NP = min(128, -(-n_sel // 8) * 8)
    t_q = i * Q + lax.broadcasted_iota(jnp.int32, (NP, Q), 1)
    n_ix = lax.broadcasted_iota(jnp.int32, (NP, Q), 0)
    cur = t_q >> 6
    valid_s = (n_ix * SEL_LEN <= t_q) & (n_ix < n_sel)
    forced = (n_ix == 0) | (n_ix == cur) | (n_ix == cur - 1)
    val = jnp.where(valid_s, imp[:NP], -1.0)
    val = jnp.where(forced & valid_s, FORCE, val)
    val = jnp.where(n_ix < n_sel, val, -2.0)
    sel_t = jnp.zeros((NP, Q), F32)
    n_f = n_ix.astype(F32)
    for _ in range(top_k):
        mx = jnp.max(val, axis=0, keepdims=True)
        first = jnp.min(jnp.where(val == mx, n_f, 1e9), axis=0, keepdims=True)
        hit = n_f == first
        sel_t = jnp.where(hit, 1.0, sel_t)
        val = jnp.where(hit, -3.0, val)
    sel_t = jnp.where(valid_s, sel_t, 0.0)
    if NP < 128:
        sel_t = jnp.concatenate([sel_t, jnp.zeros((128 - NP, Q), F32)], axis=0)
    sel_b = sel_t.T.astype(BF16)

    _softmax_init(m_ref, l_ref, acc_ref)
    t_k = i * Q + lax.broadcasted_iota(jnp.int32, (Q, KC), 0)
    col_k = lax.broadcasted_iota(jnp.int32, (Q, KC), 1)
    blk_row = lax.broadcasted_iota(jnp.int32, (128, KC), 0)
    blk_col = lax.broadcasted_iota(jnp.int32, (128, KC), 1) >> 6

    def sel_chunk(c, carry):
        k0 = pl.multiple_of(c * KC, KC)
        expand = (blk_row == blk_col + c * (KC // SEL_LEN)).astype(BF16)
        picked = _dot(sel_b, expand)
        ok = (picked > 0.5) & (col_k + k0 <= t_k)
        bias = jnp.where(ok, 0.0, NEG)
        s = _dot_nt(q4b, ks_ref[0, pl.ds(k0, KC), :]) + jnp.concatenate([bias] * R, axis=0)
        _softmax_accumulate(s, vs_ref[0, pl.ds(k0, KC), :], m_ref, l_ref, acc_ref)
        return carry

    n_chunks = ((i + 1) * Q + KC - 1) // KC
    lax.fori_loop(0, n_chunks, sel_chunk, 0)
    o_s = _softmax_result(l_ref, acc_ref)

    w0 = pl.multiple_of(jnp.maximum(i * Q - WINDOW, 0), Q)
    s_w = _dot_nt(q4b, kw_ref[0, pl.ds(w0, WK), :])
    t_w = i * Q + (lax.broadcasted_iota(jnp.int32, (R * Q, WK), 0) & (Q - 1))
    kpos = w0 + lax.broadcasted_iota(jnp.int32, (R * Q, WK), 1)
    s_w = jnp.where((kpos <= t_w) & (kpos > t_w - WINDOW), s_w, NEG)
    p_w = jnp.exp(s_w - jnp.max(s_w, axis=-1, keepdims=True))
    p_w = p_w / jnp.sum(p_w, axis=-1, keepdims=True)
    o_w = _dot(p_w.astype(BF16), vw_ref[0, pl.ds(w0, WK), :])

    gates = jax.nn.sigmoid(misc_ref[0])
    gate = gate_ref[0]
    for r in range(R):
        g0 = gates[:, MISC_GL + 3 * r:MISC_GL + 3 * r + 1]
        g1 = gates[:, MISC_GL + 3 * r + 1:MISC_GL + 3 * r + 2]
        g2 = gates[:, MISC_GL + 3 * r + 2:MISC_GL + 3 * r + 3]
        rows = slice(r * Q, (r + 1) * Q)
        o = g0 * o_c[rows] + g1 * o_s[rows] + g2 * o_w[rows]
        o_ref[0, :, r * 128:(r + 1) * 128] = o * _silu(gate[:, r * 128:(r + 1) * 128])


def _nsa_attn(qf, k_c, v_c, ks, vs, kw, vw, proj):
    b, s, _ = qf.shape
    nc = s // CMP_STRIDE
    kc_len = min(1024, s)
    wk = min(WINDOW + Q_BLOCK, s)
    per_b = lambda n, w: pl.BlockSpec((1, n, w), lambda bb, i: (bb, 0, 0))
    return pl.pallas_call(
        functools.partial(_nsa_attn_kernel, S=s, NC=nc, KC=kc_len, WK=wk),
        grid=(b, s // Q_BLOCK),
        in_specs=[pl.BlockSpec((1, Q_BLOCK, GROUP_W), lambda bb, i: (bb, i, 0)),
                  per_b(nc, 128), per_b(nc, 128),
                  per_b(s, 128), per_b(s, 128), per_b(s, 128), per_b(s, 128),
                  _seg_spec(Q_BLOCK, 128, OFF_MISC), _seg_spec(Q_BLOCK, GROUP_W, OFF_DG)],
        out_specs=pl.BlockSpec((1, Q_BLOCK, GROUP_W), lambda bb, i: (bb, i, 0)),
        out_shape=jax.ShapeDtypeStruct((b, s, GROUP_W), F32),
        scratch_shapes=[pltpu.VMEM((N_HEADS * Q_BLOCK, 1), F32),
                        pltpu.VMEM((N_HEADS * Q_BLOCK, 128), F32),
                        pltpu.VMEM((N_HEADS * Q_BLOCK, 128), F32)],
        compiler_params=_cparams(("parallel", "arbitrary")),
        name="nsa_attn",
    )(qf, k_c, v_c, ks, vs, kw, vw, proj, proj)


def _reorder_w_in(w_in):
    o = np.concatenate([[0], np.cumsum(IN_SIZES)]).tolist()
    seg = lambda k: w_in[:, :, o[k]:o[k + 1]]
    (a_qkv, a_decay, a_beta, a_gate, b_cq, b_ckv, b_kpe, b_gate, c_x, c_gate,
     d_q, d_kv, d_gl, d_gate) = [seg(k) for k in range(len(IN_SIZES))]
    zeros = jnp.zeros(w_in.shape[:2] + (128 - MLA_ROPE - 2 * N_HEADS - 3 * N_HEADS,), w_in.dtype)
    out = jnp.concatenate([a_qkv, a_gate, b_gate, c_x, c_gate, d_q, d_gate, d_kv, b_cq,
                           b_kpe, a_decay, a_beta, d_gl, zeros, b_ckv], axis=-1)
    assert out.shape[-1] == D_PROJ
    return out.astype(BF16)


def kernel(x, positions, norm_g, w_in, w_out, gdn_conv_w, gdn_a_log, gdn_dt_bias, gdn_norm_g,
           mla_q_norm_g, mla_w_uq, mla_kv_norm_g, mla_w_ukv, mla_qk_norm_g,
           lru_conv_w, lru_conv_b, lru_w_r, lru_b_r, lru_w_i, lru_b_i, lru_lambda,
           nsa_q_norm_g, nsa_k_norm_g, nsa_cmp_pe, nsa_cmp_w1, nsa_cmp_b1, nsa_cmp_w2, nsa_cmp_b2):
    b, s, d = x.shape
    depth = w_in.shape[0]
    w_in_r = _reorder_w_in(w_in)
    w_out_b = w_out.astype(BF16)
    posf = positions.astype(F32)
    posb = jnp.broadcast_to(posf[:, :, None], (b, s, 128))
    nc = s // CMP_STRIDE
    pos_end = jnp.pad(posf.reshape(b, nc, CMP_STRIDE)[:, 1:, CMP_STRIDE - 1], ((0, 0), (0, 1)))
    posc = jnp.broadcast_to(pos_end[:, :, None], (b, nc, 128))

    x2d = x.reshape(b * s, d)
    for l in range(depth):
        proj = _inproj(x2d, norm_g[l].reshape(1, d), w_in_r[l]).reshape(b, s, D_PROJ)
        y_a = _gdn(proj, gdn_conv_w[l], gdn_a_log[l], gdn_dt_bias[l], gdn_norm_g[l])
        q_b, k_b, v_b = _mla_prep(proj, posb, mla_q_norm_g[l], mla_w_uq[l], mla_kv_norm_g[l],
                                  mla_w_ukv[l], mla_qk_norm_g[l])
        y_b = _mla_flash(q_b, k_b, v_b, proj)
        y_c = _lru(proj, lru_conv_w[l], lru_conv_b[l], lru_w_r[l], lru_b_r[l], lru_w_i[l],
                   lru_b_i[l], lru_lambda[l])
        q_d, ks, vs, kw, vw, kc_raw, vc_raw = _nsa_prep(proj, posb, nsa_q_norm_g[l],
                                                        nsa_k_norm_g[l])
        k_c, v_c = _nsa_compress(kc_raw, vc_raw, posc, nsa_cmp_pe[l], nsa_cmp_w1[l],
                                 nsa_cmp_b1[l], nsa_cmp_w2[l], nsa_cmp_b2[l], nsa_k_norm_g[l, 0])
        y_d = _nsa_attn(q_d, k_c, v_c, ks, vs, kw, vw, proj)
        ys = [y.reshape(b * s, GROUP_W) for y in (y_a, y_b, y_c, y_d)]
        x2d = _outproj(ys, w_out_b[l], x2d)
    return x2d.reshape(b, s, d)
```

```python
import functools
import math

import numpy as np
import jax
import jax.numpy as jnp
from jax import lax
from jax.experimental import pallas as pl
from jax.experimental.pallas import tpu as pltpu

F32 = jnp.float32
BF16 = jnp.bfloat16
HIGHEST = lax.Precision.HIGHEST

D_MODEL = 2048
GROUP_W = 512
HEAD_DIM = 128
N_HEADS = 4
RMS_EPS = 1e-6
ROPE_THETA = 10000.0
NEG = -1e30
LOG2E = math.log2(math.e)
GDN_CONV = 4
GDN_CHUNK = 128
MLA_Q_RANK = 384
MLA_KV_RANK = 256
MLA_NOPE = 128
MLA_ROPE = 64
LRU_CONV = 4
LRU_C = 8.0
CMP_LEN = 32
CMP_STRIDE = 16
CMP_HIDDEN = 256
SEL_LEN = 64
SEL_TOPK = 16
WINDOW = 512
FORCE = 1e9
Q_BLOCK = 128

IN_SIZES = (3 * GROUP_W, N_HEADS, N_HEADS, GROUP_W,
            MLA_Q_RANK, MLA_KV_RANK, MLA_ROPE, GROUP_W,
            GROUP_W, GROUP_W,
            GROUP_W, 6 * HEAD_DIM, 3 * N_HEADS, GROUP_W)

OFF_AQ, OFF_AK, OFF_AV, OFF_AG = 0, 512, 1024, 1536
OFF_BG, OFF_CX, OFF_CG, OFF_DQ, OFF_DG = 2048, 2560, 3072, 3584, 4096
OFF_DKV, OFF_BCQ, OFF_MISC, OFF_BCKV = 4608, 5376, 5760, 5888
D_PROJ = 6144
MISC_KPE, MISC_DECAY, MISC_BETA, MISC_GL = 0, 64, 68, 72

VMEM_LIMIT = 56 * 1024 * 1024


def _cparams(sem):
    return pltpu.CompilerParams(dimension_semantics=sem, vmem_limit_bytes=VMEM_LIMIT)


def _dot(a, b, precision=None):
    return lax.dot_general(a, b, (((1,), (0,)), ((), ())), precision=precision,
                           preferred_element_type=F32)


def _dot_nt(a, b, precision=None):
    return lax.dot_general(a, b, (((1,), (1,)), ((), ())), precision=precision,
                           preferred_element_type=F32)


def _bdot(a, b):
    return _dot(a.astype(BF16), b.astype(BF16))


def _bdot_nt(a, b):
    return _dot_nt(a.astype(BF16), b.astype(BF16))


def _split_bf16(x):
    hi = x.astype(BF16)
    return hi, (x - hi.astype(F32)).astype(BF16)


def _silu(x):
    return x * jax.nn.sigmoid(x)


def _softplus(x):
    return jnp.maximum(x, 0.0) + jnp.log1p(jnp.exp(-jnp.abs(x)))


def _rms(x, n):
    return x * lax.rsqrt(jnp.sum(x * x, axis=-1, keepdims=True) * (1.0 / n) + RMS_EPS)


def _inproj_kernel(x_ref, g_ref, w_ref, o_ref, h_ref, *, tm):
    @pl.when(pl.program_id(1) == 0)
    def _():
        rows = min(256, tm)
        for r in range(tm // rows):
            x = x_ref[r * rows:(r + 1) * rows, :]
            h_ref[r * rows:(r + 1) * rows, :] = (_rms(x, D_MODEL) * g_ref[...]).astype(BF16)

    o_ref[...] = _dot(h_ref[...], w_ref[...])


def _inproj(x2d, g, w):
    m = x2d.shape[0]
    tm = min(1024, m)
    tn = 512
    return pl.pallas_call(
        functools.partial(_inproj_kernel, tm=tm),
        grid=(m // tm, D_PROJ // tn),
        in_specs=[pl.BlockSpec((tm, D_MODEL), lambda i, j: (i, 0)),
                  pl.BlockSpec((1, D_MODEL), lambda i, j: (0, 0)),
                  pl.BlockSpec((D_MODEL, tn), lambda i, j: (0, j))],
        out_specs=pl.BlockSpec((tm, tn), lambda i, j: (i, j)),
        out_shape=jax.ShapeDtypeStruct((m, D_PROJ), F32),
        scratch_shapes=[pltpu.VMEM((tm, D_MODEL), BF16)],
        compiler_params=_cparams(("parallel", "arbitrary")),
        name="inproj",
    )(x2d, g, w)


def _outproj_kernel(ya_ref, yb_ref, yc_ref, yd_ref, wa_ref, wb_ref, wc_ref, wd_ref, x_ref, o_ref):
    acc = x_ref[...]
    for y_ref, w_ref in ((ya_ref, wa_ref), (yb_ref, wb_ref), (yc_ref, wc_ref), (yd_ref, wd_ref)):
        acc = acc + _dot(y_ref[...].astype(BF16), w_ref[...])
    o_ref[...] = acc


def _outproj(ys, w, x2d):
    m = x2d.shape[0]
    tm = min(1024, m)
    tn = 512
    y_specs = [pl.BlockSpec((tm, GROUP_W), lambda i, j: (i, 0)) for _ in range(4)]
    w_specs = [pl.BlockSpec((GROUP_W, tn), functools.partial(lambda i, j, g: (g, j), g=g))
               for g in range(4)]
    return pl.pallas_call(
        _outproj_kernel,
        grid=(m // tm, D_MODEL // tn),
        in_specs=y_specs + w_specs + [pl.BlockSpec((tm, tn), lambda i, j: (i, j))],
        out_specs=pl.BlockSpec((tm, tn), lambda i, j: (i, j)),
        out_shape=jax.ShapeDtypeStruct((m, D_MODEL), F32),
        compiler_params=_cparams(("parallel", "arbitrary")),
        name="outproj",
    )(*ys, w, w, w, w, x2d)


def _causal_conv4(xin, ext_ref, w):
    t = xin.shape[0]
    ext_ref[8:8 + t, :] = xin
    acc = ext_ref[5:5 + t, :] * w[0:1]
    for j in range(1, 4):
        acc = acc + ext_ref[5 + j:5 + j + t, :] * w[j:j + 1]
    ext_ref[0:8, :] = xin[t - 8:t]
    return acc


def _gdn_kernel(q_ref, k_ref, v_ref, gate_ref, misc_ref, cw_ref, alog_ref, dtb_ref, ng_ref,
                o_ref, hist_ref, state_ref, *, T):
    C = GDN_CHUNK

    @pl.when(pl.program_id(1) == 0)
    def _():
        hist_ref[:, 0:8, :] = jnp.zeros((3, 8, GROUP_W), F32)
        state_ref[...] = jnp.zeros_like(state_ref)

    cw = cw_ref[...]
    conv = []
    for idx, ref in enumerate((q_ref, k_ref, v_ref)):
        xin = ref[0]
        acc = _causal_conv4(xin, hist_ref.at[idx], cw[:, idx * GROUP_W:(idx + 1) * GROUP_W])
        conv.append(_silu(acc))
    qc, kc, vc = conv

    misc = misc_ref[0]
    gfull = -jnp.exp(alog_ref[...]) * _softplus(misc + dtb_ref[...])
    betafull = jax.nn.sigmoid(misc)
    gate = gate_ref[0]

    row = lax.broadcasted_iota(jnp.int32, (C, C), 0)
    col = lax.broadcasted_iota(jnp.int32, (C, C), 1)
    lower = row >= col
    strict = row > col
    ltri = lower.astype(F32)
    row2 = lax.broadcasted_iota(jnp.int32, (C, 2 * C), 0)
    lane2 = lax.broadcasted_iota(jnp.int32, (C, 2 * C), 1)
    left = lane2 < C
    col2 = lane2 & (C - 1)
    leaf_bits = 4
    same_blk = {bits: (row2 >> bits) == (col2 >> bits) for bits in range(leaf_bits, 8)}
    assert C == HEAD_DIM == 1 << 7

    def pairdot(x, y):
        yb = y.astype(BF16)
        zero = jnp.zeros_like(yb)
        bd = jnp.concatenate([jnp.where(left, yb, zero), jnp.where(left, zero, yb)], axis=0)
        return _dot(x.astype(BF16), bd)

    n_chunks = T // C
    hds = []
    for c in range(n_chunks):
        r0 = c * C
        gall = _dot(ltri, gfull[r0:r0 + C], HIGHEST)
        hd = []
        hds.append(hd)
        for h in range(N_HEADS):
            l0 = h * HEAD_DIM
            q = qc[r0:r0 + C, l0:l0 + HEAD_DIM]
            k = kc[r0:r0 + C, l0:l0 + HEAD_DIM]
            v = vc[r0:r0 + C, l0:l0 + HEAD_DIM]
            q = q * lax.rsqrt(jnp.sum(q * q, axis=-1, keepdims=True) + RMS_EPS) * (HEAD_DIM ** -0.5)
            k = k * lax.rsqrt(jnp.sum(k * k, axis=-1, keepdims=True) + RMS_EPS)
            beta = betafull[r0:r0 + C, MISC_BETA + h:MISC_BETA + h + 1]
            gc = jnp.broadcast_to(gall[:, MISC_DECAY + h:MISC_DECAY + h + 1], (C, C))
            decay = jnp.exp(jnp.where(lower, gc - gc.T, NEG))
            kb = k * beta
            a = jnp.where(strict, _bdot_nt(kb, k) * decay, 0.0)
            hd.append((q, k, v, beta, gc, decay, kb, a))

    a2s = [jnp.concatenate([hds[c][h0][7], hds[c][h0 + 1][7]], axis=1)
           for c in range(n_chunks) for h0 in range(0, N_HEADS, 2)]
    ps = [jnp.where(same_blk[leaf_bits], -a2, 0.0) for a2 in a2s]
    tm2 = list(ps)
    for _ in range(leaf_bits - 1):
        ps = [pairdot(p, p) for p in ps]
        tm2 = [tm + p + pairdot(tm, p) for tm, p in zip(tm2, ps)]
    for bits in range(leaf_bits + 1, 8):
        join = same_blk[bits] & jnp.logical_not(same_blk[bits - 1])
        tas = [jnp.where(join, a2, 0.0) for a2 in a2s]
        tas = [off + pairdot(tm, off) for tm, off in zip(tm2, tas)]
        tm2 = [tm - (ta + pairdot(ta, tm)) for tm, ta in zip(tm2, tas)]

    for c in range(n_chunks):
        r0 = c * C
        hd = hds[c]
        tms = []
        for tm in tm2[2 * c:2 * c + 2]:
            tms += [tm[:, :C], tm[:, C:]]
        for h in range(N_HEADS):
            l0 = h * HEAD_DIM
            q, k, v, beta, gc, decay, kb, _ = hd[h]
            eg = jnp.exp(gc)
            rhs = jnp.concatenate([v * beta, kb * eg], axis=-1)
            uw = rhs + _bdot(tms[h], rhs)
            u = uw[:, :HEAD_DIM]
            w = uw[:, HEAD_DIM:]
            qk = _bdot_nt(q, k) * decay
            q_dec = q * eg
            g_last = gc[C - 1:C, :]
            k_end = k * jnp.exp(g_last - gc)
            c_dec = jnp.exp(g_last)
            state = state_ref[h]
            v_new = u - _bdot(w, state)
            o = _bdot(q_dec, state) + _bdot(qk, v_new)
            state_ref[h] = state * c_dec + lax.dot_general(
                k_end.astype(BF16), v_new.astype(BF16), (((0,), (0,)), ((), ())),
                preferred_element_type=F32)
            o = _rms(o, HEAD_DIM) * ng_ref[...]
            o_ref[0, r0:r0 + C, l0:l0 + HEAD_DIM] = o * _silu(gate[r0:r0 + C, l0:l0 + HEAD_DIM])


def _seg_spec(t, width, off):
    blk = off // width
    assert blk * width == off
    return pl.BlockSpec((1, t, width), lambda b, i: (b, i, blk))


def _gdn(proj, conv_w, a_log, dt_bias, norm_g):
    b, s, _ = proj.shape
    t = min(512, s)
    pad = lambda vec, off: jnp.zeros((1, 128), F32).at[0, off:off + N_HEADS].set(vec)
    full = lambda shape: pl.BlockSpec(shape, lambda bb, i: (0,) * len(shape))
    return pl.pallas_call(
        functools.partial(_gdn_kernel, T=t),
        grid=(b, s // t),
        in_specs=[_seg_spec(t, GROUP_W, OFF_AQ), _seg_spec(t, GROUP_W, OFF_AK),
                  _seg_spec(t, GROUP_W, OFF_AV), _seg_spec(t, GROUP_W, OFF_AG),
                  _seg_spec(t, 128, OFF_MISC),
                  full((GDN_CONV, 3 * GROUP_W)), full((1, 128)), full((1, 128)), full((1, 128))],
        out_specs=pl.BlockSpec((1, t, GROUP_W), lambda bb, i: (bb, i, 0)),
        out_shape=jax.ShapeDtypeStruct((b, s, GROUP_W), F32),
        scratch_shapes=[pltpu.VMEM((3, t + 8, GROUP_W), F32),
                        pltpu.VMEM((N_HEADS, HEAD_DIM, HEAD_DIM), F32)],
        compiler_params=_cparams(("parallel", "arbitrary")),
        name="gdn",
    )(proj, proj, proj, proj, proj, conv_w, pad(a_log, MISC_DECAY), pad(dt_bias, MISC_DECAY),
      norm_g.reshape(1, HEAD_DIM))


def _lru_kernel(x_ref, gate_ref, cw_ref, cb_ref, wr_ref, br_ref, wi_ref, bi_ref, lam_ref,
                o_ref, hist_ref, h_ref, *, T):
    @pl.when(pl.program_id(1) == 0)
    def _():
        hist_ref[0:8, :] = jnp.zeros((8, GROUP_W), F32)
        h_ref[...] = jnp.zeros_like(h_ref)

    xin = x_ref[0]
    xc = _causal_conv4(xin, hist_ref, cw_ref[...]) + cb_ref[...]
    rs, igs = [], []
    for h in range(N_HEADS):
        xh = xc[:, h * HEAD_DIM:(h + 1) * HEAD_DIM].astype(BF16)
        rs.append(_dot(xh, wr_ref[h]))
        igs.append(_dot(xh, wi_ref[h]))
    r = jax.nn.sigmoid(jnp.concatenate(rs, axis=-1) + br_ref[...])
    ig = jax.nn.sigmoid(jnp.concatenate(igs, axis=-1) + bi_ref[...])
    log_a = -LRU_C * r * _softplus(-lam_ref[...])
    a = jnp.exp(log_a)
    bb = jnp.sqrt(-jnp.tanh(log_a) * (a * a + 1.0)) * (ig * xc)
    rowi = lax.broadcasted_iota(jnp.int32, (T, GROUP_W), 0)
    d = 1
    while d < T:
        keep = rowi >= d
        a_sh = jnp.where(keep, pltpu.roll(a, d, 0), 1.0)
        b_sh = jnp.where(keep, pltpu.roll(bb, d, 0), 0.0)
        bb = a * b_sh + bb
        a = a * a_sh
        d *= 2
    hseq = bb + a * h_ref[...]
    h_ref[...] = hseq[T - 1:T]
    o_ref[0] = hseq * _silu(gate_ref[0])


def _lru(proj, conv_w, conv_b, w_r, b_r, w_i, b_i, lam):
    b, s, _ = proj.shape
    t = min(256, s)
    full = lambda shape: pl.BlockSpec(shape, lambda bb, i: (0,) * len(shape))
    vec = lambda v: v.reshape(1, GROUP_W)
    return pl.pallas_call(
        functools.partial(_lru_kernel, T=t),
        grid=(b, s // t),
        in_specs=[_seg_spec(t, GROUP_W, OFF_CX), _seg_spec(t, GROUP_W, OFF_CG),
                  full((LRU_CONV, GROUP_W)), full((1, GROUP_W)),
                  full((N_HEADS, HEAD_DIM, HEAD_DIM)), full((1, GROUP_W)),
                  full((N_HEADS, HEAD_DIM, HEAD_DIM)), full((1, GROUP_W)), full((1, GROUP_W))],
        out_specs=pl.BlockSpec((1, t, GROUP_W), lambda bb, i: (bb, i, 0)),
        out_shape=jax.ShapeDtypeStruct((b, s, GROUP_W), F32),
        scratch_shapes=[pltpu.VMEM((t + 8, GROUP_W), F32), pltpu.VMEM((1, GROUP_W), F32)],
        compiler_params=_cparams(("parallel", "arbitrary")),
        name="rglru",
    )(proj, proj, conv_w, vec(conv_b), w_r.astype(BF16), vec(b_r), w_i.astype(BF16), vec(b_i),
      vec(lam))


def _inv_freq_row(d):
    i = np.arange(128) % (d // 2)
    return jnp.asarray((ROPE_THETA ** (-(2.0 * i) / d)).astype(np.float32).reshape(1, 128))


def _rope128(x, cos, sin_signed):
    return x * cos + pltpu.roll(x, 64, 1) * sin_signed


def _rope64(x, cos, sin_masked, lane):
    rot = jnp.where(lane < 32, -pltpu.roll(x, 96, 1), pltpu.roll(x, 32, 1))
    return x * cos + rot * sin_masked


def _mla_prep_kernel(cq_ref, ckv_ref, misc_ref, pos_ref, qng_ref, wuq_ref, kvng_ref, wukv_ref,
                     qgn_ref, qgp_ref, kgn_ref, kgp_ref, invf_ref,
                     q_ref, k_ref, v_ref, *, T):
    scale = (MLA_NOPE + MLA_ROPE) ** -0.5 * LOG2E
    lane = lax.broadcasted_iota(jnp.int32, (T, 128), 1)
    ang = pos_ref[0] * invf_ref[...]
    cos = jnp.cos(ang)
    sin = jnp.where(lane < MLA_ROPE, jnp.sin(ang), 0.0)

    cq = _rms(cq_ref[0], MLA_Q_RANK) * qng_ref[...]
    qf = _dot(cq.astype(BF16), wuq_ref[...])
    ckv = _rms(ckv_ref[0], MLA_KV_RANK) * kvng_ref[...]
    kvf = _dot(ckv.astype(BF16), wukv_ref[...])
    kpe = jnp.where(lane < MLA_ROPE, misc_ref[0], 0.0)
    kpe = _rope64(_rms(kpe, MLA_ROPE) * kgp_ref[...], cos, sin, lane)
    for h in range(N_HEADS):
        o = h * 256
        qn = _rms(qf[:, o:o + 128], MLA_NOPE) * qgn_ref[...]
        qp = _rope64(_rms(qf[:, o + 128:o + 256], MLA_ROPE) * qgp_ref[...], cos, sin, lane)
        q_ref[0, :, o:o + 128] = (qn * scale).astype(BF16)
        q_ref[0, :, o + 128:o + 256] = (qp * scale).astype(BF16)
        kn = _rms(kvf[:, o:o + 128], MLA_NOPE) * kgn_ref[...]
        k_ref[0, :, o:o + 128] = kn.astype(BF16)
        k_ref[0, :, o + 128:o + 256] = kpe.astype(BF16)
        v_ref[0, :, h * 128:(h + 1) * 128] = kvf[:, o + 128:o + 256].astype(BF16)


def _mla_prep(proj, posb, q_norm_g, w_uq, kv_norm_g, w_ukv, qk_norm_g):
    b, s, _ = proj.shape
    t = min(512, s)
    wuq = jnp.pad(w_uq, ((0, 0), (0, 0), (0, 64))).reshape(MLA_Q_RANK, N_HEADS * 256).astype(BF16)
    wukv = w_ukv.reshape(MLA_KV_RANK, N_HEADS * 256).astype(BF16)
    pad64 = lambda v: jnp.pad(v, (0, 64)).reshape(1, 128)
    full = lambda shape: pl.BlockSpec(shape, lambda bb, i: (0,) * len(shape))
    tok = lambda w: pl.BlockSpec((1, t, w), lambda bb, i: (bb, i, 0))
    return pl.pallas_call(
        functools.partial(_mla_prep_kernel, T=t),
        grid=(b, s // t),
        in_specs=[_seg_spec(t, MLA_Q_RANK, OFF_BCQ), _seg_spec(t, MLA_KV_RANK, OFF_BCKV),
                  _seg_spec(t, 128, OFF_MISC), tok(128),
                  full((1, MLA_Q_RANK)), full((MLA_Q_RANK, 1024)),
                  full((1, MLA_KV_RANK)), full((MLA_KV_RANK, 1024)),
                  full((1, 128)), full((1, 128)), full((1, 128)), full((1, 128)), full((1, 128))],
        out_specs=[tok(1024), tok(1024), tok(512)],
        out_shape=[jax.ShapeDtypeStruct((b, s, 1024), BF16),
                   jax.ShapeDtypeStruct((b, s, 1024), BF16),
                   jax.ShapeDtypeStruct((b, s, 512), BF16)],
        compiler_params=_cparams(("parallel", "parallel")),
        name="mla_prep",
    )(proj, proj, proj, posb, q_norm_g.reshape(1, -1), wuq, kv_norm_g.reshape(1, -1), wukv,
      qk_norm_g[0, :MLA_NOPE].reshape(1, 128), pad64(qk_norm_g[0, MLA_NOPE:]),
      qk_norm_g[1, :MLA_NOPE].reshape(1, 128), pad64(qk_norm_g[1, MLA_NOPE:]),
      _inv_freq_row(MLA_ROPE))


def _softmax_init(m_ref, l_ref, acc_ref):
    m_ref[...] = jnp.full_like(m_ref, NEG)
    l_ref[...] = jnp.zeros_like(l_ref)
    acc_ref[...] = jnp.zeros_like(acc_ref)


ROW_GROUPS = 2


def _attend_chunk(q, k_blk, v_blk, m_ref, l_ref, acc_ref, mask_fn=None):
    g_rows = q.shape[0] // ROW_GROUPS
    scores = [_dot_nt(q[g * g_rows:(g + 1) * g_rows], k_blk) for g in range(ROW_GROUPS)]
    for g in range(ROW_GROUPS):
        rows = slice(g * g_rows, (g + 1) * g_rows)
        s = scores[g]
        if mask_fn is not None:
            s = mask_fn(s, g * g_rows)
        m_old = m_ref[rows, :]
        m_new = jnp.maximum(m_old, jnp.max(s, axis=-1, keepdims=True))
        alpha = jnp.exp2(m_old - m_new)
        p = jnp.exp2(s - m_new)
        part = p[:, 0:128]
        for j in range(1, s.shape[1] // 128):
            part = part + p[:, j * 128:(j + 1) * 128]
        l_ref[rows, :] = alpha * l_ref[rows, :] + part
        acc_ref[rows, :] = alpha * acc_ref[rows, :] + _dot(p.astype(BF16), v_blk)
        m_ref[rows, :] = m_new


def _softmax_result(l_ref, acc_ref):
    return acc_ref[...] / jnp.sum(l_ref[...], axis=-1, keepdims=True)


def _mla_flash_kernel(q_ref, k_ref, v_ref, gate_ref, o_ref, m_ref, l_ref, acc_ref, *, tq, kc):
    i = pl.program_id(2)
    _softmax_init(m_ref, l_ref, acc_ref)
    q = q_ref[0]

    def chunk(c, masked):
        k0 = pl.multiple_of(c * kc, kc)

        def causal(s, first_row):
            row = lax.broadcasted_iota(jnp.int32, s.shape, 0) + (i * tq + first_row)
            col = lax.broadcasted_iota(jnp.int32, s.shape, 1) + k0
            return jnp.where(col <= row, s, NEG)

        _attend_chunk(q, k_ref[0, pl.ds(k0, kc), :], v_ref[0, pl.ds(k0, kc), :],
                      m_ref, l_ref, acc_ref, causal if masked else None)

    def full_chunk(c, carry):
        chunk(c, False)
        return carry

    n_full = (i * tq) // kc
    lax.fori_loop(0, n_full, full_chunk, 0)
    chunk(n_full, True)
    o_ref[0] = _softmax_result(l_ref, acc_ref) * _silu(gate_ref[0])


def _mla_flash(q, k, v, proj):
    b, s, _ = q.shape
    tq = min(512, s)
    kc = min(1024, s)
    gate_blk = OFF_BG // 128
    return pl.pallas_call(
        functools.partial(_mla_flash_kernel, tq=tq, kc=kc),
        grid=(b, N_HEADS, s // tq),
        in_specs=[pl.BlockSpec((1, tq, 256), lambda bb, h, i: (bb, i, h)),
                  pl.BlockSpec((1, s, 256), lambda bb, h, i: (bb, 0, h)),
                  pl.BlockSpec((1, s, 128), lambda bb, h, i: (bb, 0, h)),
                  pl.BlockSpec((1, tq, 128), lambda bb, h, i: (bb, i, gate_blk + h))],
        out_specs=pl.BlockSpec((1, tq, 128), lambda bb, h, i: (bb, i, h)),
        out_shape=jax.ShapeDtypeStruct((b, s, GROUP_W), F32),
        scratch_shapes=[pltpu.VMEM((tq, 1), F32), pltpu.VMEM((tq, 128), F32),
                        pltpu.VMEM((tq, 128), F32)],
        compiler_params=_cparams(("parallel", "parallel", "arbitrary")),
        name="mla_flash",
    )(q, k, v, proj)


def _nsa_prep_kernel(dq_ref, dkv_ref, pos_ref, qg_ref, kg_ref, invf_ref,
                     q_ref, ks_ref, vs_ref, kw_ref, vw_ref, kc_ref, vc_ref, *, T):
    scale = HEAD_DIM ** -0.5 * LOG2E
    lane = lax.broadcasted_iota(jnp.int32, (T, 128), 1)
    ang = pos_ref[0] * invf_ref[...]
    cos = jnp.cos(ang)
    sin = jnp.sin(ang)
    sin = jnp.where(lane < 64, -sin, sin)
    dq = dq_ref[0]
    for h in range(N_HEADS):
        qh = _rms(dq[:, h * 128:(h + 1) * 128], HEAD_DIM) * qg_ref[...]
        q_ref[0, :, h * 128:(h + 1) * 128] = _rope128(qh, cos, sin) * scale
    kv = dkv_ref[0]
    kc_ref[0] = kv[:, 0:128]
    vc_ref[0] = kv[:, 128:256]
    ks_ref[0, :, 0:128] = _rope128(_rms(kv[:, 256:384], HEAD_DIM) * kg_ref[1:2], cos, sin).astype(BF16)
    key_blk = (pl.program_id(1) * T + lax.broadcasted_iota(jnp.int32, (T, 128), 0)) >> 6
    ks_ref[0, :, 128:256] = (lane == key_blk).astype(BF16)
    vs_ref[0] = kv[:, 384:512].astype(BF16)
    kw_ref[0] = _rope128(_rms(kv[:, 512:640], HEAD_DIM) * kg_ref[2:3], cos, sin).astype(BF16)
    vw_ref[0] = kv[:, 640:768].astype(BF16)


def _nsa_prep(proj, posb, q_norm_g, k_norm_g):
    b, s, _ = proj.shape
    t = min(512, s)
    full = lambda shape: pl.BlockSpec(shape, lambda bb, i: (0,) * len(shape))
    tok = lambda w: pl.BlockSpec((1, t, w), lambda bb, i: (bb, i, 0))
    sds = lambda w, dt: jax.ShapeDtypeStruct((b, s, w), dt)
    return pl.pallas_call(
        functools.partial(_nsa_prep_kernel, T=t),
        grid=(b, s // t),
        in_specs=[_seg_spec(t, GROUP_W, OFF_DQ), _seg_spec(t, 768, OFF_DKV), tok(128),
                  full((1, 128)), full((3, 128)), full((1, 128))],
        out_specs=[tok(512), tok(256)] + [tok(128)] * 5,
        out_shape=[sds(512, F32), sds(256, BF16), sds(128, BF16), sds(128, BF16), sds(128, BF16),
                   sds(128, F32), sds(128, F32)],
        compiler_params=_cparams(("parallel", "parallel")),
        name="nsa_prep",
    )(proj, proj, posb, q_norm_g.reshape(1, HEAD_DIM), k_norm_g, _inv_freq_row(HEAD_DIM))


def _nsa_cmp_kernel(kt_ref, vt_ref, pe_ref, w1_ref, b1_ref, w2_ref, b2_ref, kg_ref, pos_ref,
                    invf_ref, kc_ref, vc_ref, *, NC):
    half = CMP_STRIDE * HEAD_DIM
    outs = []
    for j, t_ref in enumerate((kt_ref, vt_ref)):
        t2 = t_ref[0].astype(BF16)
        first = _dot(t2, w1_ref[j, :half, :])
        second = pltpu.roll(_dot(t2, w1_ref[j, half:, :]), NC - 1, 0)
        pe8 = jnp.broadcast_to(pe_ref[j], (8, CMP_LEN * HEAD_DIM)).astype(BF16)
        bias = _dot(pe8, w1_ref[j])[0:1] + b1_ref[j]
        hid = _silu(first + second + bias)
        outs.append(_dot(hid.astype(BF16), w2_ref[j]) + b2_ref[j])
    k_c, v_c = outs
    lane = lax.broadcasted_iota(jnp.int32, (NC, 128), 1)
    ang = pos_ref[0] * invf_ref[...]
    sin = jnp.sin(ang)
    sin = jnp.where(lane < 64, -sin, sin)
    kc_ref[0] = _rope128(_rms(k_c, HEAD_DIM) * kg_ref[...], jnp.cos(ang), sin)
    vc_ref[0] = v_c


def _nsa_compress(kc_raw, vc_raw, posc, cmp_pe, cmp_w1, cmp_b1, cmp_w2, cmp_b2, kg0):
    b, s, _ = kc_raw.shape
    nc = s // CMP_STRIDE
    kt = kc_raw.reshape(b, nc, CMP_STRIDE * HEAD_DIM)
    vt = vc_raw.reshape(b, nc, CMP_STRIDE * HEAD_DIM)
    full = lambda shape: pl.BlockSpec(shape, lambda bb: (0,) * len(shape))
    per_b = lambda shape: pl.BlockSpec((1,) + shape, lambda bb: (bb, 0, 0))
    return pl.pallas_call(
        functools.partial(_nsa_cmp_kernel, NC=nc),
        grid=(b,),
        in_specs=[per_b((nc, CMP_STRIDE * HEAD_DIM)), per_b((nc, CMP_STRIDE * HEAD_DIM)),
                  full((2, 1, CMP_LEN * HEAD_DIM)), full((2, CMP_LEN * HEAD_DIM, CMP_HIDDEN)),
                  full((2, 1, CMP_HIDDEN)), full((2, CMP_HIDDEN, HEAD_DIM)),
                  full((2, 1, HEAD_DIM)), full((1, HEAD_DIM)), per_b((nc, 128)), full((1, 128))],
        out_specs=[per_b((nc, HEAD_DIM)), per_b((nc, HEAD_DIM))],
        out_shape=[jax.ShapeDtypeStruct((b, nc, HEAD_DIM), F32),
                   jax.ShapeDtypeStruct((b, nc, HEAD_DIM), F32)],
        compiler_params=_cparams(("parallel",)),
        name="nsa_compress",
    )(kt, vt, cmp_pe.reshape(2, 1, CMP_LEN * HEAD_DIM), cmp_w1.astype(BF16),
      cmp_b1.reshape(2, 1, CMP_HIDDEN), cmp_w2.astype(BF16), cmp_b2.reshape(2, 1, HEAD_DIM),
      kg0.reshape(1, HEAD_DIM), posc, _inv_freq_row(HEAD_DIM))


def _nsa_attn_kernel(q_ref, kc_ref, vc_ref, ks_ref, vs_ref, kw_ref, vw_ref, misc_ref, gate_ref,
                     o_ref, m_ref, l_ref, acc_ref, *, S, NC, KC, WK):
    Q = Q_BLOCK
    R = N_HEADS
    i = pl.program_id(1)
    n_sel = S // SEL_LEN
    n_cmp = (S - CMP_LEN) // CMP_STRIDE + 1
    top_k = min(SEL_TOPK, n_sel)

    qf = q_ref[0]
    q4 = jnp.concatenate([qf[:, r * 128:(r + 1) * 128] for r in range(R)], axis=0)
    q4b = q4.astype(BF16)

    q_hi, q_lo = _split_bf16(q4)
    k_hi, k_lo = _split_bf16(kc_ref[0])
    s_c = (_dot_nt(jnp.concatenate([q_hi, q_lo], axis=1), jnp.concatenate([k_hi, k_hi], axis=1))
           + _dot_nt(q_hi, k_lo))
    t_c = i * Q + (lax.broadcasted_iota(jnp.int32, (R * Q, NC), 0) & (Q - 1))
    c_ix = lax.broadcasted_iota(jnp.int32, (R * Q, NC), 1)
    valid_c = (c_ix * CMP_STRIDE + (CMP_LEN - 1) <= t_c) & (c_ix < n_cmp)
    s_c = jnp.where(valid_c, s_c, NEG)
    p_c = jnp.where(valid_c, jnp.exp2(s_c - jnp.max(s_c, axis=-1, keepdims=True)), 0.0)
    p_c = p_c / jnp.maximum(jnp.sum(p_c, axis=-1, keepdims=True), 1e-30)
    o_c = _bdot(p_c, vc_ref[0])
    assert NC & (NC - 1) == 0
    c_o = (lax.broadcasted_iota(jnp.int32, (2 * NC, 128), 0) & (NC - 1)) * CMP_STRIDE
    n_o = lax.broadcasted_iota(jnp.int32, (2 * NC, 128), 1) * SEL_LEN
    overlap2 = ((c_o < n_o + SEL_LEN) & (c_o + (CMP_LEN - 1) >= n_o)).astype(BF16)
    imp4 = _dot(jnp.concatenate(_split_bf16(p_c), axis=1), overlap2)
    imp = (imp4[0:Q] + imp4[Q:2 * Q] + imp4[2 * Q:3 * Q] + imp4[3 * Q:4 * Q]).T

    NP = min(128, -(-n_sel // 8) * 8)
    t_q = i * Q + lax.broadcasted_iota(jnp.int32, (NP, Q), 1)
    n_ix = lax.broadcasted_iota(jnp.int32, (NP, Q), 0)
    cur = t_q >> 6
    valid_s = (n_ix * SEL_LEN <= t_q) & (n_ix < n_sel)
    forced = (n_ix == 0) | (n_ix == cur) | (n_ix == cur - 1)
    val = jnp.where(valid_s, imp[:NP], -1.0)
    val = jnp.where(forced & valid_s, FORCE, val)
    val = jnp.where(n_ix < n_sel, val, -2.0)
    sel_t = jnp.zeros((NP, Q), F32)
    n_f = n_ix.astype(F32)
    for _ in range(top_k):
        mx = jnp.max(val, axis=0, keepdims=True)
        first = jnp.min(jnp.where(val == mx, n_f, 1e9), axis=0, keepdims=True)
        hit = n_f == first
        sel_t = jnp.where(hit, 1.0, sel_t)
        val = jnp.where(hit, -3.0, val)
    sel_t = jnp.where(valid_s, sel_t, 0.0)
    if NP < 128:
        sel_t = jnp.concatenate([sel_t, jnp.zeros((128 - NP, Q), F32)], axis=0)
    blk_bias = jnp.where(sel_t.T > 0.5, 0.0, NEG).astype(BF16)
    q_aug = jnp.concatenate([q4b, jnp.concatenate([blk_bias] * R, axis=0)], axis=1)

    _softmax_init(m_ref, l_ref, acc_ref)

    def sel_chunk(c, causal):
        k0 = pl.multiple_of(c * KC, KC)

        def causal_mask(s, first_row):
            t_k = i * Q + (lax.broadcasted_iota(jnp.int32, s.shape, 0) & (Q - 1))
            return jnp.where(k0 + lax.broadcasted_iota(jnp.int32, s.shape, 1) <= t_k, s, NEG)

        _attend_chunk(q_aug, ks_ref[0, pl.ds(k0, KC), :], vs_ref[0, pl.ds(k0, KC), :],
                      m_ref, l_ref, acc_ref, causal_mask if causal else None)

    def past_chunk(c, carry):
        sel_chunk(c, False)
        return carry

    n_past = (i * Q) // KC
    lax.fori_loop(0, n_past, past_chunk, 0)
    sel_chunk(n_past, True)
    o_s = _softmax_result(l_ref, acc_ref)

    w0 = pl.multiple_of(jnp.maximum(i * Q - WINDOW, 0), Q)
    s_w = _dot_nt(q4b, kw_ref[0, pl.ds(w0, WK), :])
    t_w = i * Q + (lax.broadcasted_iota(jnp.int32, (R * Q, WK), 0) & (Q - 1))
    kpos = w0 + lax.broadcasted_iota(jnp.int32, (R * Q, WK), 1)
    s_w = jnp.where((kpos <= t_w) & (kpos > t_w - WINDOW), s_w, NEG)
    p_w = jnp.exp2(s_w - jnp.max(s_w, axis=-1, keepdims=True))
    p_w = p_w / jnp.sum(p_w, axis=-1, keepdims=True)
    o_w = _dot(p_w.astype(BF16), vw_ref[0, pl.ds(w0, WK), :])

    gates = jax.nn.sigmoid(misc_ref[0])
    gate = gate_ref[0]
    for r in range(R):
        g0 = gates[:, MISC_GL + 3 * r:MISC_GL + 3 * r + 1]
        g1 = gates[:, MISC_GL + 3 * r + 1:MISC_GL + 3 * r + 2]
        g2 = gates[:, MISC_GL + 3 * r + 2:MISC_GL + 3 * r + 3]
        rows = slice(r * Q, (r + 1) * Q)
        o = g0 * o_c[rows] + g1 * o_s[rows] + g2 * o_w[rows]
        o_ref[0, :, r * 128:(r + 1) * 128] = o * _silu(gate[:, r * 128:(r + 1) * 128])


def _nsa_attn(qf, k_c, v_c, ks, vs, kw, vw, proj):
    b, s, _ = qf.shape
    nc = s // CMP_STRIDE
    kc_len = min(1024, s)
    wk = min(WINDOW + Q_BLOCK, s)
    per_b = lambda n, w: pl.BlockSpec((1, n, w), lambda bb, i: (bb, 0, 0))
    return pl.pallas_call(
        functools.partial(_nsa_attn_kernel, S=s, NC=nc, KC=kc_len, WK=wk),
        grid=(b, s // Q_BLOCK),
        in_specs=[pl.BlockSpec((1, Q_BLOCK, GROUP_W), lambda bb, i: (bb, i, 0)),
                  per_b(nc, 128), per_b(nc, 128),
                  per_b(s, 256), per_b(s, 128), per_b(s, 128), per_b(s, 128),
                  _seg_spec(Q_BLOCK, 128, OFF_MISC), _seg_spec(Q_BLOCK, GROUP_W, OFF_DG)],
        out_specs=pl.BlockSpec((1, Q_BLOCK, GROUP_W), lambda bb, i: (bb, i, 0)),
        out_shape=jax.ShapeDtypeStruct((b, s, GROUP_W), F32),
        scratch_shapes=[pltpu.VMEM((N_HEADS * Q_BLOCK, 1), F32),
                        pltpu.VMEM((N_HEADS * Q_BLOCK, 128), F32),
                        pltpu.VMEM((N_HEADS * Q_BLOCK, 128), F32)],
        compiler_params=_cparams(("parallel", "arbitrary")),
        name="nsa_attn",
    )(qf, k_c, v_c, ks, vs, kw, vw, proj, proj)


def _reorder_w_in_kernel(w_ref, o_ref):
    o = np.concatenate([[0], np.cumsum(IN_SIZES)]).tolist()
    w = w_ref[0]
    seg = lambda k: w[:, o[k]:o[k + 1]]
    (a_qkv, a_decay, a_beta, a_gate, b_cq, b_ckv, b_kpe, b_gate, c_x, c_gate,
     d_q, d_kv, d_gl, d_gate) = [seg(k) for k in range(len(IN_SIZES))]
    zeros = jnp.zeros((w.shape[0], 128 - MLA_ROPE - 2 * N_HEADS - 3 * N_HEADS), w.dtype)
    misc = jnp.concatenate([b_kpe, a_decay, a_beta, d_gl, zeros], axis=-1)
    off = 0
    for part in (a_qkv, a_gate, b_gate, c_x, c_gate, d_q, d_gate, d_kv, b_cq, misc, b_ckv):
        o_ref[0, :, off:off + part.shape[1]] = part.astype(BF16)
        off += part.shape[1]
    assert off == D_PROJ


def _reorder_w_in(w_in):
    depth, d, d_in = w_in.shape
    rows = 256
    return pl.pallas_call(
        _reorder_w_in_kernel,
        grid=(depth, d // rows),
        in_specs=[pl.BlockSpec((1, rows, d_in), lambda l, i: (l, i, 0))],
        out_specs=pl.BlockSpec((1, rows, D_PROJ), lambda l, i: (l, i, 0)),
        out_shape=jax.ShapeDtypeStruct((depth, d, D_PROJ), BF16),
        compiler_params=_cparams(("parallel", "parallel")),
        name="reorder_w_in",
    )(w_in)


def kernel(x, positions, norm_g, w_in, w_out, gdn_conv_w, gdn_a_log, gdn_dt_bias, gdn_norm_g,
           mla_q_norm_g, mla_w_uq, mla_kv_norm_g, mla_w_ukv, mla_qk_norm_g,
           lru_conv_w, lru_conv_b, lru_w_r, lru_b_r, lru_w_i, lru_b_i, lru_lambda,
           nsa_q_norm_g, nsa_k_norm_g, nsa_cmp_pe, nsa_cmp_w1, nsa_cmp_b1, nsa_cmp_w2, nsa_cmp_b2):
    b, s, d = x.shape
    depth = w_in.shape[0]
    w_in_r = _reorder_w_in(w_in)
    w_out_b = w_out.astype(BF16)
    posf = positions.astype(F32)
    posb = jnp.broadcast_to(posf[:, :, None], (b, s, 128))
    nc = s // CMP_STRIDE
    pos_end = jnp.pad(posf.reshape(b, nc, CMP_STRIDE)[:, 1:, CMP_STRIDE - 1], ((0, 0), (0, 1)))
    posc = jnp.broadcast_to(pos_end[:, :, None], (b, nc, 128))

    x2d = x.reshape(b * s, d)
    for l in range(depth):
        proj = _inproj(x2d, norm_g[l].reshape(1, d), w_in_r[l]).reshape(b, s, D_PROJ)
        y_a = _gdn(proj, gdn_conv_w[l], gdn_a_log[l], gdn_dt_bias[l], gdn_norm_g[l])
        q_b, k_b, v_b = _mla_prep(proj, posb, mla_q_norm_g[l], mla_w_uq[l], mla_kv_norm_g[l],
                                  mla_w_ukv[l], mla_qk_norm_g[l])
        y_b = _mla_flash(q_b, k_b, v_b, proj)
        y_c = _lru(proj, lru_conv_w[l], lru_conv_b[l], lru_w_r[l], lru_b_r[l], lru_w_i[l],
                   lru_b_i[l], lru_lambda[l])
        q_d, ks, vs, kw, vw, kc_raw, vc_raw = _nsa_prep(proj, posb, nsa_q_norm_g[l],
                                                        nsa_k_norm_g[l])
        k_c, v_c = _nsa_compress(kc_raw, vc_raw, posc, nsa_cmp_pe[l], nsa_cmp_w1[l],
                                 nsa_cmp_b1[l], nsa_cmp_w2[l], nsa_cmp_b2[l], nsa_k_norm_g[l, 0])
        y_d = _nsa_attn(q_d, k_c, v_c, ks, vs, kw, vw, proj)
        ys = [y.reshape(b * s, GROUP_W) for y in (y_a, y_b, y_c, y_d)]
        x2d = _outproj(ys, w_out_b[l], x2d)
    return x2d.reshape(b, s, d)
```

```python
import functools
import math

import numpy as np
import jax
import jax.numpy as jnp
from jax import lax
from jax.experimental import pallas as pl
from jax.experimental.pallas import tpu as pltpu

F32 = jnp.float32
BF16 = jnp.bfloat16
HIGHEST = lax.Precision.HIGHEST

D_MODEL = 2048
GROUP_W = 512
HEAD_DIM = 128
N_HEADS = 4
RMS_EPS = 1e-6
ROPE_THETA = 10000.0
NEG = -1e30
LOG2E = math.log2(math.e)
GDN_CONV = 4
GDN_CHUNK = 128
MLA_Q_RANK = 384
MLA_KV_RANK = 256
MLA_NOPE = 128
MLA_ROPE = 64
LRU_CONV = 4
LRU_C = 8.0
CMP_LEN = 32
CMP_STRIDE = 16
CMP_HIDDEN = 256
SEL_LEN = 64
SEL_TOPK = 16
WINDOW = 512
FORCE = 1e9
Q_BLOCK = 128

IN_SIZES = (3 * GROUP_W, N_HEADS, N_HEADS, GROUP_W,
            MLA_Q_RANK, MLA_KV_RANK, MLA_ROPE, GROUP_W,
            GROUP_W, GROUP_W,
            GROUP_W, 6 * HEAD_DIM, 3 * N_HEADS, GROUP_W)

OFF_AQ, OFF_AK, OFF_AV, OFF_AG = 0, 512, 1024, 1536
OFF_BG, OFF_CX, OFF_CG, OFF_DQ, OFF_DG = 2048, 2560, 3072, 3584, 4096
OFF_DKV, OFF_BCQ, OFF_MISC, OFF_BCKV = 4608, 5376, 5760, 5888
D_PROJ = 6144
MISC_KPE, MISC_DECAY, MISC_BETA, MISC_GL = 0, 64, 68, 72

VMEM_LIMIT = 56 * 1024 * 1024


def _cparams(sem):
    return pltpu.CompilerParams(dimension_semantics=sem, vmem_limit_bytes=VMEM_LIMIT)


def _dot(a, b, precision=None):
    return lax.dot_general(a, b, (((1,), (0,)), ((), ())), precision=precision,
                           preferred_element_type=F32)


def _dot_nt(a, b, precision=None):
    return lax.dot_general(a, b, (((1,), (1,)), ((), ())), precision=precision,
                           preferred_element_type=F32)


def _bdot(a, b):
    return _dot(a.astype(BF16), b.astype(BF16))


def _bdot_nt(a, b):
    return _dot_nt(a.astype(BF16), b.astype(BF16))


def _split_bf16(x):
    hi = x.astype(BF16)
    return hi, (x - hi.astype(F32)).astype(BF16)


def _silu(x):
    return x * jax.nn.sigmoid(x)


def _softplus(x):
    return jnp.maximum(x, 0.0) + jnp.log1p(jnp.exp(-jnp.abs(x)))


def _rms(x, n):
    return x * lax.rsqrt(jnp.sum(x * x, axis=-1, keepdims=True) * (1.0 / n) + RMS_EPS)


def _inproj_kernel(x_ref, g_ref, w_ref, o_ref, h_ref, *, tm):
    @pl.when(pl.program_id(1) == 0)
    def _():
        rows = min(256, tm)
        for r in range(tm // rows):
            x = x_ref[r * rows:(r + 1) * rows, :]
            h_ref[r * rows:(r + 1) * rows, :] = (_rms(x, D_MODEL) * g_ref[...]).astype(BF16)

    o_ref[...] = _dot_nt(h_ref[...], w_ref[...])


def _inproj(x2d, g, w_t, layer):
    m = x2d.shape[0]
    tm = min(1024, m)
    tn = 512
    return pl.pallas_call(
        functools.partial(_inproj_kernel, tm=tm),
        grid=(m // tm, D_PROJ // tn),
        in_specs=[pl.BlockSpec((tm, D_MODEL), lambda i, j: (i, 0)),
                  pl.BlockSpec((1, D_MODEL), lambda i, j: (0, 0)),
                  pl.BlockSpec((None, tn, D_MODEL), lambda i, j: (layer, j, 0))],
        out_specs=pl.BlockSpec((tm, tn), lambda i, j: (i, j)),
        out_shape=jax.ShapeDtypeStruct((m, D_PROJ), F32),
        scratch_shapes=[pltpu.VMEM((tm, D_MODEL), BF16)],
        compiler_params=_cparams(("parallel", "arbitrary")),
        name="inproj",
    )(x2d, g, w_t)


def _outproj_kernel(ya_ref, yb_ref, yc_ref, yd_ref, wa_ref, wb_ref, wc_ref, wd_ref, x_ref, o_ref):
    acc = x_ref[...]
    for y_ref, w_ref in ((ya_ref, wa_ref), (yb_ref, wb_ref), (yc_ref, wc_ref), (yd_ref, wd_ref)):
        acc = acc + _dot(y_ref[...].astype(BF16), w_ref[...])
    o_ref[...] = acc


def _outproj(ys, w, x2d):
    m = x2d.shape[0]
    tm = min(1024, m)
    tn = 512
    y_specs = [pl.BlockSpec((tm, GROUP_W), lambda i, j: (i, 0)) for _ in range(4)]
    w_specs = [pl.BlockSpec((GROUP_W, tn), functools.partial(lambda i, j, g: (g, j), g=g))
               for g in range(4)]
    return pl.pallas_call(
        _outproj_kernel,
        grid=(m // tm, D_MODEL // tn),
        in_specs=y_specs + w_specs + [pl.BlockSpec((tm, tn), lambda i, j: (i, j))],
        out_specs=pl.BlockSpec((tm, tn), lambda i, j: (i, j)),
        out_shape=jax.ShapeDtypeStruct((m, D_MODEL), F32),
        compiler_params=_cparams(("parallel", "arbitrary")),
        name="outproj",
    )(*ys, w, w, w, w, x2d)


def _causal_conv4(xin, ext_ref, w):
    t = xin.shape[0]
    ext_ref[8:8 + t, :] = xin
    acc = ext_ref[5:5 + t, :] * w[0:1]
    for j in range(1, 4):
        acc = acc + ext_ref[5 + j:5 + j + t, :] * w[j:j + 1]
    ext_ref[0:8, :] = xin[t - 8:t]
    return acc


def _gdn_kernel(q_ref, k_ref, v_ref, gate_ref, misc_ref, cw_ref, alog_ref, dtb_ref, ng_ref,
                o_ref, hist_ref, state_ref, *, T):
    C = GDN_CHUNK

    @pl.when(pl.program_id(1) == 0)
    def _():
        hist_ref[:, 0:8, :] = jnp.zeros((3, 8, GROUP_W), F32)
        state_ref[...] = jnp.zeros_like(state_ref)

    cw = cw_ref[...]
    conv = []
    for idx, ref in enumerate((q_ref, k_ref, v_ref)):
        xin = ref[0]
        acc = _causal_conv4(xin, hist_ref.at[idx], cw[:, idx * GROUP_W:(idx + 1) * GROUP_W])
        conv.append(_silu(acc))
    qc, kc, vc = conv

    misc = misc_ref[0]
    gfull = -jnp.exp(alog_ref[...]) * _softplus(misc + dtb_ref[...])
    betafull = jax.nn.sigmoid(misc)
    gate = gate_ref[0]

    row = lax.broadcasted_iota(jnp.int32, (C, C), 0)
    col = lax.broadcasted_iota(jnp.int32, (C, C), 1)
    lower = row >= col
    strict = row > col
    ltri = lower.astype(F32)
    row2 = lax.broadcasted_iota(jnp.int32, (C, 2 * C), 0)
    lane2 = lax.broadcasted_iota(jnp.int32, (C, 2 * C), 1)
    left = lane2 < C
    col2 = lane2 & (C - 1)
    leaf_bits = 4
    same_blk = {bits: (row2 >> bits) == (col2 >> bits) for bits in range(leaf_bits, 8)}
    assert C == HEAD_DIM == 1 << 7

    def pairdot(x, y):
        yb = y.astype(BF16)
        zero = jnp.zeros_like(yb)
        bd = jnp.concatenate([jnp.where(left, yb, zero), jnp.where(left, zero, yb)], axis=0)
        return _dot(x.astype(BF16), bd)

    n_chunks = T // C
    hds = []
    for c in range(n_chunks):
        r0 = c * C
        gall = _dot(ltri, gfull[r0:r0 + C], HIGHEST)
        hd = []
        hds.append(hd)
        for h in range(N_HEADS):
            l0 = h * HEAD_DIM
            q = qc[r0:r0 + C, l0:l0 + HEAD_DIM]
            k = kc[r0:r0 + C, l0:l0 + HEAD_DIM]
            v = vc[r0:r0 + C, l0:l0 + HEAD_DIM]
            q = q * lax.rsqrt(jnp.sum(q * q, axis=-1, keepdims=True) + RMS_EPS) * (HEAD_DIM ** -0.5)
            k = k * lax.rsqrt(jnp.sum(k * k, axis=-1, keepdims=True) + RMS_EPS)
            beta = betafull[r0:r0 + C, MISC_BETA + h:MISC_BETA + h + 1]
            gc = jnp.broadcast_to(gall[:, MISC_DECAY + h:MISC_DECAY + h + 1], (C, C))
            decay = jnp.exp(jnp.where(lower, gc - gc.T, NEG))
            kb = k * beta
            a = jnp.where(strict, _bdot_nt(kb, k) * decay, 0.0)
            hd.append((q, k, v, beta, gc, decay, kb, a))

    a2s = [jnp.concatenate([hds[c][h0][7], hds[c][h0 + 1][7]], axis=1)
           for c in range(n_chunks) for h0 in range(0, N_HEADS, 2)]
    ps = [jnp.where(same_blk[leaf_bits], -a2, 0.0) for a2 in a2s]
    tm2 = list(ps)
    for _ in range(leaf_bits - 1):
        ps = [pairdot(p, p) for p in ps]
        tm2 = [tm + p + pairdot(tm, p) for tm, p in zip(tm2, ps)]
    for bits in range(leaf_bits + 1, 8):
        join = same_blk[bits] & jnp.logical_not(same_blk[bits - 1])
        tas = [jnp.where(join, a2, 0.0) for a2 in a2s]
        tas = [off + pairdot(tm, off) for tm, off in zip(tm2, tas)]
        tm2 = [tm - (ta + pairdot(ta, tm)) for tm, ta in zip(tm2, tas)]

    for c in range(n_chunks):
        r0 = c * C
        hd = hds[c]
        tms = []
        for tm in tm2[2 * c:2 * c + 2]:
            tms += [tm[:, :C], tm[:, C:]]
        for h in range(N_HEADS):
            l0 = h * HEAD_DIM
            q, k, v, beta, gc, decay, kb, _ = hd[h]
            eg = jnp.exp(gc)
            rhs = jnp.concatenate([v * beta, kb * eg], axis=-1)
            uw = rhs + _bdot(tms[h], rhs)
            u = uw[:, :HEAD_DIM]
            w = uw[:, HEAD_DIM:]
            qk = _bdot_nt(q, k) * decay
            q_dec = q * eg
            g_last = gc[C - 1:C, :]
            k_end = k * jnp.exp(g_last - gc)
            c_dec = jnp.exp(g_last)
            state = state_ref[h]
            v_new = u - _bdot(w, state)
            o = _bdot(q_dec, state) + _bdot(qk, v_new)
            state_ref[h] = state * c_dec + lax.dot_general(
                k_end.astype(BF16), v_new.astype(BF16), (((0,), (0,)), ((), ())),
                preferred_element_type=F32)
            o = _rms(o, HEAD_DIM) * ng_ref[...]
            o_ref[0, r0:r0 + C, l0:l0 + HEAD_DIM] = (
                o * _silu(gate[r0:r0 + C, l0:l0 + HEAD_DIM])).astype(o_ref.dtype)


def _seg_spec(t, width, off):
    blk = off // width
    assert blk * width == off
    return pl.BlockSpec((1, t, width), lambda b, i: (b, i, blk))


def _gdn(proj, conv_w, a_log, dt_bias, norm_g):
    b, s, _ = proj.shape
    t = min(512, s)
    pad = lambda vec, off: jnp.zeros((1, 128), F32).at[0, off:off + N_HEADS].set(vec)
    full = lambda shape: pl.BlockSpec(shape, lambda bb, i: (0,) * len(shape))
    return pl.pallas_call(
        functools.partial(_gdn_kernel, T=t),
        grid=(b, s // t),
        in_specs=[_seg_spec(t, GROUP_W, OFF_AQ), _seg_spec(t, GROUP_W, OFF_AK),
                  _seg_spec(t, GROUP_W, OFF_AV), _seg_spec(t, GROUP_W, OFF_AG),
                  _seg_spec(t, 128, OFF_MISC),
                  full((GDN_CONV, 3 * GROUP_W)), full((1, 128)), full((1, 128)), full((1, 128))],
        out_specs=pl.BlockSpec((1, t, GROUP_W), lambda bb, i: (bb, i, 0)),
        out_shape=jax.ShapeDtypeStruct((b, s, GROUP_W), BF16),
        scratch_shapes=[pltpu.VMEM((3, t + 8, GROUP_W), F32),
                        pltpu.VMEM((N_HEADS, HEAD_DIM, HEAD_DIM), F32)],
        compiler_params=_cparams(("parallel", "arbitrary")),
        name="gdn",
    )(proj, proj, proj, proj, proj, conv_w, pad(a_log, MISC_DECAY), pad(dt_bias, MISC_DECAY),
      norm_g.reshape(1, HEAD_DIM))


def _lru_kernel(x_ref, gate_ref, cw_ref, cb_ref, wr_ref, br_ref, wi_ref, bi_ref, lam_ref,
                o_ref, hist_ref, h_ref, *, T):
    @pl.when(pl.program_id(1) == 0)
    def _():
        hist_ref[0:8, :] = jnp.zeros((8, GROUP_W), F32)
        h_ref[...] = jnp.zeros_like(h_ref)

    xin = x_ref[0]
    xc = _causal_conv4(xin, hist_ref, cw_ref[...]) + cb_ref[...]
    rs, igs = [], []
    for h in range(N_HEADS):
        xh = xc[:, h * HEAD_DIM:(h + 1) * HEAD_DIM].astype(BF16)
        rs.append(_dot(xh, wr_ref[h]))
        igs.append(_dot(xh, wi_ref[h]))
    r = jax.nn.sigmoid(jnp.concatenate(rs, axis=-1) + br_ref[...])
    ig = jax.nn.sigmoid(jnp.concatenate(igs, axis=-1) + bi_ref[...])
    log_a = -LRU_C * r * _softplus(-lam_ref[...])
    a = jnp.exp(log_a)
    bb = jnp.sqrt(-jnp.tanh(log_a) * (a * a + 1.0)) * (ig * xc)
    rowi = lax.broadcasted_iota(jnp.int32, (T, GROUP_W), 0)
    d = 1
    while d < T:
        keep = rowi >= d
        a_sh = jnp.where(keep, pltpu.roll(a, d, 0), 1.0)
        b_sh = jnp.where(keep, pltpu.roll(bb, d, 0), 0.0)
        bb = a * b_sh + bb
        a = a * a_sh
        d *= 2
    hseq = bb + a * h_ref[...]
    h_ref[...] = hseq[T - 1:T]
    o_ref[0] = (hseq * _silu(gate_ref[0])).astype(o_ref.dtype)


def _lru(proj, conv_w, conv_b, w_r, b_r, w_i, b_i, lam):
    b, s, _ = proj.shape
    t = min(256, s)
    full = lambda shape: pl.BlockSpec(shape, lambda bb, i: (0,) * len(shape))
    vec = lambda v: v.reshape(1, GROUP_W)
    return pl.pallas_call(
        functools.partial(_lru_kernel, T=t),
        grid=(b, s // t),
        in_specs=[_seg_spec(t, GROUP_W, OFF_CX), _seg_spec(t, GROUP_W, OFF_CG),
                  full((LRU_CONV, GROUP_W)), full((1, GROUP_W)),
                  full((N_HEADS, HEAD_DIM, HEAD_DIM)), full((1, GROUP_W)),
                  full((N_HEADS, HEAD_DIM, HEAD_DIM)), full((1, GROUP_W)), full((1, GROUP_W))],
        out_specs=pl.BlockSpec((1, t, GROUP_W), lambda bb, i: (bb, i, 0)),
        out_shape=jax.ShapeDtypeStruct((b, s, GROUP_W), BF16),
        scratch_shapes=[pltpu.VMEM((t + 8, GROUP_W), F32), pltpu.VMEM((1, GROUP_W), F32)],
        compiler_params=_cparams(("parallel", "arbitrary")),
        name="rglru",
    )(proj, proj, conv_w, vec(conv_b), w_r.astype(BF16), vec(b_r), w_i.astype(BF16), vec(b_i),
      vec(lam))


def _inv_freq_row(d):
    i = np.arange(128) % (d // 2)
    return jnp.asarray((ROPE_THETA ** (-(2.0 * i) / d)).astype(np.float32).reshape(1, 128))


def _rope128(x, cos, sin_signed):
    return x * cos + pltpu.roll(x, 64, 1) * sin_signed


def _rope64(x, cos, sin_masked, lane):
    rot = jnp.where(lane < 32, -pltpu.roll(x, 96, 1), pltpu.roll(x, 32, 1))
    return x * cos + rot * sin_masked


def _rope_tables_kernel(pos_ref, invf64_ref, invf128_ref, o_ref):
    pos = pos_ref[0]
    lane = lax.broadcasted_iota(jnp.int32, pos.shape, 1)
    ang = pos * invf64_ref[...]
    o_ref[0, :, 0:128] = jnp.cos(ang)
    o_ref[0, :, 128:256] = jnp.where(lane < MLA_ROPE, jnp.sin(ang), 0.0)
    ang = pos * invf128_ref[...]
    sin = jnp.sin(ang)
    o_ref[0, :, 256:384] = jnp.cos(ang)
    o_ref[0, :, 384:512] = jnp.where(lane < 64, -sin, sin)


def _rope_tables(posb):
    b, s, _ = posb.shape
    t = min(512, s)
    full = lambda shape: pl.BlockSpec(shape, lambda bb, i: (0,) * len(shape))
    return pl.pallas_call(
        _rope_tables_kernel,
        grid=(b, s // t),
        in_specs=[pl.BlockSpec((1, t, 128), lambda bb, i: (bb, i, 0)), full((1, 128)), full((1, 128))],
        out_specs=pl.BlockSpec((1, t, 512), lambda bb, i: (bb, i, 0)),
        out_shape=jax.ShapeDtypeStruct((b, s, 512), F32),
        compiler_params=_cparams(("parallel", "parallel")),
        name="rope_tables",
    )(posb, _inv_freq_row(MLA_ROPE), _inv_freq_row(HEAD_DIM))


def _mla_prep_kernel(cq_ref, ckv_ref, misc_ref, rope_ref, qng_ref, wuq_ref, kvng_ref, wukv_ref,
                     qgn_ref, qgp_ref, kgn_ref, kgp_ref,
                     q_ref, k_ref, v_ref, *, T):
    scale = (MLA_NOPE + MLA_ROPE) ** -0.5 * LOG2E
    lane = lax.broadcasted_iota(jnp.int32, (T, 128), 1)
    cos = rope_ref[0, :, 0:128]
    sin = rope_ref[0, :, 128:256]

    cq = _rms(cq_ref[0], MLA_Q_RANK) * qng_ref[...]
    qf = _dot(cq.astype(BF16), wuq_ref[...])
    ckv = _rms(ckv_ref[0], MLA_KV_RANK) * kvng_ref[...]
    kvf = _dot(ckv.astype(BF16), wukv_ref[...])
    kpe = jnp.where(lane < MLA_ROPE, misc_ref[0], 0.0)
    kpe = _rope64(_rms(kpe, MLA_ROPE) * kgp_ref[...], cos, sin, lane)
    for h in range(N_HEADS):
        o = h * 256
        qn = _rms(qf[:, o:o + 128], MLA_NOPE) * qgn_ref[...]
        qp = _rope64(_rms(qf[:, o + 128:o + 256], MLA_ROPE) * qgp_ref[...], cos, sin, lane)
        q_ref[0, :, o:o + 128] = (qn * scale).astype(BF16)
        q_ref[0, :, o + 128:o + 256] = (qp * scale).astype(BF16)
        kn = _rms(kvf[:, o:o + 128], MLA_NOPE) * kgn_ref[...]
        k_ref[0, :, o:o + 128] = kn.astype(BF16)
        k_ref[0, :, o + 128:o + 256] = kpe.astype(BF16)
        v_ref[0, :, h * 128:(h + 1) * 128] = kvf[:, o + 128:o + 256].astype(BF16)


def _mla_prep(proj, rope, q_norm_g, w_uq, kv_norm_g, w_ukv, qk_norm_g):
    b, s, _ = proj.shape
    t = min(512, s)
    wuq = jnp.pad(w_uq, ((0, 0), (0, 0), (0, 64))).reshape(MLA_Q_RANK, N_HEADS * 256).astype(BF16)
    wukv = w_ukv.reshape(MLA_KV_RANK, N_HEADS * 256).astype(BF16)
    pad64 = lambda v: jnp.pad(v, (0, 64)).reshape(1, 128)
    full = lambda shape: pl.BlockSpec(shape, lambda bb, i: (0,) * len(shape))
    tok = lambda w: pl.BlockSpec((1, t, w), lambda bb, i: (bb, i, 0))
    return pl.pallas_call(
        functools.partial(_mla_prep_kernel, T=t),
        grid=(b, s // t),
        in_specs=[_seg_spec(t, MLA_Q_RANK, OFF_BCQ), _seg_spec(t, MLA_KV_RANK, OFF_BCKV),
                  _seg_spec(t, 128, OFF_MISC), pl.BlockSpec((1, t, 256), lambda bb, i: (bb, i, 0)),
                  full((1, MLA_Q_RANK)), full((MLA_Q_RANK, 1024)),
                  full((1, MLA_KV_RANK)), full((MLA_KV_RANK, 1024)),
                  full((1, 128)), full((1, 128)), full((1, 128)), full((1, 128))],
        out_specs=[tok(1024), tok(1024), tok(512)],
        out_shape=[jax.ShapeDtypeStruct((b, s, 1024), BF16),
                   jax.ShapeDtypeStruct((b, s, 1024), BF16),
                   jax.ShapeDtypeStruct((b, s, 512), BF16)],
        compiler_params=_cparams(("parallel", "parallel")),
        name="mla_prep",
    )(proj, proj, proj, rope, q_norm_g.reshape(1, -1), wuq, kv_norm_g.reshape(1, -1), wukv,
      qk_norm_g[0, :MLA_NOPE].reshape(1, 128), pad64(qk_norm_g[0, MLA_NOPE:]),
      qk_norm_g[1, :MLA_NOPE].reshape(1, 128), pad64(qk_norm_g[1, MLA_NOPE:]))


def _softmax_init(m_ref, l_ref, acc_ref):
    m_ref[...] = jnp.full_like(m_ref, NEG)
    l_ref[...] = jnp.zeros_like(l_ref)
    acc_ref[...] = jnp.zeros_like(acc_ref)


ROW_GROUPS = 2


def _attend_chunk(q, k_blk, v_blk, m_ref, l_ref, acc_ref, mask_fn=None):
    g_rows = q.shape[0] // ROW_GROUPS
    scores = [_dot_nt(q[g * g_rows:(g + 1) * g_rows], k_blk) for g in range(ROW_GROUPS)]
    for g in range(ROW_GROUPS):
        rows = slice(g * g_rows, (g + 1) * g_rows)
        s = scores[g]
        if mask_fn is not None:
            s = mask_fn(s, g * g_rows)
        m_old = m_ref[rows, :]
        m_new = jnp.maximum(m_old, jnp.max(s, axis=-1, keepdims=True))
        alpha = jnp.exp2(m_old - m_new)
        p = jnp.exp2(s - m_new)
        part = p[:, 0:128]
        for j in range(1, s.shape[1] // 128):
            part = part + p[:, j * 128:(j + 1) * 128]
        l_ref[rows, :] = alpha * l_ref[rows, :] + part
        acc_ref[rows, :] = alpha * acc_ref[rows, :] + _dot(p.astype(BF16), v_blk)
        m_ref[rows, :] = m_new


def _softmax_result(l_ref, acc_ref):
    return acc_ref[...] / jnp.sum(l_ref[...], axis=-1, keepdims=True)


def _mla_flash_kernel(q_ref, k_ref, v_ref, gate_ref, o_ref, m_ref, l_ref, acc_ref, *, tq, kc):
    i = pl.program_id(2)
    _softmax_init(m_ref, l_ref, acc_ref)
    q = q_ref[0]

    def chunk(c, masked):
        k0 = pl.multiple_of(c * kc, kc)

        def causal(s, first_row):
            row = lax.broadcasted_iota(jnp.int32, s.shape, 0) + (i * tq + first_row)
            col = lax.broadcasted_iota(jnp.int32, s.shape, 1) + k0
            return jnp.where(col <= row, s, NEG)

        _attend_chunk(q, k_ref[0, pl.ds(k0, kc), :], v_ref[0, pl.ds(k0, kc), :],
                      m_ref, l_ref, acc_ref, causal if masked else None)

    def full_chunk(c, carry):
        chunk(c, False)
        return carry

    n_full = (i * tq) // kc
    lax.fori_loop(0, n_full, full_chunk, 0)
    chunk(n_full, True)
    o_ref[0] = (_softmax_result(l_ref, acc_ref) * _silu(gate_ref[0])).astype(o_ref.dtype)


def _mla_flash(q, k, v, proj):
    b, s, _ = q.shape
    tq = min(512, s)
    kc = min(1024, s)
    gate_blk = OFF_BG // 128
    return pl.pallas_call(
        functools.partial(_mla_flash_kernel, tq=tq, kc=kc),
        grid=(b, N_HEADS, s // tq),
        in_specs=[pl.BlockSpec((1, tq, 256), lambda bb, h, i: (bb, i, h)),
                  pl.BlockSpec((1, s, 256), lambda bb, h, i: (bb, 0, h)),
                  pl.BlockSpec((1, s, 128), lambda bb, h, i: (bb, 0, h)),
                  pl.BlockSpec((1, tq, 128), lambda bb, h, i: (bb, i, gate_blk + h))],
        out_specs=pl.BlockSpec((1, tq, 128), lambda bb, h, i: (bb, i, h)),
        out_shape=jax.ShapeDtypeStruct((b, s, GROUP_W), BF16),
        scratch_shapes=[pltpu.VMEM((tq, 1), F32), pltpu.VMEM((tq, 128), F32),
                        pltpu.VMEM((tq, 128), F32)],
        compiler_params=_cparams(("parallel", "parallel", "arbitrary")),
        name="mla_flash",
    )(q, k, v, proj)


def _nsa_prep_kernel(dq_ref, dkv_ref, rope_ref, qg_ref, kg_ref,
                     q_ref, ks_ref, vs_ref, kw_ref, vw_ref, kc_ref, vc_ref, *, T):
    scale = HEAD_DIM ** -0.5 * LOG2E
    lane = lax.broadcasted_iota(jnp.int32, (T, 128), 1)
    cos = rope_ref[0, :, 0:128]
    sin = rope_ref[0, :, 128:256]
    dq = dq_ref[0]
    for h in range(N_HEADS):
        qh = _rms(dq[:, h * 128:(h + 1) * 128], HEAD_DIM) * qg_ref[...]
        q_ref[0, :, h * 128:(h + 1) * 128] = _rope128(qh, cos, sin) * scale
    kv = dkv_ref[0]
    kc_ref[0] = kv[:, 0:128]
    vc_ref[0] = kv[:, 128:256]
    ks_ref[0, :, 0:128] = _rope128(_rms(kv[:, 256:384], HEAD_DIM) * kg_ref[1:2], cos, sin).astype(BF16)
    key_blk = (pl.program_id(1) * T + lax.broadcasted_iota(jnp.int32, (T, 128), 0)) >> 6
    ks_ref[0, :, 128:256] = (lane == key_blk).astype(BF16)
    vs_ref[0] = kv[:, 384:512].astype(BF16)
    kw_ref[0] = _rope128(_rms(kv[:, 512:640], HEAD_DIM) * kg_ref[2:3], cos, sin).astype(BF16)
    vw_ref[0] = kv[:, 640:768].astype(BF16)


def _nsa_prep(proj, rope, q_norm_g, k_norm_g):
    b, s, _ = proj.shape
    t = min(512, s)
    full = lambda shape: pl.BlockSpec(shape, lambda bb, i: (0,) * len(shape))
    tok = lambda w: pl.BlockSpec((1, t, w), lambda bb, i: (bb, i, 0))
    sds = lambda w, dt: jax.ShapeDtypeStruct((b, s, w), dt)
    return pl.pallas_call(
        functools.partial(_nsa_prep_kernel, T=t),
        grid=(b, s // t),
        in_specs=[_seg_spec(t, GROUP_W, OFF_DQ), _seg_spec(t, 768, OFF_DKV),
                  pl.BlockSpec((1, t, 256), lambda bb, i: (bb, i, 1)),
                  full((1, 128)), full((3, 128))],
        out_specs=[tok(512), tok(256)] + [tok(128)] * 5,
        out_shape=[sds(512, F32), sds(256, BF16), sds(128, BF16), sds(128, BF16), sds(128, BF16),
                   sds(128, F32), sds(128, F32)],
        compiler_params=_cparams(("parallel", "parallel")),
        name="nsa_prep",
    )(proj, proj, rope, q_norm_g.reshape(1, HEAD_DIM), k_norm_g)


def _nsa_cmp_kernel(kt_ref, vt_ref, pe_ref, w1_ref, b1_ref, w2_ref, b2_ref, kg_ref, pos_ref,
                    invf_ref, kc_ref, vc_ref, *, NC):
    half = CMP_STRIDE * HEAD_DIM
    outs = []
    for j, t_ref in enumerate((kt_ref, vt_ref)):
        t2 = t_ref[0].astype(BF16)
        first = _dot(t2, w1_ref[j, :half, :])
        second = pltpu.roll(_dot(t2, w1_ref[j, half:, :]), NC - 1, 0)
        pe8 = jnp.broadcast_to(pe_ref[j], (8, CMP_LEN * HEAD_DIM)).astype(BF16)
        bias = _dot(pe8, w1_ref[j])[0:1] + b1_ref[j]
        hid = _silu(first + second + bias)
        outs.append(_dot(hid.astype(BF16), w2_ref[j]) + b2_ref[j])
    k_c, v_c = outs
    lane = lax.broadcasted_iota(jnp.int32, (NC, 128), 1)
    ang = pos_ref[0] * invf_ref[...]
    sin = jnp.sin(ang)
    sin = jnp.where(lane < 64, -sin, sin)
    kc_ref[0] = _rope128(_rms(k_c, HEAD_DIM) * kg_ref[...], jnp.cos(ang), sin)
    vc_ref[0] = v_c


def _nsa_compress(kc_raw, vc_raw, posc, cmp_pe, cmp_w1, cmp_b1, cmp_w2, cmp_b2, kg0):
    b, s, _ = kc_raw.shape
    nc = s // CMP_STRIDE
    kt = kc_raw.reshape(b, nc, CMP_STRIDE * HEAD_DIM)
    vt = vc_raw.reshape(b, nc, CMP_STRIDE * HEAD_DIM)
    full = lambda shape: pl.BlockSpec(shape, lambda bb: (0,) * len(shape))
    per_b = lambda shape: pl.BlockSpec((1,) + shape, lambda bb: (bb, 0, 0))
    return pl.pallas_call(
        functools.partial(_nsa_cmp_kernel, NC=nc),
        grid=(b,),
        in_specs=[per_b((nc, CMP_STRIDE * HEAD_DIM)), per_b((nc, CMP_STRIDE * HEAD_DIM)),
                  full((2, 1, CMP_LEN * HEAD_DIM)), full((2, CMP_LEN * HEAD_DIM, CMP_HIDDEN)),
                  full((2, 1, CMP_HIDDEN)), full((2, CMP_HIDDEN, HEAD_DIM)),
                  full((2, 1, HEAD_DIM)), full((1, HEAD_DIM)), per_b((nc, 128)), full((1, 128))],
        out_specs=[per_b((nc, HEAD_DIM)), per_b((nc, HEAD_DIM))],
        out_shape=[jax.ShapeDtypeStruct((b, nc, HEAD_DIM), F32),
                   jax.ShapeDtypeStruct((b, nc, HEAD_DIM), F32)],
        compiler_params=_cparams(("parallel",)),
        name="nsa_compress",
    )(kt, vt, cmp_pe.reshape(2, 1, CMP_LEN * HEAD_DIM), cmp_w1.astype(BF16),
      cmp_b1.reshape(2, 1, CMP_HIDDEN), cmp_w2.astype(BF16), cmp_b2.reshape(2, 1, HEAD_DIM),
      kg0.reshape(1, HEAD_DIM), posc, _inv_freq_row(HEAD_DIM))


def _nsa_attn_kernel(q_ref, kc_ref, vc_ref, ks_ref, vs_ref, kw_ref, vw_ref, misc_ref, gate_ref,
                     o_ref, m_ref, l_ref, acc_ref, *, S, NC, KC, WK):
    Q = Q_BLOCK
    R = N_HEADS
    i = pl.program_id(1)
    n_sel = S // SEL_LEN
    n_cmp = (S - CMP_LEN) // CMP_STRIDE + 1
    top_k = min(SEL_TOPK, n_sel)

    qf = q_ref[0]
    q4 = jnp.concatenate([qf[:, r * 128:(r + 1) * 128] for r in range(R)], axis=0)
    q4b = q4.astype(BF16)

    w0 = pl.multiple_of(jnp.maximum(i * Q - WINDOW, 0), Q)
    hi_w = (i * Q - w0) + lax.broadcasted_iota(jnp.int32, (Q, WK), 0)
    col_w = lax.broadcasted_iota(jnp.int32, (Q, WK), 1)
    band = jnp.where((col_w <= hi_w) & (col_w > hi_w - WINDOW), 0.0, NEG)
    s_w = _dot_nt(q4b, kw_ref[0, pl.ds(w0, WK), :]) + jnp.concatenate([band] * R, axis=0)
    p_w = jnp.exp2(s_w - jnp.max(s_w, axis=-1, keepdims=True))
    o_w = (_dot(p_w.astype(BF16), vw_ref[0, pl.ds(w0, WK), :])
           / jnp.sum(p_w, axis=-1, keepdims=True))

    q_hi, q_lo = _split_bf16(q4)
    k_hi, k_lo = _split_bf16(kc_ref[0])
    s_c = (_dot_nt(jnp.concatenate([q_hi, q_lo], axis=1), jnp.concatenate([k_hi, k_hi], axis=1))
           + _dot_nt(q_hi, k_lo))
    t_c = i * Q + (lax.broadcasted_iota(jnp.int32, (R * Q, NC), 0) & (Q - 1))
    c_ix = lax.broadcasted_iota(jnp.int32, (R * Q, NC), 1)
    valid_c = (c_ix * CMP_STRIDE + (CMP_LEN - 1) <= t_c) & (c_ix < n_cmp)
    s_c = jnp.where(valid_c, s_c, NEG)
    p_c = jnp.where(valid_c, jnp.exp2(s_c - jnp.max(s_c, axis=-1, keepdims=True)), 0.0)
    p_c = p_c / jnp.maximum(jnp.sum(p_c, axis=-1, keepdims=True), 1e-30)
    o_c = _bdot(p_c, vc_ref[0])
    assert NC & (NC - 1) == 0
    c_o = (lax.broadcasted_iota(jnp.int32, (2 * NC, 128), 0) & (NC - 1)) * CMP_STRIDE
    n_o = lax.broadcasted_iota(jnp.int32, (2 * NC, 128), 1) * SEL_LEN
    overlap2 = ((c_o < n_o + SEL_LEN) & (c_o + (CMP_LEN - 1) >= n_o)).astype(BF16)
    imp4 = _dot(jnp.concatenate(_split_bf16(p_c), axis=1), overlap2)
    imp = (imp4[0:Q] + imp4[Q:2 * Q] + imp4[2 * Q:3 * Q] + imp4[3 * Q:4 * Q]).T

    NP = min(128, -(-n_sel // 8) * 8)
    t_q = i * Q + lax.broadcasted_iota(jnp.int32, (NP, Q), 1)
    n_ix = lax.broadcasted_iota(jnp.int32, (NP, Q), 0)
    cur = t_q >> 6
    valid_s = (n_ix * SEL_LEN <= t_q) & (n_ix < n_sel)
    forced = (n_ix == 0) | (n_ix == cur) | (n_ix == cur - 1)
    val = jnp.where(valid_s, imp[:NP], -1.0)
    val = jnp.where(forced & valid_s, FORCE, val)
    val = jnp.where(n_ix < n_sel, val, -2.0)
    sel_t = jnp.zeros((NP, Q), F32)
    n_f = n_ix.astype(F32)
    for _ in range(top_k):
        mx = jnp.max(val, axis=0, keepdims=True)
        first = jnp.min(jnp.where(val == mx, n_f, 1e9), axis=0, keepdims=True)
        hit = n_f == first
        sel_t = jnp.where(hit, 1.0, sel_t)
        val = jnp.where(hit, -3.0, val)
    sel_t = jnp.where(valid_s, sel_t, 0.0)
    if NP < 128:
        sel_t = jnp.concatenate([sel_t, jnp.zeros((128 - NP, Q), F32)], axis=0)
    blk_bias = jnp.where(sel_t.T > 0.5, 0.0, NEG).astype(BF16)
    q_aug = jnp.concatenate([q4b, jnp.concatenate([blk_bias] * R, axis=0)], axis=1)

    _softmax_init(m_ref, l_ref, acc_ref)

    def sel_chunk(c, causal):
        k0 = pl.multiple_of(c * KC, KC)

        def causal_mask(s, first_row):
            t_k = i * Q + (lax.broadcasted_iota(jnp.int32, s.shape, 0) & (Q - 1))
            return jnp.where(k0 + lax.broadcasted_iota(jnp.int32, s.shape, 1) <= t_k, s, NEG)

        _attend_chunk(q_aug, ks_ref[0, pl.ds(k0, KC), :], vs_ref[0, pl.ds(k0, KC), :],
                      m_ref, l_ref, acc_ref, causal_mask if causal else None)

    def past_chunk(c, carry):
        sel_chunk(c, False)
        return carry

    n_past = (i * Q) // KC
    lax.fori_loop(0, n_past, past_chunk, 0)
    sel_chunk(n_past, True)
    o_s = _softmax_result(l_ref, acc_ref)

    gates = jax.nn.sigmoid(misc_ref[0])
    gate = gate_ref[0]
    for r in range(R):
        g0 = gates[:, MISC_GL + 3 * r:MISC_GL + 3 * r + 1]
        g1 = gates[:, MISC_GL + 3 * r + 1:MISC_GL + 3 * r + 2]
        g2 = gates[:, MISC_GL + 3 * r + 2:MISC_GL + 3 * r + 3]
        rows = slice(r * Q, (r + 1) * Q)
        o = g0 * o_c[rows] + g1 * o_s[rows] + g2 * o_w[rows]
        o_ref[0, :, r * 128:(r + 1) * 128] = (
            o * _silu(gate[:, r * 128:(r + 1) * 128])).astype(o_ref.dtype)


def _nsa_attn(qf, k_c, v_c, ks, vs, kw, vw, proj):
    b, s, _ = qf.shape
    nc = s // CMP_STRIDE
    kc_len = min(1024, s)
    wk = min(WINDOW + Q_BLOCK, s)
    per_b = lambda n, w: pl.BlockSpec((1, n, w), lambda bb, i: (bb, 0, 0))
    return pl.pallas_call(
        functools.partial(_nsa_attn_kernel, S=s, NC=nc, KC=kc_len, WK=wk),
        grid=(b, s // Q_BLOCK),
        in_specs=[pl.BlockSpec((1, Q_BLOCK, GROUP_W), lambda bb, i: (bb, i, 0)),
                  per_b(nc, 128), per_b(nc, 128),
                  per_b(s, 256), per_b(s, 128), per_b(s, 128), per_b(s, 128),
                  _seg_spec(Q_BLOCK, 128, OFF_MISC), _seg_spec(Q_BLOCK, GROUP_W, OFF_DG)],
        out_specs=pl.BlockSpec((1, Q_BLOCK, GROUP_W), lambda bb, i: (bb, i, 0)),
        out_shape=jax.ShapeDtypeStruct((b, s, GROUP_W), BF16),
        scratch_shapes=[pltpu.VMEM((N_HEADS * Q_BLOCK, 1), F32),
                        pltpu.VMEM((N_HEADS * Q_BLOCK, 128), F32),
                        pltpu.VMEM((N_HEADS * Q_BLOCK, 128), F32)],
        compiler_params=_cparams(("parallel", "arbitrary")),
        name="nsa_attn",
    )(qf, k_c, v_c, ks, vs, kw, vw, proj, proj)


def _reorder_w_in_kernel(w_ref, o_ref):
    o = np.concatenate([[0], np.cumsum(IN_SIZES)]).tolist()
    (a_qkv, a_decay, a_beta, a_gate, b_cq, b_ckv, b_kpe, b_gate, c_x, c_gate,
     d_q, d_kv, d_gl, d_gate) = range(len(IN_SIZES))
    for l in range(w_ref.shape[1]):
        seg = lambda k: w_ref[o[k]:o[k + 1], l, :]
        off = 0
        for k in (a_qkv, a_gate, b_gate, c_x, c_gate, d_q, d_gate, d_kv, b_cq):
            o_ref[l, off:off + IN_SIZES[k], :] = seg(k).astype(BF16)
            off += IN_SIZES[k]
        assert off == OFF_MISC
        zeros = jnp.zeros((128 - MLA_ROPE - 2 * N_HEADS - 3 * N_HEADS, w_ref.shape[2]), F32)
        misc = jnp.concatenate([seg(b_kpe), seg(a_decay), seg(a_beta), seg(d_gl), zeros], axis=0)
        o_ref[l, OFF_MISC:OFF_MISC + 128, :] = misc.astype(BF16)
        o_ref[l, OFF_BCKV:OFF_BCKV + MLA_KV_RANK, :] = seg(b_ckv).astype(BF16)


def _reorder_w_in(w_in):
    depth, d, d_in = w_in.shape
    tc = 128
    w_t = jnp.transpose(w_in, (2, 0, 1))
    return pl.pallas_call(
        _reorder_w_in_kernel,
        grid=(d // tc,),
        in_specs=[pl.BlockSpec((d_in, depth, tc), lambda i: (0, 0, i))],
        out_specs=pl.BlockSpec((depth, D_PROJ, tc), lambda i: (0, 0, i)),
        out_shape=jax.ShapeDtypeStruct((depth, D_PROJ, d), BF16),
        compiler_params=_cparams(("parallel",)),
        name="reorder_w_in",
    )(w_t)


def kernel(x, positions, norm_g, w_in, w_out, gdn_conv_w, gdn_a_log, gdn_dt_bias, gdn_norm_g,
           mla_q_norm_g, mla_w_uq, mla_kv_norm_g, mla_w_ukv, mla_qk_norm_g,
           lru_conv_w, lru_conv_b, lru_w_r, lru_b_r, lru_w_i, lru_b_i, lru_lambda,
           nsa_q_norm_g, nsa_k_norm_g, nsa_cmp_pe, nsa_cmp_w1, nsa_cmp_b1, nsa_cmp_w2, nsa_cmp_b2):
    b, s, d = x.shape
    depth = w_in.shape[0]
    w_in_r = _reorder_w_in(w_in)
    w_out_b = w_out.astype(BF16)
    posf = positions.astype(F32)
    rope = _rope_tables(jnp.broadcast_to(posf[:, :, None], (b, s, 128)))
    nc = s // CMP_STRIDE
    pos_end = jnp.pad(posf.reshape(b, nc, CMP_STRIDE)[:, 1:, CMP_STRIDE - 1], ((0, 0), (0, 1)))
    posc = jnp.broadcast_to(pos_end[:, :, None], (b, nc, 128))

    x2d = x.reshape(b * s, d)
    for l in range(depth):
        proj = _inproj(x2d, norm_g[l].reshape(1, d), w_in_r, l).reshape(b, s, D_PROJ)
        y_a = _gdn(proj, gdn_conv_w[l], gdn_a_log[l], gdn_dt_bias[l], gdn_norm_g[l])
        q_b, k_b, v_b = _mla_prep(proj, rope, mla_q_norm_g[l], mla_w_uq[l], mla_kv_norm_g[l],
                                  mla_w_ukv[l], mla_qk_norm_g[l])
        y_b = _mla_flash(q_b, k_b, v_b, proj)
        y_c = _lru(proj, lru_conv_w[l], lru_conv_b[l], lru_w_r[l], lru_b_r[l], lru_w_i[l],
                   lru_b_i[l], lru_lambda[l])
        q_d, ks, vs, kw, vw, kc_raw, vc_raw = _nsa_prep(proj, rope, nsa_q_norm_g[l],
                                                        nsa_k_norm_g[l])
        k_c, v_c = _nsa_compress(kc_raw, vc_raw, posc, nsa_cmp_pe[l], nsa_cmp_w1[l],
                                 nsa_cmp_b1[l], nsa_cmp_w2[l], nsa_cmp_b2[l], nsa_k_norm_g[l, 0])
        y_d = _nsa_attn(q_d, k_c, v_c, ks, vs, kw, vw, proj)
        ys = [y.reshape(b * s, GROUP_W) for y in (y_a, y_b, y_c, y_d)]
        x2d = _outproj(ys, w_out_b[l], x2d)
    return x2d.reshape(b, s, d)
```

```python
import functools
import math

import numpy as np
import jax
import jax.numpy as jnp
from jax import lax
from jax.experimental import pallas as pl
from jax.experimental.pallas import tpu as pltpu

F32 = jnp.float32
BF16 = jnp.bfloat16
HIGHEST = lax.Precision.HIGHEST

D_MODEL = 2048
GROUP_W = 512
HEAD_DIM = 128
N_HEADS = 4
RMS_EPS = 1e-6
ROPE_THETA = 10000.0
NEG = -1e30
LOG2E = math.log2(math.e)
GDN_CONV = 4
GDN_CHUNK = 128
MLA_Q_RANK = 384
MLA_KV_RANK = 256
MLA_NOPE = 128
MLA_ROPE = 64
LRU_CONV = 4
LRU_C = 8.0
CMP_LEN = 32
CMP_STRIDE = 16
CMP_HIDDEN = 256
SEL_LEN = 64
SEL_TOPK = 16
WINDOW = 512
FORCE = 1e9
Q_BLOCK = 128

IN_SIZES = (3 * GROUP_W, N_HEADS, N_HEADS, GROUP_W,
            MLA_Q_RANK, MLA_KV_RANK, MLA_ROPE, GROUP_W,
            GROUP_W, GROUP_W,
            GROUP_W, 6 * HEAD_DIM, 3 * N_HEADS, GROUP_W)

OFF_AQ, OFF_AK, OFF_AV, OFF_AG = 0, 512, 1024, 1536
OFF_BG, OFF_CX, OFF_CG, OFF_DQ, OFF_DG = 2048, 2560, 3072, 3584, 4096
OFF_DKV, OFF_BCQ, OFF_MISC, OFF_BCKV = 4608, 5376, 5760, 5888
D_PROJ = 6144
MISC_KPE, MISC_DECAY, MISC_BETA, MISC_GL = 0, 64, 68, 72

VMEM_LIMIT = 56 * 1024 * 1024


def _cparams(sem):
    return pltpu.CompilerParams(dimension_semantics=sem, vmem_limit_bytes=VMEM_LIMIT)


def _dot(a, b, precision=None):
    return lax.dot_general(a, b, (((1,), (0,)), ((), ())), precision=precision,
                           preferred_element_type=F32)


def _dot_nt(a, b, precision=None):
    return lax.dot_general(a, b, (((1,), (1,)), ((), ())), precision=precision,
                           preferred_element_type=F32)


def _bdot(a, b):
    return _dot(a.astype(BF16), b.astype(BF16))


def _bdot_nt(a, b):
    return _dot_nt(a.astype(BF16), b.astype(BF16))


def _split_bf16(x):
    hi = x.astype(BF16)
    return hi, (x - hi.astype(F32)).astype(BF16)


def _silu(x):
    return x * jax.nn.sigmoid(x)


def _softplus(x):
    return jnp.maximum(x, 0.0) + jnp.log1p(jnp.exp(-jnp.abs(x)))


def _rms(x, n):
    return x * lax.rsqrt(jnp.sum(x * x, axis=-1, keepdims=True) * (1.0 / n) + RMS_EPS)


def _inproj_kernel(x_ref, g_ref, w_ref, o_ref, h_ref, *, tm):
    @pl.when(pl.program_id(1) == 0)
    def _():
        rows = min(256, tm)
        for r in range(tm // rows):
            x = x_ref[r * rows:(r + 1) * rows, :]
            h_ref[r * rows:(r + 1) * rows, :] = (_rms(x, D_MODEL) * g_ref[...]).astype(BF16)

    o_ref[...] = _dot_nt(h_ref[...], w_ref[...])


def _inproj(x2d, g, w_t, layer):
    m = x2d.shape[0]
    tm = min(1024, m)
    tn = 1024
    return pl.pallas_call(
        functools.partial(_inproj_kernel, tm=tm),
        grid=(m // tm, D_PROJ // tn),
        in_specs=[pl.BlockSpec((tm, D_MODEL), lambda i, j: (i, 0)),
                  pl.BlockSpec((1, D_MODEL), lambda i, j: (0, 0)),
                  pl.BlockSpec((None, tn, D_MODEL), lambda i, j: (layer, j, 0))],
        out_specs=pl.BlockSpec((tm, tn), lambda i, j: (i, j)),
        out_shape=jax.ShapeDtypeStruct((m, D_PROJ), F32),
        scratch_shapes=[pltpu.VMEM((tm, D_MODEL), BF16)],
        compiler_params=_cparams(("parallel", "arbitrary")),
        name="inproj",
    )(x2d, g, w_t)


def _outproj_kernel(ya_ref, yb_ref, yc_ref, yd_ref, wa_ref, wb_ref, wc_ref, wd_ref, x_ref, o_ref):
    acc = x_ref[...]
    for y_ref, w_ref in ((ya_ref, wa_ref), (yb_ref, wb_ref), (yc_ref, wc_ref), (yd_ref, wd_ref)):
        acc = acc + _dot(y_ref[...].astype(BF16), w_ref[...])
    o_ref[...] = acc


def _outproj(ys, w, x2d):
    m = x2d.shape[0]
    tm = min(1024, m)
    tn = 1024
    y_specs = [pl.BlockSpec((tm, GROUP_W), lambda i, j: (i, 0)) for _ in range(4)]
    w_specs = [pl.BlockSpec((GROUP_W, tn), functools.partial(lambda i, j, g: (g, j), g=g))
               for g in range(4)]
    return pl.pallas_call(
        _outproj_kernel,
        grid=(m // tm, D_MODEL // tn),
        in_specs=y_specs + w_specs + [pl.BlockSpec((tm, tn), lambda i, j: (i, j))],
        out_specs=pl.BlockSpec((tm, tn), lambda i, j: (i, j)),
        out_shape=jax.ShapeDtypeStruct((m, D_MODEL), F32),
        compiler_params=_cparams(("parallel", "arbitrary")),
        name="outproj",
    )(*ys, w, w, w, w, x2d)


def _causal_conv4(xin, ext_ref, w):
    t = xin.shape[0]
    ext_ref[8:8 + t, :] = xin
    acc = ext_ref[5:5 + t, :] * w[0:1]
    for j in range(1, 4):
        acc = acc + ext_ref[5 + j:5 + j + t, :] * w[j:j + 1]
    ext_ref[0:8, :] = xin[t - 8:t]
    return acc


def _gdn_kernel(q_ref, k_ref, v_ref, gate_ref, misc_ref, cw_ref, alog_ref, dtb_ref, ng_ref,
                o_ref, hist_ref, state_ref, *, T):
    C = GDN_CHUNK

    @pl.when(pl.program_id(1) == 0)
    def _():
        hist_ref[:, 0:8, :] = jnp.zeros((3, 8, GROUP_W), F32)
        state_ref[...] = jnp.zeros_like(state_ref)

    cw = cw_ref[...]
    conv = []
    for idx, ref in enumerate((q_ref, k_ref, v_ref)):
        xin = ref[0]
        acc = _causal_conv4(xin, hist_ref.at[idx], cw[:, idx * GROUP_W:(idx + 1) * GROUP_W])
        conv.append(_silu(acc))
    qc, kc, vc = conv

    misc = misc_ref[0]
    gfull = -jnp.exp(alog_ref[...]) * _softplus(misc + dtb_ref[...])
    betafull = jax.nn.sigmoid(misc)
    gate = gate_ref[0]

    row = lax.broadcasted_iota(jnp.int32, (C, C), 0)
    col = lax.broadcasted_iota(jnp.int32, (C, C), 1)
    lower = row >= col
    strict = row > col
    ltri = lower.astype(F32)
    row2 = lax.broadcasted_iota(jnp.int32, (C, 2 * C), 0)
    lane2 = lax.broadcasted_iota(jnp.int32, (C, 2 * C), 1)
    left = lane2 < C
    col2 = lane2 & (C - 1)
    leaf_bits = 4
    same_blk = {bits: (row2 >> bits) == (col2 >> bits) for bits in range(leaf_bits, 8)}
    assert C == HEAD_DIM == 1 << 7

    def pairdot(x, y):
        yb = y.astype(BF16)
        zero = jnp.zeros_like(yb)
        bd = jnp.concatenate([jnp.where(left, yb, zero), jnp.where(left, zero, yb)], axis=0)
        return _dot(x.astype(BF16), bd)

    n_chunks = T // C
    hds = []
    for c in range(n_chunks):
        r0 = c * C
        gall = _dot(ltri, gfull[r0:r0 + C], HIGHEST)
        hd = []
        hds.append(hd)
        for h in range(N_HEADS):
            l0 = h * HEAD_DIM
            q = qc[r0:r0 + C, l0:l0 + HEAD_DIM]
            k = kc[r0:r0 + C, l0:l0 + HEAD_DIM]
            v = vc[r0:r0 + C, l0:l0 + HEAD_DIM]
            q = q * lax.rsqrt(jnp.sum(q * q, axis=-1, keepdims=True) + RMS_EPS) * (HEAD_DIM ** -0.5)
            k = k * lax.rsqrt(jnp.sum(k * k, axis=-1, keepdims=True) + RMS_EPS)
            beta = betafull[r0:r0 + C, MISC_BETA + h:MISC_BETA + h + 1]
            gc = jnp.broadcast_to(gall[:, MISC_DECAY + h:MISC_DECAY + h + 1], (C, C))
            decay = jnp.exp(jnp.where(lower, gc - gc.T, NEG))
            kb = k * beta
            a = jnp.where(strict, _bdot_nt(kb, k) * decay, 0.0)
            hd.append((q, k, v, beta, gc, decay, kb, a))

    a2s = [jnp.concatenate([hds[c][h0][7], hds[c][h0 + 1][7]], axis=1)
           for c in range(n_chunks) for h0 in range(0, N_HEADS, 2)]
    ps = [jnp.where(same_blk[leaf_bits], -a2, 0.0) for a2 in a2s]
    tm2 = list(ps)
    for _ in range(leaf_bits - 1):
        ps = [pairdot(p, p) for p in ps]
        tm2 = [tm + p + pairdot(tm, p) for tm, p in zip(tm2, ps)]
    for bits in range(leaf_bits + 1, 8):
        join = same_blk[bits] & jnp.logical_not(same_blk[bits - 1])
        tas = [jnp.where(join, a2, 0.0) for a2 in a2s]
        tas = [off + pairdot(tm, off) for tm, off in zip(tm2, tas)]
        tm2 = [tm - (ta + pairdot(ta, tm)) for tm, ta in zip(tm2, tas)]

    for c in range(n_chunks):
        r0 = c * C
        hd = hds[c]
        tms = []
        for tm in tm2[2 * c:2 * c + 2]:
            tms += [tm[:, :C], tm[:, C:]]
        for h in range(N_HEADS):
            l0 = h * HEAD_DIM
            q, k, v, beta, gc, decay, kb, _ = hd[h]
            eg = jnp.exp(gc)
            rhs = jnp.concatenate([v * beta, kb * eg], axis=-1)
            uw = rhs + _bdot(tms[h], rhs)
            u = uw[:, :HEAD_DIM]
            w = uw[:, HEAD_DIM:]
            qk = _bdot_nt(q, k) * decay
            q_dec = q * eg
            g_last = gc[C - 1:C, :]
            k_end = k * jnp.exp(g_last - gc)
            c_dec = jnp.exp(g_last)
            state = state_ref[h]
            v_new = u - _bdot(w, state)
            o = _bdot(q_dec, state) + _bdot(qk, v_new)
            state_ref[h] = state * c_dec + lax.dot_general(
                k_end.astype(BF16), v_new.astype(BF16), (((0,), (0,)), ((), ())),
                preferred_element_type=F32)
            o = _rms(o, HEAD_DIM) * ng_ref[...]
            o_ref[0, r0:r0 + C, l0:l0 + HEAD_DIM] = (
                o * _silu(gate[r0:r0 + C, l0:l0 + HEAD_DIM])).astype(o_ref.dtype)


def _seg_spec(t, width, off):
    blk = off // width
    assert blk * width == off
    return pl.BlockSpec((1, t, width), lambda b, i: (b, i, blk))


def _gdn(proj, conv_w, a_log, dt_bias, norm_g):
    b, s, _ = proj.shape
    t = min(512, s)
    pad = lambda vec, off: jnp.zeros((1, 128), F32).at[0, off:off + N_HEADS].set(vec)
    full = lambda shape: pl.BlockSpec(shape, lambda bb, i: (0,) * len(shape))
    return pl.pallas_call(
        functools.partial(_gdn_kernel, T=t),
        grid=(b, s // t),
        in_specs=[_seg_spec(t, GROUP_W, OFF_AQ), _seg_spec(t, GROUP_W, OFF_AK),
                  _seg_spec(t, GROUP_W, OFF_AV), _seg_spec(t, GROUP_W, OFF_AG),
                  _seg_spec(t, 128, OFF_MISC),
                  full((GDN_CONV, 3 * GROUP_W)), full((1, 128)), full((1, 128)), full((1, 128))],
        out_specs=pl.BlockSpec((1, t, GROUP_W), lambda bb, i: (bb, i, 0)),
        out_shape=jax.ShapeDtypeStruct((b, s, GROUP_W), BF16),
        scratch_shapes=[pltpu.VMEM((3, t + 8, GROUP_W), F32),
                        pltpu.VMEM((N_HEADS, HEAD_DIM, HEAD_DIM), F32)],
        compiler_params=_cparams(("parallel", "arbitrary")),
        name="gdn",
    )(proj, proj, proj, proj, proj, conv_w, pad(a_log, MISC_DECAY), pad(dt_bias, MISC_DECAY),
      norm_g.reshape(1, HEAD_DIM))


def _lru_kernel(x_ref, gate_ref, cw_ref, cb_ref, wr_ref, br_ref, wi_ref, bi_ref, lam_ref,
                o_ref, hist_ref, h_ref, *, T):
    @pl.when(pl.program_id(1) == 0)
    def _():
        hist_ref[0:8, :] = jnp.zeros((8, GROUP_W), F32)
        h_ref[...] = jnp.zeros_like(h_ref)

    xin = x_ref[0]
    xc = _causal_conv4(xin, hist_ref, cw_ref[...]) + cb_ref[...]
    rs, igs = [], []
    for h in range(N_HEADS):
        xh = xc[:, h * HEAD_DIM:(h + 1) * HEAD_DIM].astype(BF16)
        rs.append(_dot(xh, wr_ref[h]))
        igs.append(_dot(xh, wi_ref[h]))
    r = jax.nn.sigmoid(jnp.concatenate(rs, axis=-1) + br_ref[...])
    ig = jax.nn.sigmoid(jnp.concatenate(igs, axis=-1) + bi_ref[...])
    log_a = -LRU_C * r * _softplus(-lam_ref[...])
    a = jnp.exp(log_a)
    bb = jnp.sqrt(-jnp.tanh(log_a) * (a * a + 1.0)) * (ig * xc)
    rowi = lax.broadcasted_iota(jnp.int32, (T, GROUP_W), 0)
    d = 1
    while d < T:
        keep = rowi >= d
        a_sh = jnp.where(keep, pltpu.roll(a, d, 0), 1.0)
        b_sh = jnp.where(keep, pltpu.roll(bb, d, 0), 0.0)
        bb = a * b_sh + bb
        a = a * a_sh
        d *= 2
    hseq = bb + a * h_ref[...]
    h_ref[...] = hseq[T - 1:T]
    o_ref[0] = (hseq * _silu(gate_ref[0])).astype(o_ref.dtype)


def _lru(proj, conv_w, conv_b, w_r, b_r, w_i, b_i, lam):
    b, s, _ = proj.shape
    t = min(256, s)
    full = lambda shape: pl.BlockSpec(shape, lambda bb, i: (0,) * len(shape))
    vec = lambda v: v.reshape(1, GROUP_W)
    return pl.pallas_call(
        functools.partial(_lru_kernel, T=t),
        grid=(b, s // t),
        in_specs=[_seg_spec(t, GROUP_W, OFF_CX), _seg_spec(t, GROUP_W, OFF_CG),
                  full((LRU_CONV, GROUP_W)), full((1, GROUP_W)),
                  full((N_HEADS, HEAD_DIM, HEAD_DIM)), full((1, GROUP_W)),
                  full((N_HEADS, HEAD_DIM, HEAD_DIM)), full((1, GROUP_W)), full((1, GROUP_W))],
        out_specs=pl.BlockSpec((1, t, GROUP_W), lambda bb, i: (bb, i, 0)),
        out_shape=jax.ShapeDtypeStruct((b, s, GROUP_W), BF16),
        scratch_shapes=[pltpu.VMEM((t + 8, GROUP_W), F32), pltpu.VMEM((1, GROUP_W), F32)],
        compiler_params=_cparams(("parallel", "arbitrary")),
        name="rglru",
    )(proj, proj, conv_w, vec(conv_b), w_r.astype(BF16), vec(b_r), w_i.astype(BF16), vec(b_i),
      vec(lam))


def _inv_freq_row(d):
    i = np.arange(128) % (d // 2)
    return jnp.asarray((ROPE_THETA ** (-(2.0 * i) / d)).astype(np.float32).reshape(1, 128))


def _rope128(x, cos, sin_signed):
    return x * cos + pltpu.roll(x, 64, 1) * sin_signed


def _rope64(x, cos, sin_masked, lane):
    rot = jnp.where(lane < 32, -pltpu.roll(x, 96, 1), pltpu.roll(x, 32, 1))
    return x * cos + rot * sin_masked


def _rope_tables_kernel(pos_ref, invf64_ref, invf128_ref, o_ref):
    pos = pos_ref[0]
    lane = lax.broadcasted_iota(jnp.int32, pos.shape, 1)
    ang = pos * invf64_ref[...]
    o_ref[0, :, 0:128] = jnp.cos(ang)
    o_ref[0, :, 128:256] = jnp.where(lane < MLA_ROPE, jnp.sin(ang), 0.0)
    ang = pos * invf128_ref[...]
    sin = jnp.sin(ang)
    o_ref[0, :, 256:384] = jnp.cos(ang)
    o_ref[0, :, 384:512] = jnp.where(lane < 64, -sin, sin)


def _rope_tables(posb):
    b, s, _ = posb.shape
    t = min(512, s)
    full = lambda shape: pl.BlockSpec(shape, lambda bb, i: (0,) * len(shape))
    return pl.pallas_call(
        _rope_tables_kernel,
        grid=(b, s // t),
        in_specs=[pl.BlockSpec((1, t, 128), lambda bb, i: (bb, i, 0)), full((1, 128)), full((1, 128))],
        out_specs=pl.BlockSpec((1, t, 512), lambda bb, i: (bb, i, 0)),
        out_shape=jax.ShapeDtypeStruct((b, s, 512), F32),
        compiler_params=_cparams(("parallel", "parallel")),
        name="rope_tables",
    )(posb, _inv_freq_row(MLA_ROPE), _inv_freq_row(HEAD_DIM))


def _mla_prep_kernel(cq_ref, ckv_ref, misc_ref, rope_ref, qng_ref, wuq_ref, kvng_ref, wukv_ref,
                     qgn_ref, qgp_ref, kgn_ref, kgp_ref,
                     q_ref, k_ref, v_ref, *, T):
    scale = (MLA_NOPE + MLA_ROPE) ** -0.5 * LOG2E
    lane = lax.broadcasted_iota(jnp.int32, (T, 128), 1)
    cos = rope_ref[0, :, 0:128]
    sin = rope_ref[0, :, 128:256]

    cq = _rms(cq_ref[0], MLA_Q_RANK) * qng_ref[...]
    qf = _dot(cq.astype(BF16), wuq_ref[...])
    ckv = _rms(ckv_ref[0], MLA_KV_RANK) * kvng_ref[...]
    kvf = _dot(ckv.astype(BF16), wukv_ref[...])
    kpe = jnp.where(lane < MLA_ROPE, misc_ref[0], 0.0)
    kpe = _rope64(_rms(kpe, MLA_ROPE) * kgp_ref[...], cos, sin, lane)
    for h in range(N_HEADS):
        o = h * 256
        qn = _rms(qf[:, o:o + 128], MLA_NOPE) * qgn_ref[...]
        qp = _rope64(_rms(qf[:, o + 128:o + 256], MLA_ROPE) * qgp_ref[...], cos, sin, lane)
        q_ref[0, :, o:o + 128] = (qn * scale).astype(BF16)
        q_ref[0, :, o + 128:o + 256] = (qp * scale).astype(BF16)
        kn = _rms(kvf[:, o:o + 128], MLA_NOPE) * kgn_ref[...]
        k_ref[0, :, o:o + 128] = kn.astype(BF16)
        k_ref[0, :, o + 128:o + 256] = kpe.astype(BF16)
        v_ref[0, :, h * 128:(h + 1) * 128] = kvf[:, o + 128:o + 256].astype(BF16)


def _mla_prep(proj, rope, q_norm_g, w_uq, kv_norm_g, w_ukv, qk_norm_g):
    b, s, _ = proj.shape
    t = min(512, s)
    wuq = jnp.pad(w_uq, ((0, 0), (0, 0), (0, 64))).reshape(MLA_Q_RANK, N_HEADS * 256).astype(BF16)
    wukv = w_ukv.reshape(MLA_KV_RANK, N_HEADS * 256).astype(BF16)
    pad64 = lambda v: jnp.pad(v, (0, 64)).reshape(1, 128)
    full = lambda shape: pl.BlockSpec(shape, lambda bb, i: (0,) * len(shape))
    tok = lambda w: pl.BlockSpec((1, t, w), lambda bb, i: (bb, i, 0))
    return pl.pallas_call(
        functools.partial(_mla_prep_kernel, T=t),
        grid=(b, s // t),
        in_specs=[_seg_spec(t, MLA_Q_RANK, OFF_BCQ), _seg_spec(t, MLA_KV_RANK, OFF_BCKV),
                  _seg_spec(t, 128, OFF_MISC), pl.BlockSpec((1, t, 256), lambda bb, i: (bb, i, 0)),
                  full((1, MLA_Q_RANK)), full((MLA_Q_RANK, 1024)),
                  full((1, MLA_KV_RANK)), full((MLA_KV_RANK, 1024)),
                  full((1, 128)), full((1, 128)), full((1, 128)), full((1, 128))],
        out_specs=[tok(1024), tok(1024), tok(512)],
        out_shape=[jax.ShapeDtypeStruct((b, s, 1024), BF16),
                   jax.ShapeDtypeStruct((b, s, 1024), BF16),
                   jax.ShapeDtypeStruct((b, s, 512), BF16)],
        compiler_params=_cparams(("parallel", "parallel")),
        name="mla_prep",
    )(proj, proj, proj, rope, q_norm_g.reshape(1, -1), wuq, kv_norm_g.reshape(1, -1), wukv,
      qk_norm_g[0, :MLA_NOPE].reshape(1, 128), pad64(qk_norm_g[0, MLA_NOPE:]),
      qk_norm_g[1, :MLA_NOPE].reshape(1, 128), pad64(qk_norm_g[1, MLA_NOPE:]))


def _softmax_init(m_ref, l_ref, acc_ref):
    m_ref[...] = jnp.full_like(m_ref, NEG)
    l_ref[...] = jnp.zeros_like(l_ref)
    acc_ref[...] = jnp.zeros_like(acc_ref)


ROW_GROUPS = 2


def _attend_causal(q, k_ref, v_ref, n_past, kc, m_ref, l_ref, acc_ref, sa_ref, sb_ref, mask_fn):
    g_rows = q.shape[0] // ROW_GROUPS

    def scores(c, s_ref):
        k_blk = k_ref[0, pl.ds(pl.multiple_of(c * kc, kc), kc), :]
        for g in range(ROW_GROUPS):
            rows = slice(g * g_rows, (g + 1) * g_rows)
            s_ref[rows, :] = _dot_nt(q[rows], k_blk)

    def consume(c, s_ref, masked):
        k0 = pl.multiple_of(c * kc, kc)
        v_blk = v_ref[0, pl.ds(k0, kc), :]
        for g in range(ROW_GROUPS):
            rows = slice(g * g_rows, (g + 1) * g_rows)
            s = s_ref[rows, :]
            if masked:
                s = mask_fn(s, g * g_rows, k0)
            m_old = m_ref[rows, :]
            m_new = jnp.maximum(m_old, jnp.max(s, axis=-1, keepdims=True))
            alpha = jnp.exp2(m_old - m_new)
            p = jnp.exp2(s - m_new)
            part = p[:, 0:128]
            for j in range(1, kc // 128):
                part = part + p[:, j * 128:(j + 1) * 128]
            l_ref[rows, :] = alpha * l_ref[rows, :] + part
            acc_ref[rows, :] = alpha * acc_ref[rows, :] + _dot(p.astype(BF16), v_blk)
            m_ref[rows, :] = m_new

    def past_pair(j, carry):
        c = 2 * j
        scores(c + 1, sb_ref)
        consume(c, sa_ref, False)
        scores(c + 2, sa_ref)
        consume(c + 1, sb_ref, False)
        return carry

    scores(0, sa_ref)
    lax.fori_loop(0, n_past // 2, past_pair, 0)

    @pl.when(n_past % 2 == 0)
    def _():
        consume(n_past, sa_ref, True)

    @pl.when(n_past % 2 == 1)
    def _():
        scores(n_past, sb_ref)
        consume(n_past - 1, sa_ref, False)
        consume(n_past, sb_ref, True)


def _softmax_result(l_ref, acc_ref):
    return acc_ref[...] / jnp.sum(l_ref[...], axis=-1, keepdims=True)


def _mla_flash_kernel(q_ref, k_ref, v_ref, gate_ref, o_ref, m_ref, l_ref, acc_ref, sa_ref, sb_ref,
                      *, tq, kc):
    i = pl.program_id(2)
    _softmax_init(m_ref, l_ref, acc_ref)
    q = q_ref[0]

    def causal(s, first_row, k0):
        row = lax.broadcasted_iota(jnp.int32, s.shape, 0) + (i * tq + first_row)
        col = lax.broadcasted_iota(jnp.int32, s.shape, 1) + k0
        return jnp.where(col <= row, s, NEG)

    n_full = (i * tq) // kc
    _attend_causal(q, k_ref, v_ref, n_full, kc, m_ref, l_ref, acc_ref, sa_ref, sb_ref, causal)
    o_ref[0] = (_softmax_result(l_ref, acc_ref) * _silu(gate_ref[0])).astype(o_ref.dtype)


def _mla_flash(q, k, v, proj):
    b, s, _ = q.shape
    tq = min(512, s)
    kc = min(1024, s)
    gate_blk = OFF_BG // 128
    return pl.pallas_call(
        functools.partial(_mla_flash_kernel, tq=tq, kc=kc),
        grid=(b, N_HEADS, s // tq),
        in_specs=[pl.BlockSpec((1, tq, 256), lambda bb, h, i: (bb, i, h)),
                  pl.BlockSpec((1, s, 256), lambda bb, h, i: (bb, 0, h)),
                  pl.BlockSpec((1, s, 128), lambda bb, h, i: (bb, 0, h)),
                  pl.BlockSpec((1, tq, 128), lambda bb, h, i: (bb, i, gate_blk + h))],
        out_specs=pl.BlockSpec((1, tq, 128), lambda bb, h, i: (bb, i, h)),
        out_shape=jax.ShapeDtypeStruct((b, s, GROUP_W), BF16),
        scratch_shapes=[pltpu.VMEM((tq, 1), F32), pltpu.VMEM((tq, 128), F32),
                        pltpu.VMEM((tq, 128), F32),
                        pltpu.VMEM((tq, kc), F32), pltpu.VMEM((tq, kc), F32)],
        compiler_params=_cparams(("parallel", "parallel", "arbitrary")),
        name="mla_flash",
    )(q, k, v, proj)


def _nsa_prep_kernel(dq_ref, dkv_ref, rope_ref, qg_ref, kg_ref,
                     q_ref, ks_ref, vs_ref, kw_ref, vw_ref, kc_ref, vc_ref, *, T):
    scale = HEAD_DIM ** -0.5 * LOG2E
    lane = lax.broadcasted_iota(jnp.int32, (T, 128), 1)
    cos = rope_ref[0, :, 0:128]
    sin = rope_ref[0, :, 128:256]
    dq = dq_ref[0]
    for h in range(N_HEADS):
        qh = _rms(dq[:, h * 128:(h + 1) * 128], HEAD_DIM) * qg_ref[...]
        q_ref[0, :, h * 128:(h + 1) * 128] = _rope128(qh, cos, sin) * scale
    kv = dkv_ref[0]
    kc_ref[0] = kv[:, 0:128]
    vc_ref[0] = kv[:, 128:256]
    ks_ref[0, :, 0:128] = _rope128(_rms(kv[:, 256:384], HEAD_DIM) * kg_ref[1:2], cos, sin).astype(BF16)
    key_blk = (pl.program_id(1) * T + lax.broadcasted_iota(jnp.int32, (T, 128), 0)) >> 6
    ks_ref[0, :, 128:256] = (lane == key_blk).astype(BF16)
    vs_ref[0] = kv[:, 384:512].astype(BF16)
    kw_ref[0] = _rope128(_rms(kv[:, 512:640], HEAD_DIM) * kg_ref[2:3], cos, sin).astype(BF16)
    vw_ref[0] = kv[:, 640:768].astype(BF16)


def _nsa_prep(proj, rope, q_norm_g, k_norm_g):
    b, s, _ = proj.shape
    t = min(512, s)
    full = lambda shape: pl.BlockSpec(shape, lambda bb, i: (0,) * len(shape))
    tok = lambda w: pl.BlockSpec((1, t, w), lambda bb, i: (bb, i, 0))
    sds = lambda w, dt: jax.ShapeDtypeStruct((b, s, w), dt)
    return pl.pallas_call(
        functools.partial(_nsa_prep_kernel, T=t),
        grid=(b, s // t),
        in_specs=[_seg_spec(t, GROUP_W, OFF_DQ), _seg_spec(t, 768, OFF_DKV),
                  pl.BlockSpec((1, t, 256), lambda bb, i: (bb, i, 1)),
                  full((1, 128)), full((3, 128))],
        out_specs=[tok(512), tok(256)] + [tok(128)] * 5,
        out_shape=[sds(512, F32), sds(256, BF16), sds(128, BF16), sds(128, BF16), sds(128, BF16),
                   sds(128, F32), sds(128, F32)],
        compiler_params=_cparams(("parallel", "parallel")),
        name="nsa_prep",
    )(proj, proj, rope, q_norm_g.reshape(1, HEAD_DIM), k_norm_g)


def _nsa_cmp_kernel(kt_ref, vt_ref, pe_ref, w1_ref, b1_ref, w2_ref, b2_ref, kg_ref, pos_ref,
                    invf_ref, kc_ref, vc_ref, *, NC):
    half = CMP_STRIDE * HEAD_DIM
    outs = []
    for j, t_ref in enumerate((kt_ref, vt_ref)):
        t2 = t_ref[0].astype(BF16)
        first = _dot(t2, w1_ref[j, :half, :])
        second = pltpu.roll(_dot(t2, w1_ref[j, half:, :]), NC - 1, 0)
        pe8 = jnp.broadcast_to(pe_ref[j], (8, CMP_LEN * HEAD_DIM)).astype(BF16)
        bias = _dot(pe8, w1_ref[j])[0:1] + b1_ref[j]
        hid = _silu(first + second + bias)
        outs.append(_dot(hid.astype(BF16), w2_ref[j]) + b2_ref[j])
    k_c, v_c = outs
    lane = lax.broadcasted_iota(jnp.int32, (NC, 128), 1)
    ang = pos_ref[0] * invf_ref[...]
    sin = jnp.sin(ang)
    sin = jnp.where(lane < 64, -sin, sin)
    kc_ref[0] = _rope128(_rms(k_c, HEAD_DIM) * kg_ref[...], jnp.cos(ang), sin)
    vc_ref[0] = v_c


def _nsa_compress(kc_raw, vc_raw, posc, cmp_pe, cmp_w1, cmp_b1, cmp_w2, cmp_b2, kg0):
    b, s, _ = kc_raw.shape
    nc = s // CMP_STRIDE
    kt = kc_raw.reshape(b, nc, CMP_STRIDE * HEAD_DIM)
    vt = vc_raw.reshape(b, nc, CMP_STRIDE * HEAD_DIM)
    full = lambda shape: pl.BlockSpec(shape, lambda bb: (0,) * len(shape))
    per_b = lambda shape: pl.BlockSpec((1,) + shape, lambda bb: (bb, 0, 0))
    return pl.pallas_call(
        functools.partial(_nsa_cmp_kernel, NC=nc),
        grid=(b,),
        in_specs=[per_b((nc, CMP_STRIDE * HEAD_DIM)), per_b((nc, CMP_STRIDE * HEAD_DIM)),
                  full((2, 1, CMP_LEN * HEAD_DIM)), full((2, CMP_LEN * HEAD_DIM, CMP_HIDDEN)),
                  full((2, 1, CMP_HIDDEN)), full((2, CMP_HIDDEN, HEAD_DIM)),
                  full((2, 1, HEAD_DIM)), full((1, HEAD_DIM)), per_b((nc, 128)), full((1, 128))],
        out_specs=[per_b((nc, HEAD_DIM)), per_b((nc, HEAD_DIM))],
        out_shape=[jax.ShapeDtypeStruct((b, nc, HEAD_DIM), F32),
                   jax.ShapeDtypeStruct((b, nc, HEAD_DIM), F32)],
        compiler_params=_cparams(("parallel",)),
        name="nsa_compress",
    )(kt, vt, cmp_pe.reshape(2, 1, CMP_LEN * HEAD_DIM), cmp_w1.astype(BF16),
      cmp_b1.reshape(2, 1, CMP_HIDDEN), cmp_w2.astype(BF16), cmp_b2.reshape(2, 1, HEAD_DIM),
      kg0.reshape(1, HEAD_DIM), posc, _inv_freq_row(HEAD_DIM))


def _nsa_attn_kernel(q_ref, kc_ref, vc_ref, ks_ref, vs_ref, kw_ref, vw_ref, misc_ref, gate_ref,
                     o_ref, m_ref, l_ref, acc_ref, sa_ref, sb_ref, *, S, NC, KC, WK):
    Q = Q_BLOCK
    R = N_HEADS
    i = pl.program_id(1)
    n_sel = S // SEL_LEN
    n_cmp = (S - CMP_LEN) // CMP_STRIDE + 1
    top_k = min(SEL_TOPK, n_sel)

    qf = q_ref[0]
    q4 = jnp.concatenate([qf[:, r * 128:(r + 1) * 128] for r in range(R)], axis=0)
    q4b = q4.astype(BF16)

    w0 = pl.multiple_of(jnp.maximum(i * Q - WINDOW, 0), Q)
    hi_w = (i * Q - w0) + lax.broadcasted_iota(jnp.int32, (Q, WK), 0)
    col_w = lax.broadcasted_iota(jnp.int32, (Q, WK), 1)
    band = jnp.where((col_w <= hi_w) & (col_w > hi_w - WINDOW), 0.0, NEG)
    s_w = _dot_nt(q4b, kw_ref[0, pl.ds(w0, WK), :]) + jnp.concatenate([band] * R, axis=0)
    p_w = jnp.exp2(s_w - jnp.max(s_w, axis=-1, keepdims=True))
    o_w = (_dot(p_w.astype(BF16), vw_ref[0, pl.ds(w0, WK), :])
           / jnp.sum(p_w, axis=-1, keepdims=True))

    q_hi, q_lo = _split_bf16(q4)
    k_hi, k_lo = _split_bf16(kc_ref[0])
    s_c = (_dot_nt(jnp.concatenate([q_hi, q_lo], axis=1), jnp.concatenate([k_hi, k_hi], axis=1))
           + _dot_nt(q_hi, k_lo))
    t_c = i * Q + (lax.broadcasted_iota(jnp.int32, (R * Q, NC), 0) & (Q - 1))
    c_ix = lax.broadcasted_iota(jnp.int32, (R * Q, NC), 1)
    valid_c = (c_ix * CMP_STRIDE + (CMP_LEN - 1) <= t_c) & (c_ix < n_cmp)
    s_c = jnp.where(valid_c, s_c, NEG)
    p_c = jnp.where(valid_c, jnp.exp2(s_c - jnp.max(s_c, axis=-1, keepdims=True)), 0.0)
    p_c = p_c / jnp.maximum(jnp.sum(p_c, axis=-1, keepdims=True), 1e-30)
    o_c = _bdot(p_c, vc_ref[0])
    assert NC & (NC - 1) == 0
    c_o = (lax.broadcasted_iota(jnp.int32, (2 * NC, 128), 0) & (NC - 1)) * CMP_STRIDE
    n_o = lax.broadcasted_iota(jnp.int32, (2 * NC, 128), 1) * SEL_LEN
    overlap2 = ((c_o < n_o + SEL_LEN) & (c_o + (CMP_LEN - 1) >= n_o)).astype(BF16)
    imp4 = _dot(jnp.concatenate(_split_bf16(p_c), axis=1), overlap2)
    imp = (imp4[0:Q] + imp4[Q:2 * Q] + imp4[2 * Q:3 * Q] + imp4[3 * Q:4 * Q]).T

    NP = min(128, -(-n_sel // 8) * 8)
    t_q = i * Q + lax.broadcasted_iota(jnp.int32, (NP, Q), 1)
    n_ix = lax.broadcasted_iota(jnp.int32, (NP, Q), 0)
    cur = t_q >> 6
    valid_s = (n_ix * SEL_LEN <= t_q) & (n_ix < n_sel)
    forced = (n_ix == 0) | (n_ix == cur) | (n_ix == cur - 1)
    val = jnp.where(valid_s, imp[:NP], -1.0)
    val = jnp.where(forced & valid_s, FORCE, val)
    val = jnp.where(n_ix < n_sel, val, -2.0)
    sel_t = jnp.zeros((NP, Q), F32)
    n_f = n_ix.astype(F32)
    for _ in range(top_k):
        mx = jnp.max(val, axis=0, keepdims=True)
        first = jnp.min(jnp.where(val == mx, n_f, 1e9), axis=0, keepdims=True)
        hit = n_f == first
        sel_t = jnp.where(hit, 1.0, sel_t)
        val = jnp.where(hit, -3.0, val)
    sel_t = jnp.where(valid_s, sel_t, 0.0)
    if NP < 128:
        sel_t = jnp.concatenate([sel_t, jnp.zeros((128 - NP, Q), F32)], axis=0)
    blk_bias = jnp.where(sel_t.T > 0.5, 0.0, NEG).astype(BF16)
    q_aug = jnp.concatenate([q4b, jnp.concatenate([blk_bias] * R, axis=0)], axis=1)

    _softmax_init(m_ref, l_ref, acc_ref)

    def causal_mask(s, first_row, k0):
        t_k = i * Q + (lax.broadcasted_iota(jnp.int32, s.shape, 0) & (Q - 1))
        return jnp.where(k0 + lax.broadcasted_iota(jnp.int32, s.shape, 1) <= t_k, s, NEG)

    n_past = (i * Q) // KC
    _attend_causal(q_aug, ks_ref, vs_ref, n_past, KC, m_ref, l_ref, acc_ref, sa_ref, sb_ref,
                   causal_mask)
    o_s = _softmax_result(l_ref, acc_ref)

    gates = jax.nn.sigmoid(misc_ref[0])
    gate = gate_ref[0]
    for r in range(R):
        g0 = gates[:, MISC_GL + 3 * r:MISC_GL + 3 * r + 1]
        g1 = gates[:, MISC_GL + 3 * r + 1:MISC_GL + 3 * r + 2]
        g2 = gates[:, MISC_GL + 3 * r + 2:MISC_GL + 3 * r + 3]
        rows = slice(r * Q, (r + 1) * Q)
        o = g0 * o_c[rows] + g1 * o_s[rows] + g2 * o_w[rows]
        o_ref[0, :, r * 128:(r + 1) * 128] = (
            o * _silu(gate[:, r * 128:(r + 1) * 128])).astype(o_ref.dtype)


def _nsa_attn(qf, k_c, v_c, ks, vs, kw, vw, proj):
    b, s, _ = qf.shape
    nc = s // CMP_STRIDE
    kc_len = min(1024, s)
    wk = min(WINDOW + Q_BLOCK, s)
    per_b = lambda n, w: pl.BlockSpec((1, n, w), lambda bb, i: (bb, 0, 0))
    return pl.pallas_call(
        functools.partial(_nsa_attn_kernel, S=s, NC=nc, KC=kc_len, WK=wk),
        grid=(b, s // Q_BLOCK),
        in_specs=[pl.BlockSpec((1, Q_BLOCK, GROUP_W), lambda bb, i: (bb, i, 0)),
                  per_b(nc, 128), per_b(nc, 128),
                  per_b(s, 256), per_b(s, 128), per_b(s, 128), per_b(s, 128),
                  _seg_spec(Q_BLOCK, 128, OFF_MISC), _seg_spec(Q_BLOCK, GROUP_W, OFF_DG)],
        out_specs=pl.BlockSpec((1, Q_BLOCK, GROUP_W), lambda bb, i: (bb, i, 0)),
        out_shape=jax.ShapeDtypeStruct((b, s, GROUP_W), BF16),
        scratch_shapes=[pltpu.VMEM((N_HEADS * Q_BLOCK, 1), F32),
                        pltpu.VMEM((N_HEADS * Q_BLOCK, 128), F32),
                        pltpu.VMEM((N_HEADS * Q_BLOCK, 128), F32),
                        pltpu.VMEM((N_HEADS * Q_BLOCK, kc_len), F32),
                        pltpu.VMEM((N_HEADS * Q_BLOCK, kc_len), F32)],
        compiler_params=_cparams(("parallel", "arbitrary")),
        name="nsa_attn",
    )(qf, k_c, v_c, ks, vs, kw, vw, proj, proj)


def _reorder_w_in_kernel(w_ref, o_ref):
    o = np.concatenate([[0], np.cumsum(IN_SIZES)]).tolist()
    (a_qkv, a_decay, a_beta, a_gate, b_cq, b_ckv, b_kpe, b_gate, c_x, c_gate,
     d_q, d_kv, d_gl, d_gate) = range(len(IN_SIZES))
    for l in range(w_ref.shape[1]):
        seg = lambda k: w_ref[o[k]:o[k + 1], l, :]
        off = 0
        for k in (a_qkv, a_gate, b_gate, c_x, c_gate, d_q, d_gate, d_kv, b_cq):
            o_ref[l, off:off + IN_SIZES[k], :] = seg(k).astype(BF16)
            off += IN_SIZES[k]
        assert off == OFF_MISC
        zeros = jnp.zeros((128 - MLA_ROPE - 2 * N_HEADS - 3 * N_HEADS, w_ref.shape[2]), F32)
        misc = jnp.concatenate([seg(b_kpe), seg(a_decay), seg(a_beta), seg(d_gl), zeros], axis=0)
        o_ref[l, OFF_MISC:OFF_MISC + 128, :] = misc.astype(BF16)
        o_ref[l, OFF_BCKV:OFF_BCKV + MLA_KV_RANK, :] = seg(b_ckv).astype(BF16)


def _reorder_w_in(w_in):
    depth, d, d_in = w_in.shape
    tc = 128
    w_t = jnp.transpose(w_in, (2, 0, 1))
    return pl.pallas_call(
        _reorder_w_in_kernel,
        grid=(d // tc,),
        in_specs=[pl.BlockSpec((d_in, depth, tc), lambda i: (0, 0, i))],
        out_specs=pl.BlockSpec((depth, D_PROJ, tc), lambda i: (0, 0, i)),
        out_shape=jax.ShapeDtypeStruct((depth, D_PROJ, d), BF16),
        compiler_params=_cparams(("parallel",)),
        name="reorder_w_in",
    )(w_t)


def kernel(x, positions, norm_g, w_in, w_out, gdn_conv_w, gdn_a_log, gdn_dt_bias, gdn_norm_g,
           mla_q_norm_g, mla_w_uq, mla_kv_norm_g, mla_w_ukv, mla_qk_norm_g,
           lru_conv_w, lru_conv_b, lru_w_r, lru_b_r, lru_w_i, lru_b_i, lru_lambda,
           nsa_q_norm_g, nsa_k_norm_g, nsa_cmp_pe, nsa_cmp_w1, nsa_cmp_b1, nsa_cmp_w2, nsa_cmp_b2):
    b, s, d = x.shape
    depth = w_in.shape[0]
    w_in_r = _reorder_w_in(w_in)
    w_out_b = w_out.astype(BF16)
    posf = positions.astype(F32)
    rope = _rope_tables(jnp.broadcast_to(posf[:, :, None], (b, s, 128)))
    nc = s // CMP_STRIDE
    pos_end = jnp.pad(posf.reshape(b, nc, CMP_STRIDE)[:, 1:, CMP_STRIDE - 1], ((0, 0), (0, 1)))
    posc = jnp.broadcast_to(pos_end[:, :, None], (b, nc, 128))

    x2d = x.reshape(b * s, d)
    for l in range(depth):
        proj = _inproj(x2d, norm_g[l].reshape(1, d), w_in_r, l).reshape(b, s, D_PROJ)
        y_a = _gdn(proj, gdn_conv_w[l], gdn_a_log[l], gdn_dt_bias[l], gdn_norm_g[l])
        q_b, k_b, v_b = _mla_prep(proj, rope, mla_q_norm_g[l], mla_w_uq[l], mla_kv_norm_g[l],
                                  mla_w_ukv[l], mla_qk_norm_g[l])
        y_b = _mla_flash(q_b, k_b, v_b, proj)
        y_c = _lru(proj, lru_conv_w[l], lru_conv_b[l], lru_w_r[l], lru_b_r[l], lru_w_i[l],
                   lru_b_i[l], lru_lambda[l])
        q_d, ks, vs, kw, vw, kc_raw, vc_raw = _nsa_prep(proj, rope, nsa_q_norm_g[l],
                                                        nsa_k_norm_g[l])
        k_c, v_c = _nsa_compress(kc_raw, vc_raw, posc, nsa_cmp_pe[l], nsa_cmp_w1[l],
                                 nsa_cmp_b1[l], nsa_cmp_w2[l], nsa_cmp_b2[l], nsa_k_norm_g[l, 0])
        y_d = _nsa_attn(q_d, k_c, v_c, ks, vs, kw, vw, proj)
        ys = [y.reshape(b * s, GROUP_W) for y in (y_a, y_b, y_c, y_d)]
        x2d = _outproj(ys, w_out_b[l], x2d)
    return x2d.reshape(b, s, d)
```

```python
import functools
import math

import numpy as np
import jax
import jax.numpy as jnp
from jax import lax
from jax.experimental import pallas as pl
from jax.experimental.pallas import tpu as pltpu

F32 = jnp.float32
BF16 = jnp.bfloat16
HIGHEST = lax.Precision.HIGHEST

D_MODEL = 2048
GROUP_W = 512
HEAD_DIM = 128
N_HEADS = 4
RMS_EPS = 1e-6
ROPE_THETA = 10000.0
NEG = -1e30
LOG2E = math.log2(math.e)
GDN_CONV = 4
GDN_CHUNK = 128
MLA_Q_RANK = 384
MLA_KV_RANK = 256
MLA_NOPE = 128
MLA_ROPE = 64
LRU_CONV = 4
LRU_C = 8.0
CMP_LEN = 32
CMP_STRIDE = 16
CMP_HIDDEN = 256
SEL_LEN = 64
SEL_TOPK = 16
WINDOW = 512
FORCE = 1e9
Q_BLOCK = 256

IN_SIZES = (3 * GROUP_W, N_HEADS, N_HEADS, GROUP_W,
            MLA_Q_RANK, MLA_KV_RANK, MLA_ROPE, GROUP_W,
            GROUP_W, GROUP_W,
            GROUP_W, 6 * HEAD_DIM, 3 * N_HEADS, GROUP_W)

OFF_AQ, OFF_AK, OFF_AV, OFF_AG = 0, 512, 1024, 1536
OFF_BG, OFF_CX, OFF_CG, OFF_DQ, OFF_DG = 2048, 2560, 3072, 3584, 4096
OFF_DKV, OFF_BCQ, OFF_MISC, OFF_BCKV = 4608, 5376, 5760, 5888
D_PROJ = 6144
MISC_KPE, MISC_DECAY, MISC_BETA, MISC_GL = 0, 64, 68, 72

VMEM_LIMIT = 56 * 1024 * 1024


def _cparams(sem):
    return pltpu.CompilerParams(dimension_semantics=sem, vmem_limit_bytes=VMEM_LIMIT)


def _dot(a, b, precision=None):
    return lax.dot_general(a, b, (((1,), (0,)), ((), ())), precision=precision,
                           preferred_element_type=F32)


def _dot_nt(a, b, precision=None):
    return lax.dot_general(a, b, (((1,), (1,)), ((), ())), precision=precision,
                           preferred_element_type=F32)


def _bdot(a, b):
    return _dot(a.astype(BF16), b.astype(BF16))


def _bdot_nt(a, b):
    return _dot_nt(a.astype(BF16), b.astype(BF16))


def _split_bf16(x):
    hi = x.astype(BF16)
    return hi, (x - hi.astype(F32)).astype(BF16)


def _silu(x):
    return x * jax.nn.sigmoid(x)


def _softplus(x):
    return jnp.maximum(x, 0.0) + jnp.log1p(jnp.exp(-jnp.abs(x)))


def _rms(x, n):
    return x * lax.rsqrt(jnp.sum(x * x, axis=-1, keepdims=True) * (1.0 / n) + RMS_EPS)


def _inproj_kernel(x_ref, g_ref, w_ref, o_ref, h_ref, *, tm):
    @pl.when(pl.program_id(1) == 0)
    def _():
        rows = min(256, tm)
        for r in range(tm // rows):
            x = x_ref[r * rows:(r + 1) * rows, :]
            h_ref[r * rows:(r + 1) * rows, :] = (_rms(x, D_MODEL) * g_ref[...]).astype(BF16)

    o_ref[...] = _dot_nt(h_ref[...], w_ref[...])


def _inproj(x2d, g, w_t, layer):
    m = x2d.shape[0]
    tm = min(1024, m)
    tn = 1024
    return pl.pallas_call(
        functools.partial(_inproj_kernel, tm=tm),
        grid=(m // tm, D_PROJ // tn),
        in_specs=[pl.BlockSpec((tm, D_MODEL), lambda i, j: (i, 0)),
                  pl.BlockSpec((1, D_MODEL), lambda i, j: (0, 0)),
                  pl.BlockSpec((None, tn, D_MODEL), lambda i, j: (layer, j, 0))],
        out_specs=pl.BlockSpec((tm, tn), lambda i, j: (i, j)),
        out_shape=jax.ShapeDtypeStruct((m, D_PROJ), F32),
        scratch_shapes=[pltpu.VMEM((tm, D_MODEL), BF16)],
        compiler_params=_cparams(("parallel", "arbitrary")),
        name="inproj",
    )(x2d, g, w_t)


def _outproj_kernel(ya_ref, yb_ref, yc_ref, yd_ref, wa_ref, wb_ref, wc_ref, wd_ref, x_ref, o_ref):
    acc = x_ref[...]
    for y_ref, w_ref in ((ya_ref, wa_ref), (yb_ref, wb_ref), (yc_ref, wc_ref), (yd_ref, wd_ref)):
        acc = acc + _dot(y_ref[...].astype(BF16), w_ref[...])
    o_ref[...] = acc


def _outproj(ys, w, x2d):
    m = x2d.shape[0]
    tm = min(1024, m)
    tn = 1024
    y_specs = [pl.BlockSpec((tm, GROUP_W), lambda i, j: (i, 0)) for _ in range(4)]
    w_specs = [pl.BlockSpec((GROUP_W, tn), functools.partial(lambda i, j, g: (g, j), g=g))
               for g in range(4)]
    return pl.pallas_call(
        _outproj_kernel,
        grid=(m // tm, D_MODEL // tn),
        in_specs=y_specs + w_specs + [pl.BlockSpec((tm, tn), lambda i, j: (i, j))],
        out_specs=pl.BlockSpec((tm, tn), lambda i, j: (i, j)),
        out_shape=jax.ShapeDtypeStruct((m, D_MODEL), F32),
        compiler_params=_cparams(("parallel", "arbitrary")),
        name="outproj",
    )(*ys, w, w, w, w, x2d)


def _causal_conv4(xin, ext_ref, w):
    t = xin.shape[0]
    ext_ref[8:8 + t, :] = xin
    acc = ext_ref[5:5 + t, :] * w[0:1]
    for j in range(1, 4):
        acc = acc + ext_ref[5 + j:5 + j + t, :] * w[j:j + 1]
    ext_ref[0:8, :] = xin[t - 8:t]
    return acc


def _gdn_kernel(q_ref, k_ref, v_ref, gate_ref, misc_ref, cw_ref, alog_ref, dtb_ref, ng_ref,
                o_ref, hist_ref, state_ref, *, T):
    C = GDN_CHUNK

    @pl.when(pl.program_id(1) == 0)
    def _():
        hist_ref[:, 0:8, :] = jnp.zeros((3, 8, GROUP_W), F32)
        state_ref[...] = jnp.zeros_like(state_ref)

    cw = cw_ref[...]
    conv = []
    for idx, ref in enumerate((q_ref, k_ref, v_ref)):
        xin = ref[0]
        acc = _causal_conv4(xin, hist_ref.at[idx], cw[:, idx * GROUP_W:(idx + 1) * GROUP_W])
        conv.append(_silu(acc))
    qc, kc, vc = conv

    misc = misc_ref[0]
    gfull = -jnp.exp(alog_ref[...]) * _softplus(misc + dtb_ref[...])
    betafull = jax.nn.sigmoid(misc)
    gate = gate_ref[0]

    row = lax.broadcasted_iota(jnp.int32, (C, C), 0)
    col = lax.broadcasted_iota(jnp.int32, (C, C), 1)
    lower = row >= col
    strict = row > col
    ltri = lower.astype(F32)
    row2 = lax.broadcasted_iota(jnp.int32, (C, 2 * C), 0)
    lane2 = lax.broadcasted_iota(jnp.int32, (C, 2 * C), 1)
    left = lane2 < C
    col2 = lane2 & (C - 1)
    leaf_bits = 4
    same_blk = {bits: (row2 >> bits) == (col2 >> bits) for bits in range(leaf_bits, 8)}
    assert C == HEAD_DIM == 1 << 7

    def pairdot(x, y):
        yb = y.astype(BF16)
        zero = jnp.zeros_like(yb)
        bd = jnp.concatenate([jnp.where(left, yb, zero), jnp.where(left, zero, yb)], axis=0)
        return _dot(x.astype(BF16), bd)

    n_chunks = T // C
    hds = []
    for c in range(n_chunks):
        r0 = c * C
        gall = _dot(ltri, gfull[r0:r0 + C], HIGHEST)
        hd = []
        hds.append(hd)
        for h in range(N_HEADS):
            l0 = h * HEAD_DIM
            q = qc[r0:r0 + C, l0:l0 + HEAD_DIM]
            k = kc[r0:r0 + C, l0:l0 + HEAD_DIM]
            v = vc[r0:r0 + C, l0:l0 + HEAD_DIM]
            q = q * lax.rsqrt(jnp.sum(q * q, axis=-1, keepdims=True) + RMS_EPS) * (HEAD_DIM ** -0.5)
            k = k * lax.rsqrt(jnp.sum(k * k, axis=-1, keepdims=True) + RMS_EPS)
            beta = betafull[r0:r0 + C, MISC_BETA + h:MISC_BETA + h + 1]
            gc = jnp.broadcast_to(gall[:, MISC_DECAY + h:MISC_DECAY + h + 1], (C, C))
            decay = jnp.exp(jnp.where(lower, gc - gc.T, NEG))
            kb = k * beta
            a = jnp.where(strict, _bdot_nt(kb, k) * decay, 0.0)
            hd.append((q, k, v, beta, gc, decay, kb, a))

    a2s = [jnp.concatenate([hds[c][h0][7], hds[c][h0 + 1][7]], axis=1)
           for c in range(n_chunks) for h0 in range(0, N_HEADS, 2)]
    ps = [jnp.where(same_blk[leaf_bits], -a2, 0.0) for a2 in a2s]
    tm2 = list(ps)
    for _ in range(leaf_bits - 1):
        ps = [pairdot(p, p) for p in ps]
        tm2 = [tm + p + pairdot(tm, p) for tm, p in zip(tm2, ps)]
    for bits in range(leaf_bits + 1, 8):
        join = same_blk[bits] & jnp.logical_not(same_blk[bits - 1])
        tas = [jnp.where(join, a2, 0.0) for a2 in a2s]
        tas = [off + pairdot(tm, off) for tm, off in zip(tm2, tas)]
        tm2 = [tm - (ta + pairdot(ta, tm)) for tm, ta in zip(tm2, tas)]

    for c in range(n_chunks):
        r0 = c * C
        hd = hds[c]
        tms = []
        for tm in tm2[2 * c:2 * c + 2]:
            tms += [tm[:, :C], tm[:, C:]]
        for h in range(N_HEADS):
            l0 = h * HEAD_DIM
            q, k, v, beta, gc, decay, kb, _ = hd[h]
            eg = jnp.exp(gc)
            rhs = jnp.concatenate([v * beta, kb * eg], axis=-1)
            uw = rhs + _bdot(tms[h], rhs)
            u = uw[:, :HEAD_DIM]
            w = uw[:, HEAD_DIM:]
            qk = _bdot_nt(q, k) * decay
            q_dec = q * eg
            g_last = gc[C - 1:C, :]
            k_end = k * jnp.exp(g_last - gc)
            c_dec = jnp.exp(g_last)
            state = state_ref[h]
            v_new = u - _bdot(w, state)
            o = _bdot(q_dec, state) + _bdot(qk, v_new)
            state_ref[h] = state * c_dec + lax.dot_general(
                k_end.astype(BF16), v_new.astype(BF16), (((0,), (0,)), ((), ())),
                preferred_element_type=F32)
            o = _rms(o, HEAD_DIM) * ng_ref[...]
            o_ref[0, r0:r0 + C, l0:l0 + HEAD_DIM] = (
                o * _silu(gate[r0:r0 + C, l0:l0 + HEAD_DIM])).astype(o_ref.dtype)


def _seg_spec(t, width, off):
    blk = off // width
    assert blk * width == off
    return pl.BlockSpec((1, t, width), lambda b, i: (b, i, blk))


def _gdn(proj, conv_w, a_log, dt_bias, norm_g):
    b, s, _ = proj.shape
    t = min(512, s)
    pad = lambda vec, off: jnp.zeros((1, 128), F32).at[0, off:off + N_HEADS].set(vec)
    full = lambda shape: pl.BlockSpec(shape, lambda bb, i: (0,) * len(shape))
    return pl.pallas_call(
        functools.partial(_gdn_kernel, T=t),
        grid=(b, s // t),
        in_specs=[_seg_spec(t, GROUP_W, OFF_AQ), _seg_spec(t, GROUP_W, OFF_AK),
                  _seg_spec(t, GROUP_W, OFF_AV), _seg_spec(t, GROUP_W, OFF_AG),
                  _seg_spec(t, 128, OFF_MISC),
                  full((GDN_CONV, 3 * GROUP_W)), full((1, 128)), full((1, 128)), full((1, 128))],
        out_specs=pl.BlockSpec((1, t, GROUP_W), lambda bb, i: (bb, i, 0)),
        out_shape=jax.ShapeDtypeStruct((b, s, GROUP_W), BF16),
        scratch_shapes=[pltpu.VMEM((3, t + 8, GROUP_W), F32),
                        pltpu.VMEM((N_HEADS, HEAD_DIM, HEAD_DIM), F32)],
        compiler_params=_cparams(("parallel", "arbitrary")),
        name="gdn",
    )(proj, proj, proj, proj, proj, conv_w, pad(a_log, MISC_DECAY), pad(dt_bias, MISC_DECAY),
      norm_g.reshape(1, HEAD_DIM))


def _lru_kernel(x_ref, gate_ref, cw_ref, cb_ref, wr_ref, br_ref, wi_ref, bi_ref, lam_ref,
                o_ref, hist_ref, h_ref, *, T):
    @pl.when(pl.program_id(1) == 0)
    def _():
        hist_ref[0:8, :] = jnp.zeros((8, GROUP_W), F32)
        h_ref[...] = jnp.zeros_like(h_ref)

    xin = x_ref[0]
    xc = _causal_conv4(xin, hist_ref, cw_ref[...]) + cb_ref[...]
    rs, igs = [], []
    for h in range(N_HEADS):
        xh = xc[:, h * HEAD_DIM:(h + 1) * HEAD_DIM].astype(BF16)
        rs.append(_dot(xh, wr_ref[h]))
        igs.append(_dot(xh, wi_ref[h]))
    r = jax.nn.sigmoid(jnp.concatenate(rs, axis=-1) + br_ref[...])
    ig = jax.nn.sigmoid(jnp.concatenate(igs, axis=-1) + bi_ref[...])
    log_a = -LRU_C * r * _softplus(-lam_ref[...])
    a = jnp.exp(log_a)
    bb = jnp.sqrt(-jnp.tanh(log_a) * (a * a + 1.0)) * (ig * xc)
    a3 = a.reshape(T // 8, 8, GROUP_W)
    b3 = bb.reshape(T // 8, 8, GROUP_W)
    sub = lax.broadcasted_iota(jnp.int32, (T // 8, 8, GROUP_W), 1)
    for d in (1, 2, 4):
        keep = sub >= d
        a_sh = jnp.where(keep, pltpu.roll(a3, d, 1), 1.0)
        b_sh = jnp.where(keep, pltpu.roll(b3, d, 1), 0.0)
        b3 = a3 * b_sh + b3
        a3 = a3 * a_sh
    h = h_ref[...]
    groups = []
    for g in range(T // 8):
        hg = b3[g] + a3[g] * h
        h = hg[7:8]
        groups.append(hg)
    h_ref[...] = h
    o_ref[0] = (jnp.concatenate(groups, axis=0) * _silu(gate_ref[0])).astype(o_ref.dtype)


def _lru(proj, conv_w, conv_b, w_r, b_r, w_i, b_i, lam):
    b, s, _ = proj.shape
    t = min(512, s)
    full = lambda shape: pl.BlockSpec(shape, lambda bb, i: (0,) * len(shape))
    vec = lambda v: v.reshape(1, GROUP_W)
    return pl.pallas_call(
        functools.partial(_lru_kernel, T=t),
        grid=(b, s // t),
        in_specs=[_seg_spec(t, GROUP_W, OFF_CX), _seg_spec(t, GROUP_W, OFF_CG),
                  full((LRU_CONV, GROUP_W)), full((1, GROUP_W)),
                  full((N_HEADS, HEAD_DIM, HEAD_DIM)), full((1, GROUP_W)),
                  full((N_HEADS, HEAD_DIM, HEAD_DIM)), full((1, GROUP_W)), full((1, GROUP_W))],
        out_specs=pl.BlockSpec((1, t, GROUP_W), lambda bb, i: (bb, i, 0)),
        out_shape=jax.ShapeDtypeStruct((b, s, GROUP_W), BF16),
        scratch_shapes=[pltpu.VMEM((t + 8, GROUP_W), F32), pltpu.VMEM((1, GROUP_W), F32)],
        compiler_params=_cparams(("parallel", "arbitrary")),
        name="rglru",
    )(proj, proj, conv_w, vec(conv_b), w_r.astype(BF16), vec(b_r), w_i.astype(BF16), vec(b_i),
      vec(lam))


def _inv_freq_row(d):
    i = np.arange(128) % (d // 2)
    return jnp.asarray((ROPE_THETA ** (-(2.0 * i) / d)).astype(np.float32).reshape(1, 128))


def _rope128(x, cos, sin_signed):
    return x * cos + pltpu.roll(x, 64, 1) * sin_signed


def _rope64(x, cos, sin_masked, lane):
    rot = jnp.where(lane < 32, -pltpu.roll(x, 96, 1), pltpu.roll(x, 32, 1))
    return x * cos + rot * sin_masked


def _rope_tables_kernel(pos_ref, invf64_ref, invf128_ref, o_ref):
    pos = pos_ref[0]
    lane = lax.broadcasted_iota(jnp.int32, pos.shape, 1)
    ang = pos * invf64_ref[...]
    o_ref[0, :, 0:128] = jnp.cos(ang)
    o_ref[0, :, 128:256] = jnp.where(lane < MLA_ROPE, jnp.sin(ang), 0.0)
    ang = pos * invf128_ref[...]
    sin = jnp.sin(ang)
    o_ref[0, :, 256:384] = jnp.cos(ang)
    o_ref[0, :, 384:512] = jnp.where(lane < 64, -sin, sin)


def _rope_tables(posb):
    b, s, _ = posb.shape
    t = min(512, s)
    full = lambda shape: pl.BlockSpec(shape, lambda bb, i: (0,) * len(shape))
    return pl.pallas_call(
        _rope_tables_kernel,
        grid=(b, s // t),
        in_specs=[pl.BlockSpec((1, t, 128), lambda bb, i: (bb, i, 0)), full((1, 128)), full((1, 128))],
        out_specs=pl.BlockSpec((1, t, 512), lambda bb, i: (bb, i, 0)),
        out_shape=jax.ShapeDtypeStruct((b, s, 512), F32),
        compiler_params=_cparams(("parallel", "parallel")),
        name="rope_tables",
    )(posb, _inv_freq_row(MLA_ROPE), _inv_freq_row(HEAD_DIM))


def _mla_prep_kernel(cq_ref, ckv_ref, misc_ref, rope_ref, qng_ref, wuq_ref, kvng_ref, wukv_ref,
                     qgn_ref, qgp_ref, kgn_ref, kgp_ref,
                     q_ref, k_ref, v_ref, *, T):
    scale = (MLA_NOPE + MLA_ROPE) ** -0.5 * LOG2E
    lane = lax.broadcasted_iota(jnp.int32, (T, 128), 1)
    cos = rope_ref[0, :, 0:128]
    sin = rope_ref[0, :, 128:256]

    cq = _rms(cq_ref[0], MLA_Q_RANK) * qng_ref[...]
    qf = _dot(cq.astype(BF16), wuq_ref[...])
    ckv = _rms(ckv_ref[0], MLA_KV_RANK) * kvng_ref[...]
    kvf = _dot(ckv.astype(BF16), wukv_ref[...])
    kpe = jnp.where(lane < MLA_ROPE, misc_ref[0], 0.0)
    kpe = _rope64(_rms(kpe, MLA_ROPE) * kgp_ref[...], cos, sin, lane)
    for h in range(N_HEADS):
        o = h * 256
        qn = _rms(qf[:, o:o + 128], MLA_NOPE) * qgn_ref[...]
        qp = _rope64(_rms(qf[:, o + 128:o + 256], MLA_ROPE) * qgp_ref[...], cos, sin, lane)
        q_ref[0, :, o:o + 128] = (qn * scale).astype(BF16)
        q_ref[0, :, o + 128:o + 256] = (qp * scale).astype(BF16)
        kn = _rms(kvf[:, o:o + 128], MLA_NOPE) * kgn_ref[...]
        k_ref[0, :, o:o + 128] = kn.astype(BF16)
        k_ref[0, :, o + 128:o + 256] = kpe.astype(BF16)
        v_ref[0, :, h * 128:(h + 1) * 128] = kvf[:, o + 128:o + 256].astype(BF16)


def _mla_prep(proj, rope, q_norm_g, w_uq, kv_norm_g, w_ukv, qk_norm_g):
    b, s, _ = proj.shape
    t = min(512, s)
    wuq = jnp.pad(w_uq, ((0, 0), (0, 0), (0, 64))).reshape(MLA_Q_RANK, N_HEADS * 256).astype(BF16)
    wukv = w_ukv.reshape(MLA_KV_RANK, N_HEADS * 256).astype(BF16)
    pad64 = lambda v: jnp.pad(v, (0, 64)).reshape(1, 128)
    full = lambda shape: pl.BlockSpec(shape, lambda bb, i: (0,) * len(shape))
    tok = lambda w: pl.BlockSpec((1, t, w), lambda bb, i: (bb, i, 0))
    return pl.pallas_call(
        functools.partial(_mla_prep_kernel, T=t),
        grid=(b, s // t),
        in_specs=[_seg_spec(t, MLA_Q_RANK, OFF_BCQ), _seg_spec(t, MLA_KV_RANK, OFF_BCKV),
                  _seg_spec(t, 128, OFF_MISC), pl.BlockSpec((1, t, 256), lambda bb, i: (bb, i, 0)),
                  full((1, MLA_Q_RANK)), full((MLA_Q_RANK, 1024)),
                  full((1, MLA_KV_RANK)), full((MLA_KV_RANK, 1024)),
                  full((1, 128)), full((1, 128)), full((1, 128)), full((1, 128))],
        out_specs=[tok(1024), tok(1024), tok(512)],
        out_shape=[jax.ShapeDtypeStruct((b, s, 1024), BF16),
                   jax.ShapeDtypeStruct((b, s, 1024), BF16),
                   jax.ShapeDtypeStruct((b, s, 512), BF16)],
        compiler_params=_cparams(("parallel", "parallel")),
        name="mla_prep",
    )(proj, proj, proj, rope, q_norm_g.reshape(1, -1), wuq, kv_norm_g.reshape(1, -1), wukv,
      qk_norm_g[0, :MLA_NOPE].reshape(1, 128), pad64(qk_norm_g[0, MLA_NOPE:]),
      qk_norm_g[1, :MLA_NOPE].reshape(1, 128), pad64(qk_norm_g[1, MLA_NOPE:]))


def _softmax_init(m_ref, l_ref, acc_ref):
    m_ref[...] = jnp.full_like(m_ref, NEG)
    l_ref[...] = jnp.zeros_like(l_ref)
    acc_ref[...] = jnp.zeros_like(acc_ref)


ROW_GROUPS = 2


def _attend_causal(q, k_ref, v_ref, n_past, kc, m_ref, l_ref, acc_ref, sa_ref, sb_ref, mask_fn):
    g_rows = q.shape[0] // ROW_GROUPS

    def scores(c, s_ref):
        k_blk = k_ref[0, pl.ds(pl.multiple_of(c * kc, kc), kc), :]
        for g in range(ROW_GROUPS):
            rows = slice(g * g_rows, (g + 1) * g_rows)
            s_ref[rows, :] = _dot_nt(q[rows], k_blk)

    def consume(c, s_ref, masked):
        k0 = pl.multiple_of(c * kc, kc)
        v_blk = v_ref[0, pl.ds(k0, kc), :]
        for g in range(ROW_GROUPS):
            rows = slice(g * g_rows, (g + 1) * g_rows)
            s = s_ref[rows, :]
            if masked:
                s = mask_fn(s, g * g_rows, k0)
            m_old = m_ref[rows, :]
            m_new = jnp.maximum(m_old, jnp.max(s, axis=-1, keepdims=True))
            alpha = jnp.exp2(m_old - m_new)
            p = jnp.exp2(s - m_new)
            part = p[:, 0:128]
            for j in range(1, kc // 128):
                part = part + p[:, j * 128:(j + 1) * 128]
            l_ref[rows, :] = alpha * l_ref[rows, :] + part
            acc_ref[rows, :] = alpha * acc_ref[rows, :] + _dot(p.astype(BF16), v_blk)
            m_ref[rows, :] = m_new

    def past_pair(j, carry):
        c = 2 * j
        scores(c + 1, sb_ref)
        consume(c, sa_ref, False)
        scores(c + 2, sa_ref)
        consume(c + 1, sb_ref, False)
        return carry

    scores(0, sa_ref)
    lax.fori_loop(0, n_past // 2, past_pair, 0)

    @pl.when(n_past % 2 == 0)
    def _():
        consume(n_past, sa_ref, True)

    @pl.when(n_past % 2 == 1)
    def _():
        scores(n_past, sb_ref)
        consume(n_past - 1, sa_ref, False)
        consume(n_past, sb_ref, True)


def _softmax_result(l_ref, acc_ref):
    return acc_ref[...] / jnp.sum(l_ref[...], axis=-1, keepdims=True)


def _mla_flash_kernel(q_ref, k_ref, v_ref, gate_ref, o_ref, m_ref, l_ref, acc_ref, sa_ref, sb_ref,
                      *, tq, kc):
    i = pl.program_id(2)
    _softmax_init(m_ref, l_ref, acc_ref)
    q = q_ref[0]

    def causal(s, first_row, k0):
        row = lax.broadcasted_iota(jnp.int32, s.shape, 0) + (i * tq + first_row)
        col = lax.broadcasted_iota(jnp.int32, s.shape, 1) + k0
        return jnp.where(col <= row, s, NEG)

    n_full = (i * tq) // kc
    _attend_causal(q, k_ref, v_ref, n_full, kc, m_ref, l_ref, acc_ref, sa_ref, sb_ref, causal)
    o_ref[0] = (_softmax_result(l_ref, acc_ref) * _silu(gate_ref[0])).astype(o_ref.dtype)


def _mla_flash(q, k, v, proj):
    b, s, _ = q.shape
    tq = min(1024, s)
    kc = min(1024, s)
    gate_blk = OFF_BG // 128
    return pl.pallas_call(
        functools.partial(_mla_flash_kernel, tq=tq, kc=kc),
        grid=(b, N_HEADS, s // tq),
        in_specs=[pl.BlockSpec((1, tq, 256), lambda bb, h, i: (bb, i, h)),
                  pl.BlockSpec((1, s, 256), lambda bb, h, i: (bb, 0, h)),
                  pl.BlockSpec((1, s, 128), lambda bb, h, i: (bb, 0, h)),
                  pl.BlockSpec((1, tq, 128), lambda bb, h, i: (bb, i, gate_blk + h))],
        out_specs=pl.BlockSpec((1, tq, 128), lambda bb, h, i: (bb, i, h)),
        out_shape=jax.ShapeDtypeStruct((b, s, GROUP_W), BF16),
        scratch_shapes=[pltpu.VMEM((tq, 1), F32), pltpu.VMEM((tq, 128), F32),
                        pltpu.VMEM((tq, 128), F32),
                        pltpu.VMEM((tq, kc), F32), pltpu.VMEM((tq, kc), F32)],
        compiler_params=_cparams(("parallel", "parallel", "arbitrary")),
        name="mla_flash",
    )(q, k, v, proj)


def _nsa_prep_kernel(dq_ref, dkv_ref, rope_ref, qg_ref, kg_ref,
                     q_ref, ks_ref, vs_ref, kw_ref, vw_ref, kc_ref, vc_ref, *, T):
    scale = HEAD_DIM ** -0.5 * LOG2E
    lane = lax.broadcasted_iota(jnp.int32, (T, 128), 1)
    cos = rope_ref[0, :, 0:128]
    sin = rope_ref[0, :, 128:256]
    dq = dq_ref[0]
    for h in range(N_HEADS):
        qh = _rms(dq[:, h * 128:(h + 1) * 128], HEAD_DIM) * qg_ref[...]
        q_ref[0, :, h * 128:(h + 1) * 128] = _rope128(qh, cos, sin) * scale
    kv = dkv_ref[0]
    kc_ref[0] = kv[:, 0:128]
    vc_ref[0] = kv[:, 128:256]
    ks_ref[0, :, 0:128] = _rope128(_rms(kv[:, 256:384], HEAD_DIM) * kg_ref[1:2], cos, sin).astype(BF16)
    key_blk = (pl.program_id(1) * T + lax.broadcasted_iota(jnp.int32, (T, 128), 0)) >> 6
    ks_ref[0, :, 128:256] = (lane == key_blk).astype(BF16)
    vs_ref[0] = kv[:, 384:512].astype(BF16)
    kw_ref[0] = _rope128(_rms(kv[:, 512:640], HEAD_DIM) * kg_ref[2:3], cos, sin).astype(BF16)
    vw_ref[0] = kv[:, 640:768].astype(BF16)


def _nsa_prep(proj, rope, q_norm_g, k_norm_g):
    b, s, _ = proj.shape
    t = min(512, s)
    full = lambda shape: pl.BlockSpec(shape, lambda bb, i: (0,) * len(shape))
    tok = lambda w: pl.BlockSpec((1, t, w), lambda bb, i: (bb, i, 0))
    sds = lambda w, dt: jax.ShapeDtypeStruct((b, s, w), dt)
    return pl.pallas_call(
        functools.partial(_nsa_prep_kernel, T=t),
        grid=(b, s // t),
        in_specs=[_seg_spec(t, GROUP_W, OFF_DQ), _seg_spec(t, 768, OFF_DKV),
                  pl.BlockSpec((1, t, 256), lambda bb, i: (bb, i, 1)),
                  full((1, 128)), full((3, 128))],
        out_specs=[tok(512), tok(256)] + [tok(128)] * 5,
        out_shape=[sds(512, F32), sds(256, BF16), sds(128, BF16), sds(128, BF16), sds(128, BF16),
                   sds(128, F32), sds(128, F32)],
        compiler_params=_cparams(("parallel", "parallel")),
        name="nsa_prep",
    )(proj, proj, rope, q_norm_g.reshape(1, HEAD_DIM), k_norm_g)


def _nsa_cmp_kernel(kt_ref, vt_ref, pe_ref, w1_ref, b1_ref, w2_ref, b2_ref, kg_ref, pos_ref,
                    invf_ref, kc_ref, vc_ref, *, NC):
    half = CMP_STRIDE * HEAD_DIM
    outs = []
    for j, t_ref in enumerate((kt_ref, vt_ref)):
        t2 = t_ref[0].astype(BF16)
        first = _dot(t2, w1_ref[j, :half, :])
        second = pltpu.roll(_dot(t2, w1_ref[j, half:, :]), NC - 1, 0)
        pe8 = jnp.broadcast_to(pe_ref[j], (8, CMP_LEN * HEAD_DIM)).astype(BF16)
        bias = _dot(pe8, w1_ref[j])[0:1] + b1_ref[j]
        hid = _silu(first + second + bias)
        outs.append(_dot(hid.astype(BF16), w2_ref[j]) + b2_ref[j])
    k_c, v_c = outs
    lane = lax.broadcasted_iota(jnp.int32, (NC, 128), 1)
    ang = pos_ref[0] * invf_ref[...]
    sin = jnp.sin(ang)
    sin = jnp.where(lane < 64, -sin, sin)
    kc_ref[0] = _rope128(_rms(k_c, HEAD_DIM) * kg_ref[...], jnp.cos(ang), sin)
    vc_ref[0] = v_c


def _nsa_compress(kc_raw, vc_raw, posc, cmp_pe, cmp_w1, cmp_b1, cmp_w2, cmp_b2, kg0):
    b, s, _ = kc_raw.shape
    nc = s // CMP_STRIDE
    kt = kc_raw.reshape(b, nc, CMP_STRIDE * HEAD_DIM)
    vt = vc_raw.reshape(b, nc, CMP_STRIDE * HEAD_DIM)
    full = lambda shape: pl.BlockSpec(shape, lambda bb: (0,) * len(shape))
    per_b = lambda shape: pl.BlockSpec((1,) + shape, lambda bb: (bb, 0, 0))
    return pl.pallas_call(
        functools.partial(_nsa_cmp_kernel, NC=nc),
        grid=(b,),
        in_specs=[per_b((nc, CMP_STRIDE * HEAD_DIM)), per_b((nc, CMP_STRIDE * HEAD_DIM)),
                  full((2, 1, CMP_LEN * HEAD_DIM)), full((2, CMP_LEN * HEAD_DIM, CMP_HIDDEN)),
                  full((2, 1, CMP_HIDDEN)), full((2, CMP_HIDDEN, HEAD_DIM)),
                  full((2, 1, HEAD_DIM)), full((1, HEAD_DIM)), per_b((nc, 128)), full((1, 128))],
        out_specs=[per_b((nc, HEAD_DIM)), per_b((nc, HEAD_DIM))],
        out_shape=[jax.ShapeDtypeStruct((b, nc, HEAD_DIM), F32),
                   jax.ShapeDtypeStruct((b, nc, HEAD_DIM), F32)],
        compiler_params=_cparams(("parallel",)),
        name="nsa_compress",
    )(kt, vt, cmp_pe.reshape(2, 1, CMP_LEN * HEAD_DIM), cmp_w1.astype(BF16),
      cmp_b1.reshape(2, 1, CMP_HIDDEN), cmp_w2.astype(BF16), cmp_b2.reshape(2, 1, HEAD_DIM),
      kg0.reshape(1, HEAD_DIM), posc, _inv_freq_row(HEAD_DIM))


def _nsa_attn_kernel(q_ref, kc_ref, vc_ref, ks_ref, vs_ref, kw_ref, vw_ref, misc_ref, gate_ref,
                     o_ref, m_ref, l_ref, acc_ref, sa_ref, sb_ref, *, S, NC, KC, WK):
    Q = Q_BLOCK
    R = N_HEADS
    i = pl.program_id(1)
    n_sel = S // SEL_LEN
    n_cmp = (S - CMP_LEN) // CMP_STRIDE + 1
    top_k = min(SEL_TOPK, n_sel)

    qf = q_ref[0]
    q4 = jnp.concatenate([qf[:, r * 128:(r + 1) * 128] for r in range(R)], axis=0)
    q4b = q4.astype(BF16)

    w0 = pl.multiple_of(jnp.maximum(i * Q - WINDOW, 0), Q)
    hi_w = (i * Q - w0) + lax.broadcasted_iota(jnp.int32, (Q, WK), 0)
    col_w = lax.broadcasted_iota(jnp.int32, (Q, WK), 1)
    band = jnp.where((col_w <= hi_w) & (col_w > hi_w - WINDOW), 0.0, NEG)
    s_w = _dot_nt(q4b, kw_ref[0, pl.ds(w0, WK), :]) + jnp.concatenate([band] * R, axis=0)
    p_w = jnp.exp2(s_w - jnp.max(s_w, axis=-1, keepdims=True))
    o_w = (_dot(p_w.astype(BF16), vw_ref[0, pl.ds(w0, WK), :])
           / jnp.sum(p_w, axis=-1, keepdims=True))

    q_hi, q_lo = _split_bf16(q4)
    k_hi, k_lo = _split_bf16(kc_ref[0])
    s_c = (_dot_nt(jnp.concatenate([q_hi, q_lo], axis=1), jnp.concatenate([k_hi, k_hi], axis=1))
           + _dot_nt(q_hi, k_lo))
    t_c = i * Q + (lax.broadcasted_iota(jnp.int32, (R * Q, NC), 0) & (Q - 1))
    c_ix = lax.broadcasted_iota(jnp.int32, (R * Q, NC), 1)
    valid_c = (c_ix * CMP_STRIDE + (CMP_LEN - 1) <= t_c) & (c_ix < n_cmp)
    s_c = jnp.where(valid_c, s_c, NEG)
    p_c = jnp.where(valid_c, jnp.exp2(s_c - jnp.max(s_c, axis=-1, keepdims=True)), 0.0)
    p_c = p_c / jnp.maximum(jnp.sum(p_c, axis=-1, keepdims=True), 1e-30)
    o_c = _bdot(p_c, vc_ref[0])
    assert NC & (NC - 1) == 0
    c_o = (lax.broadcasted_iota(jnp.int32, (2 * NC, 128), 0) & (NC - 1)) * CMP_STRIDE
    n_o = lax.broadcasted_iota(jnp.int32, (2 * NC, 128), 1) * SEL_LEN
    overlap2 = ((c_o < n_o + SEL_LEN) & (c_o + (CMP_LEN - 1) >= n_o)).astype(BF16)
    imp4 = _dot(jnp.concatenate(_split_bf16(p_c), axis=1), overlap2)
    imp = (imp4[0:Q] + imp4[Q:2 * Q] + imp4[2 * Q:3 * Q] + imp4[3 * Q:4 * Q]).T

    NP = min(128, -(-n_sel // 8) * 8)
    t_q = i * Q + lax.broadcasted_iota(jnp.int32, (NP, Q), 1)
    n_ix = lax.broadcasted_iota(jnp.int32, (NP, Q), 0)
    cur = t_q >> 6
    valid_s = (n_ix * SEL_LEN <= t_q) & (n_ix < n_sel)
    forced = (n_ix == 0) | (n_ix == cur) | (n_ix == cur - 1)
    val = jnp.where(valid_s, imp[:NP], -1.0)
    val = jnp.where(forced & valid_s, FORCE, val)
    val = jnp.where(n_ix < n_sel, val, -2.0)
    sel_t = jnp.zeros((NP, Q), F32)
    n_f = n_ix.astype(F32)
    for _ in range(top_k):
        mx = jnp.max(val, axis=0, keepdims=True)
        first = jnp.min(jnp.where(val == mx, n_f, 1e9), axis=0, keepdims=True)
        hit = n_f == first
        sel_t = jnp.where(hit, 1.0, sel_t)
        val = jnp.where(hit, -3.0, val)
    sel_t = jnp.where(valid_s, sel_t, 0.0)
    if NP < 128:
        sel_t = jnp.concatenate([sel_t, jnp.zeros((128 - NP, Q), F32)], axis=0)
    blk_bias = jnp.where(sel_t.T > 0.5, 0.0, NEG).astype(BF16)
    q_aug = jnp.concatenate([q4b, jnp.concatenate([blk_bias] * R, axis=0)], axis=1)

    _softmax_init(m_ref, l_ref, acc_ref)

    def causal_mask(s, first_row, k0):
        t_k = i * Q + (lax.broadcasted_iota(jnp.int32, s.shape, 0) & (Q - 1))
        return jnp.where(k0 + lax.broadcasted_iota(jnp.int32, s.shape, 1) <= t_k, s, NEG)

    n_past = (i * Q) // KC
    _attend_causal(q_aug, ks_ref, vs_ref, n_past, KC, m_ref, l_ref, acc_ref, sa_ref, sb_ref,
                   causal_mask)
    o_s = _softmax_result(l_ref, acc_ref)

    gates = jax.nn.sigmoid(misc_ref[0])
    gate = gate_ref[0]
    for r in range(R):
        g0 = gates[:, MISC_GL + 3 * r:MISC_GL + 3 * r + 1]
        g1 = gates[:, MISC_GL + 3 * r + 1:MISC_GL + 3 * r + 2]
        g2 = gates[:, MISC_GL + 3 * r + 2:MISC_GL + 3 * r + 3]
        rows = slice(r * Q, (r + 1) * Q)
        o = g0 * o_c[rows] + g1 * o_s[rows] + g2 * o_w[rows]
        o_ref[0, :, r * 128:(r + 1) * 128] = (
            o * _silu(gate[:, r * 128:(r + 1) * 128])).astype(o_ref.dtype)


def _nsa_attn(qf, k_c, v_c, ks, vs, kw, vw, proj):
    b, s, _ = qf.shape
    nc = s // CMP_STRIDE
    kc_len = min(1024, s)
    wk = min(WINDOW + Q_BLOCK, s)
    per_b = lambda n, w: pl.BlockSpec((1, n, w), lambda bb, i: (bb, 0, 0))
    return pl.pallas_call(
        functools.partial(_nsa_attn_kernel, S=s, NC=nc, KC=kc_len, WK=wk),
        grid=(b, s // Q_BLOCK),
        in_specs=[pl.BlockSpec((1, Q_BLOCK, GROUP_W), lambda bb, i: (bb, i, 0)),
                  per_b(nc, 128), per_b(nc, 128),
                  per_b(s, 256), per_b(s, 128), per_b(s, 128), per_b(s, 128),
                  _seg_spec(Q_BLOCK, 128, OFF_MISC), _seg_spec(Q_BLOCK, GROUP_W, OFF_DG)],
        out_specs=pl.BlockSpec((1, Q_BLOCK, GROUP_W), lambda bb, i: (bb, i, 0)),
        out_shape=jax.ShapeDtypeStruct((b, s, GROUP_W), BF16),
        scratch_shapes=[pltpu.VMEM((N_HEADS * Q_BLOCK, 1), F32),
                        pltpu.VMEM((N_HEADS * Q_BLOCK, 128), F32),
                        pltpu.VMEM((N_HEADS * Q_BLOCK, 128), F32),
                        pltpu.VMEM((N_HEADS * Q_BLOCK, kc_len), F32),
                        pltpu.VMEM((N_HEADS * Q_BLOCK, kc_len), F32)],
        compiler_params=_cparams(("parallel", "arbitrary")),
        name="nsa_attn",
    )(qf, k_c, v_c, ks, vs, kw, vw, proj, proj)


def _reorder_w_in_kernel(w_ref, o_ref):
    o = np.concatenate([[0], np.cumsum(IN_SIZES)]).tolist()
    (a_qkv, a_decay, a_beta, a_gate, b_cq, b_ckv, b_kpe, b_gate, c_x, c_gate,
     d_q, d_kv, d_gl, d_gate) = range(len(IN_SIZES))
    for l in range(w_ref.shape[1]):
        seg = lambda k: w_ref[o[k]:o[k + 1], l, :]
        off = 0
        for k in (a_qkv, a_gate, b_gate, c_x, c_gate, d_q, d_gate, d_kv, b_cq):
            o_ref[l, off:off + IN_SIZES[k], :] = seg(k).astype(BF16)
            off += IN_SIZES[k]
        assert off == OFF_MISC
        zeros = jnp.zeros((128 - MLA_ROPE - 2 * N_HEADS - 3 * N_HEADS, w_ref.shape[2]), F32)
        misc = jnp.concatenate([seg(b_kpe), seg(a_decay), seg(a_beta), seg(d_gl), zeros], axis=0)
        o_ref[l, OFF_MISC:OFF_MISC + 128, :] = misc.astype(BF16)
        o_ref[l, OFF_BCKV:OFF_BCKV + MLA_KV_RANK, :] = seg(b_ckv).astype(BF16)


def _reorder_w_in(w_in):
    depth, d, d_in = w_in.shape
    tc = 128
    w_t = jnp.transpose(w_in, (2, 0, 1))
    return pl.pallas_call(
        _reorder_w_in_kernel,
        grid=(d // tc,),
        in_specs=[pl.BlockSpec((d_in, depth, tc), lambda i: (0, 0, i))],
        out_specs=pl.BlockSpec((depth, D_PROJ, tc), lambda i: (0, 0, i)),
        out_shape=jax.ShapeDtypeStruct((depth, D_PROJ, d), BF16),
        compiler_params=_cparams(("parallel",)),
        name="reorder_w_in",
    )(w_t)


def kernel(x, positions, norm_g, w_in, w_out, gdn_conv_w, gdn_a_log, gdn_dt_bias, gdn_norm_g,
           mla_q_norm_g, mla_w_uq, mla_kv_norm_g, mla_w_ukv, mla_qk_norm_g,
           lru_conv_w, lru_conv_b, lru_w_r, lru_b_r, lru_w_i, lru_b_i, lru_lambda,
           nsa_q_norm_g, nsa_k_norm_g, nsa_cmp_pe, nsa_cmp_w1, nsa_cmp_b1, nsa_cmp_w2, nsa_cmp_b2):
    b, s, d = x.shape
    depth = w_in.shape[0]
    w_in_r = _reorder_w_in(w_in)
    w_out_b = w_out.astype(BF16)
    posf = positions.astype(F32)
    rope = _rope_tables(jnp.broadcast_to(posf[:, :, None], (b, s, 128)))
    nc = s // CMP_STRIDE
    pos_end = jnp.pad(posf.reshape(b, nc, CMP_STRIDE)[:, 1:, CMP_STRIDE - 1], ((0, 0), (0, 1)))
    posc = jnp.broadcast_to(pos_end[:, :, None], (b, nc, 128))

    x2d = x.reshape(b * s, d)
    for l in range(depth):
        proj = _inproj(x2d, norm_g[l].reshape(1, d), w_in_r, l).reshape(b, s, D_PROJ)
        y_a = _gdn(proj, gdn_conv_w[l], gdn_a_log[l], gdn_dt_bias[l], gdn_norm_g[l])
        q_b, k_b, v_b = _mla_prep(proj, rope, mla_q_norm_g[l], mla_w_uq[l], mla_kv_norm_g[l],
                                  mla_w_ukv[l], mla_qk_norm_g[l])
        y_b = _mla_flash(q_b, k_b, v_b, proj)
        y_c = _lru(proj, lru_conv_w[l], lru_conv_b[l], lru_w_r[l], lru_b_r[l], lru_w_i[l],
                   lru_b_i[l], lru_lambda[l])
        q_d, ks, vs, kw, vw, kc_raw, vc_raw = _nsa_prep(proj, rope, nsa_q_norm_g[l],
                                                        nsa_k_norm_g[l])
        k_c, v_c = _nsa_compress(kc_raw, vc_raw, posc, nsa_cmp_pe[l], nsa_cmp_w1[l],
                                 nsa_cmp_b1[l], nsa_cmp_w2[l], nsa_cmp_b2[l], nsa_k_norm_g[l, 0])
        y_d = _nsa_attn(q_d, k_c, v_c, ks, vs, kw, vw, proj)
        ys = [y.reshape(b * s, GROUP_W) for y in (y_a, y_b, y_c, y_d)]
        x2d = _outproj(ys, w_out_b[l], x2d)
    return x2d.reshape(b, s, d)
```

```python
import functools
import math

import numpy as np
import jax
import jax.numpy as jnp
from jax import lax
from jax.experimental import pallas as pl
from jax.experimental.pallas import tpu as pltpu

F32 = jnp.float32
BF16 = jnp.bfloat16
HIGHEST = lax.Precision.HIGHEST

D_MODEL = 2048
GROUP_W = 512
HEAD_DIM = 128
N_HEADS = 4
RMS_EPS = 1e-6
ROPE_THETA = 10000.0
NEG = -1e30
LOG2E = math.log2(math.e)
GDN_CONV = 4
GDN_CHUNK = 128
MLA_Q_RANK = 384
MLA_KV_RANK = 256
MLA_NOPE = 128
MLA_ROPE = 64
LRU_CONV = 4
LRU_C = 8.0
CMP_LEN = 32
CMP_STRIDE = 16
CMP_HIDDEN = 256
SEL_LEN = 64
SEL_TOPK = 16
WINDOW = 512
FORCE = 1e9
Q_BLOCK = 256

IN_SIZES = (3 * GROUP_W, N_HEADS, N_HEADS, GROUP_W,
            MLA_Q_RANK, MLA_KV_RANK, MLA_ROPE, GROUP_W,
            GROUP_W, GROUP_W,
            GROUP_W, 6 * HEAD_DIM, 3 * N_HEADS, GROUP_W)

OFF_AQ, OFF_AK, OFF_AV, OFF_AG = 0, 512, 1024, 1536
OFF_BG, OFF_CX, OFF_CG, OFF_DQ, OFF_DG = 2048, 2560, 3072, 3584, 4096
OFF_DKV, OFF_BCQ, OFF_MISC, OFF_BCKV = 4608, 5376, 5760, 5888
D_PROJ = 6144
MISC_KPE, MISC_DECAY, MISC_BETA, MISC_GL = 0, 64, 68, 72

VMEM_LIMIT = 56 * 1024 * 1024


def _cparams(sem):
    return pltpu.CompilerParams(dimension_semantics=sem, vmem_limit_bytes=VMEM_LIMIT)


def _dot(a, b, precision=None):
    return lax.dot_general(a, b, (((1,), (0,)), ((), ())), precision=precision,
                           preferred_element_type=F32)


def _dot_nt(a, b, precision=None):
    return lax.dot_general(a, b, (((1,), (1,)), ((), ())), precision=precision,
                           preferred_element_type=F32)


def _bdot(a, b):
    return _dot(a.astype(BF16), b.astype(BF16))


def _bdot_nt(a, b):
    return _dot_nt(a.astype(BF16), b.astype(BF16))


def _split_bf16(x):
    hi = x.astype(BF16)
    return hi, (x - hi.astype(F32)).astype(BF16)


def _silu(x):
    h = 0.5 * x
    return h + h * jnp.tanh(h)


def _softplus(x):
    return jnp.maximum(x, 0.0) + jnp.log1p(jnp.exp(-jnp.abs(x)))


def _rms(x, n):
    return x * lax.rsqrt(jnp.sum(x * x, axis=-1, keepdims=True) * (1.0 / n) + RMS_EPS)


def _inproj_kernel(x_ref, g_ref, w_ref, o_ref, h_ref, *, tm):
    @pl.when(pl.program_id(1) == 0)
    def _():
        rows = min(256, tm)
        for r in range(tm // rows):
            x = x_ref[r * rows:(r + 1) * rows, :]
            h_ref[r * rows:(r + 1) * rows, :] = (_rms(x, D_MODEL) * g_ref[...]).astype(BF16)

    o_ref[...] = _dot_nt(h_ref[...], w_ref[...])


def _inproj(x2d, g, w_t, layer):
    m = x2d.shape[0]
    tm = min(1024, m)
    tn = 1024
    return pl.pallas_call(
        functools.partial(_inproj_kernel, tm=tm),
        grid=(m // tm, D_PROJ // tn),
        in_specs=[pl.BlockSpec((tm, D_MODEL), lambda i, j: (i, 0)),
                  pl.BlockSpec((1, D_MODEL), lambda i, j: (0, 0)),
                  pl.BlockSpec((None, tn, D_MODEL), lambda i, j: (layer, j, 0))],
        out_specs=pl.BlockSpec((tm, tn), lambda i, j: (i, j)),
        out_shape=jax.ShapeDtypeStruct((m, D_PROJ), F32),
        scratch_shapes=[pltpu.VMEM((tm, D_MODEL), BF16)],
        compiler_params=_cparams(("parallel", "arbitrary")),
        name="inproj",
    )(x2d, g, w_t)


def _outproj_kernel(ya_ref, yb_ref, yc_ref, yd_ref, wa_ref, wb_ref, wc_ref, wd_ref, x_ref, o_ref):
    acc = x_ref[...]
    for y_ref, w_ref in ((ya_ref, wa_ref), (yb_ref, wb_ref), (yc_ref, wc_ref), (yd_ref, wd_ref)):
        acc = acc + _dot(y_ref[...].astype(BF16), w_ref[...])
    o_ref[...] = acc


def _outproj(ys, w, x2d):
    m = x2d.shape[0]
    tm = min(1024, m)
    tn = 1024
    y_specs = [pl.BlockSpec((tm, GROUP_W), lambda i, j: (i, 0)) for _ in range(4)]
    w_specs = [pl.BlockSpec((GROUP_W, tn), functools.partial(lambda i, j, g: (g, j), g=g))
               for g in range(4)]
    return pl.pallas_call(
        _outproj_kernel,
        grid=(m // tm, D_MODEL // tn),
        in_specs=y_specs + w_specs + [pl.BlockSpec((tm, tn), lambda i, j: (i, j))],
        out_specs=pl.BlockSpec((tm, tn), lambda i, j: (i, j)),
        out_shape=jax.ShapeDtypeStruct((m, D_MODEL), F32),
        compiler_params=_cparams(("parallel", "arbitrary")),
        name="outproj",
    )(*ys, w, w, w, w, x2d)


def _causal_conv4(xin, ext_ref, w):
    t = xin.shape[0]
    ext_ref[8:8 + t, :] = xin
    ext = ext_ref[...]
    acc = ext * w[0:1]
    for j in range(1, 4):
        acc = pltpu.roll(acc, 1, 0) + ext * w[j:j + 1]
    ext_ref[0:8, :] = xin[t - 8:t]
    return acc[8:8 + t]


def _gdn_kernel(q_ref, k_ref, v_ref, gate_ref, misc_ref, cw_ref, alog_ref, dtb_ref, ng_ref,
                o_ref, hist_ref, state_ref, *, T):
    C = GDN_CHUNK

    @pl.when(pl.program_id(1) == 0)
    def _():
        hist_ref[:, 0:8, :] = jnp.zeros((3, 8, GROUP_W), F32)
        state_ref[...] = jnp.zeros_like(state_ref)

    cw = cw_ref[...]
    conv = []
    for idx, ref in enumerate((q_ref, k_ref, v_ref)):
        xin = ref[0]
        acc = _causal_conv4(xin, hist_ref.at[idx], cw[:, idx * GROUP_W:(idx + 1) * GROUP_W])
        conv.append(_silu(acc))
    qc, kc, vc = conv

    misc = misc_ref[0]
    gfull = -jnp.exp(alog_ref[...]) * _softplus(misc + dtb_ref[...])
    betafull = jax.nn.sigmoid(misc)
    gate = gate_ref[0]

    row = lax.broadcasted_iota(jnp.int32, (C, C), 0)
    col = lax.broadcasted_iota(jnp.int32, (C, C), 1)
    lower = row >= col
    strict = row > col
    ltri = lower.astype(F32)
    row2 = lax.broadcasted_iota(jnp.int32, (C, 2 * C), 0)
    lane2 = lax.broadcasted_iota(jnp.int32, (C, 2 * C), 1)
    left = lane2 < C
    col2 = lane2 & (C - 1)
    leaf_bits = 4
    same_blk = {bits: (row2 >> bits) == (col2 >> bits) for bits in range(leaf_bits, 8)}
    assert C == HEAD_DIM == 1 << 7

    def pairdot(x, y):
        yb = y.astype(BF16)
        zero = jnp.zeros_like(yb)
        bd = jnp.concatenate([jnp.where(left, yb, zero), jnp.where(left, zero, yb)], axis=0)
        return _dot(x.astype(BF16), bd)

    n_chunks = T // C
    hds = []
    for c in range(n_chunks):
        r0 = c * C
        gall = _dot(ltri, gfull[r0:r0 + C], HIGHEST)
        hd = []
        hds.append(hd)
        for h in range(N_HEADS):
            l0 = h * HEAD_DIM
            q = qc[r0:r0 + C, l0:l0 + HEAD_DIM]
            k = kc[r0:r0 + C, l0:l0 + HEAD_DIM]
            v = vc[r0:r0 + C, l0:l0 + HEAD_DIM]
            q = q * lax.rsqrt(jnp.sum(q * q, axis=-1, keepdims=True) + RMS_EPS) * (HEAD_DIM ** -0.5)
            k = k * lax.rsqrt(jnp.sum(k * k, axis=-1, keepdims=True) + RMS_EPS)
            beta = betafull[r0:r0 + C, MISC_BETA + h:MISC_BETA + h + 1]
            gc = jnp.broadcast_to(gall[:, MISC_DECAY + h:MISC_DECAY + h + 1], (C, C))
            decay = jnp.exp(jnp.where(lower, gc - gc.T, NEG))
            kb = k * beta
            a = jnp.where(strict, _bdot_nt(kb, k) * decay, 0.0)
            hd.append((q, k, v, beta, gc, decay, kb, a))

    a2s = [jnp.concatenate([hds[c][h0][7], hds[c][h0 + 1][7]], axis=1)
           for c in range(n_chunks) for h0 in range(0, N_HEADS, 2)]
    ps = [jnp.where(same_blk[leaf_bits], -a2, 0.0) for a2 in a2s]
    tm2 = list(ps)
    for _ in range(leaf_bits - 1):
        ps = [pairdot(p, p) for p in ps]
        tm2 = [tm + p + pairdot(tm, p) for tm, p in zip(tm2, ps)]
    for bits in range(leaf_bits + 1, 8):
        join = same_blk[bits] & jnp.logical_not(same_blk[bits - 1])
        tas = [jnp.where(join, a2, 0.0) for a2 in a2s]
        tas = [off + pairdot(tm, off) for tm, off in zip(tm2, tas)]
        tm2 = [tm - (ta + pairdot(ta, tm)) for tm, ta in zip(tm2, tas)]

    for c in range(n_chunks):
        r0 = c * C
        hd = hds[c]
        tms = []
        for tm in tm2[2 * c:2 * c + 2]:
            tms += [tm[:, :C], tm[:, C:]]
        for h in range(N_HEADS):
            l0 = h * HEAD_DIM
            q, k, v, beta, gc, decay, kb, _ = hd[h]
            eg = jnp.exp(gc)
            rhs = jnp.concatenate([v * beta, kb * eg], axis=-1)
            uw = rhs + _bdot(tms[h], rhs)
            u = uw[:, :HEAD_DIM]
            w = uw[:, HEAD_DIM:]
            qk = _bdot_nt(q, k) * decay
            q_dec = q * eg
            g_last = gc[C - 1:C, :]
            k_end = k * jnp.exp(g_last - gc)
            c_dec = jnp.exp(g_last)
            state = state_ref[h]
            v_new = u - _bdot(w, state)
            o = _bdot(q_dec, state) + _bdot(qk, v_new)
            state_ref[h] = state * c_dec + lax.dot_general(
                k_end.astype(BF16), v_new.astype(BF16), (((0,), (0,)), ((), ())),
                preferred_element_type=F32)
            o = _rms(o, HEAD_DIM) * ng_ref[...]
            o_ref[0, r0:r0 + C, l0:l0 + HEAD_DIM] = (
                o * _silu(gate[r0:r0 + C, l0:l0 + HEAD_DIM])).astype(o_ref.dtype)


def _seg_spec(t, width, off):
    blk = off // width
    assert blk * width == off
    return pl.BlockSpec((1, t, width), lambda b, i: (b, i, blk))


def _gdn(proj, conv_w, a_log, dt_bias, norm_g):
    b, s, _ = proj.shape
    t = min(512, s)
    pad = lambda vec, off: jnp.zeros((1, 128), F32).at[0, off:off + N_HEADS].set(vec)
    full = lambda shape: pl.BlockSpec(shape, lambda bb, i: (0,) * len(shape))
    return pl.pallas_call(
        functools.partial(_gdn_kernel, T=t),
        grid=(b, s // t),
        in_specs=[_seg_spec(t, GROUP_W, OFF_AQ), _seg_spec(t, GROUP_W, OFF_AK),
                  _seg_spec(t, GROUP_W, OFF_AV), _seg_spec(t, GROUP_W, OFF_AG),
                  _seg_spec(t, 128, OFF_MISC),
                  full((GDN_CONV, 3 * GROUP_W)), full((1, 128)), full((1, 128)), full((1, 128))],
        out_specs=pl.BlockSpec((1, t, GROUP_W), lambda bb, i: (bb, i, 0)),
        out_shape=jax.ShapeDtypeStruct((b, s, GROUP_W), BF16),
        scratch_shapes=[pltpu.VMEM((3, t + 8, GROUP_W), F32),
                        pltpu.VMEM((N_HEADS, HEAD_DIM, HEAD_DIM), F32)],
        compiler_params=_cparams(("parallel", "arbitrary")),
        name="gdn",
    )(proj, proj, proj, proj, proj, conv_w, pad(a_log, MISC_DECAY), pad(dt_bias, MISC_DECAY),
      norm_g.reshape(1, HEAD_DIM))


def _lru_kernel(x_ref, gate_ref, cw_ref, cb_ref, wr_ref, br_ref, wi_ref, bi_ref, lam_ref,
                o_ref, hist_ref, h_ref, *, T):
    @pl.when(pl.program_id(1) == 0)
    def _():
        hist_ref[0:8, :] = jnp.zeros((8, GROUP_W), F32)
        h_ref[...] = jnp.zeros_like(h_ref)

    xin = x_ref[0]
    xc = _causal_conv4(xin, hist_ref, cw_ref[...]) + cb_ref[...]
    rs, igs = [], []
    for h in range(N_HEADS):
        xh = xc[:, h * HEAD_DIM:(h + 1) * HEAD_DIM].astype(BF16)
        rs.append(_dot(xh, wr_ref[h]))
        igs.append(_dot(xh, wi_ref[h]))
    r = jax.nn.sigmoid(jnp.concatenate(rs, axis=-1) + br_ref[...])
    ig = jax.nn.sigmoid(jnp.concatenate(igs, axis=-1) + bi_ref[...])
    log_a = -LRU_C * r * _softplus(-lam_ref[...])
    a = jnp.exp(log_a)
    bb = jnp.sqrt(-jnp.tanh(log_a) * (a * a + 1.0)) * (ig * xc)
    a3 = a.reshape(T // 8, 8, GROUP_W)
    b3 = bb.reshape(T // 8, 8, GROUP_W)
    sub = lax.broadcasted_iota(jnp.int32, (T // 8, 8, GROUP_W), 1)
    for d in (1, 2, 4):
        keep = sub >= d
        a_sh = jnp.where(keep, pltpu.roll(a3, d, 1), 1.0)
        b_sh = jnp.where(keep, pltpu.roll(b3, d, 1), 0.0)
        b3 = a3 * b_sh + b3
        a3 = a3 * a_sh
    h = h_ref[...]
    groups = []
    for g in range(T // 8):
        hg = b3[g] + a3[g] * h
        h = hg[7:8]
        groups.append(hg)
    h_ref[...] = h
    o_ref[0] = (jnp.concatenate(groups, axis=0) * _silu(gate_ref[0])).astype(o_ref.dtype)


def _lru(proj, conv_w, conv_b, w_r, b_r, w_i, b_i, lam):
    b, s, _ = proj.shape
    t = min(512, s)
    full = lambda shape: pl.BlockSpec(shape, lambda bb, i: (0,) * len(shape))
    vec = lambda v: v.reshape(1, GROUP_W)
    return pl.pallas_call(
        functools.partial(_lru_kernel, T=t),
        grid=(b, s // t),
        in_specs=[_seg_spec(t, GROUP_W, OFF_CX), _seg_spec(t, GROUP_W, OFF_CG),
                  full((LRU_CONV, GROUP_W)), full((1, GROUP_W)),
                  full((N_HEADS, HEAD_DIM, HEAD_DIM)), full((1, GROUP_W)),
                  full((N_HEADS, HEAD_DIM, HEAD_DIM)), full((1, GROUP_W)), full((1, GROUP_W))],
        out_specs=pl.BlockSpec((1, t, GROUP_W), lambda bb, i: (bb, i, 0)),
        out_shape=jax.ShapeDtypeStruct((b, s, GROUP_W), BF16),
        scratch_shapes=[pltpu.VMEM((t + 8, GROUP_W), F32), pltpu.VMEM((1, GROUP_W), F32)],
        compiler_params=_cparams(("parallel", "arbitrary")),
        name="rglru",
    )(proj, proj, conv_w, vec(conv_b), w_r.astype(BF16), vec(b_r), w_i.astype(BF16), vec(b_i),
      vec(lam))


def _inv_freq_row(d):
    i = np.arange(128) % (d // 2)
    return jnp.asarray((ROPE_THETA ** (-(2.0 * i) / d)).astype(np.float32).reshape(1, 128))


def _rope128(x, cos, sin_signed):
    return x * cos + pltpu.roll(x, 64, 1) * sin_signed


def _rope64(x, cos, sin_masked, lane):
    rot = jnp.where(lane < 32, -pltpu.roll(x, 96, 1), pltpu.roll(x, 32, 1))
    return x * cos + rot * sin_masked


def _rope_tables_kernel(pos_ref, invf_ref, o_ref):
    lane = lax.broadcasted_iota(jnp.int32, pos_ref.shape[1:], 1)
    ang = pos_ref[0] * invf_ref[...]
    c = jnp.cos(ang)
    s = jnp.sin(ang)
    c32, c96 = pltpu.roll(c, 32, 1), pltpu.roll(c, 96, 1)
    s32, s96 = pltpu.roll(s, 32, 1), pltpu.roll(s, 96, 1)
    o_ref[0, :, 0:128] = jnp.where(lane < 32, c, c32)
    o_ref[0, :, 128:256] = jnp.where(lane < 32, s, jnp.where(lane < MLA_ROPE, s32, 0.0))
    o_ref[0, :, 256:384] = jnp.where(lane < 64, c96, c32)
    o_ref[0, :, 384:512] = jnp.where(lane < 64, -s96, s32)


def _rope_tables(posb):
    b, s, _ = posb.shape
    t = min(512, s)
    full = lambda shape: pl.BlockSpec(shape, lambda bb, i: (0,) * len(shape))
    return pl.pallas_call(
        _rope_tables_kernel,
        grid=(b, s // t),
        in_specs=[pl.BlockSpec((1, t, 128), lambda bb, i: (bb, i, 0)), full((1, 128))],
        out_specs=pl.BlockSpec((1, t, 512), lambda bb, i: (bb, i, 0)),
        out_shape=jax.ShapeDtypeStruct((b, s, 512), F32),
        compiler_params=_cparams(("parallel", "parallel")),
        name="rope_tables",
    )(posb, jnp.concatenate([_inv_freq_row(MLA_ROPE)[:, :32], _inv_freq_row(HEAD_DIM)[:, :64],
                             jnp.zeros((1, 32), F32)], axis=1))


def _mla_prep_kernel(cq_ref, ckv_ref, misc_ref, rope_ref, qng_ref, wuq_ref, kvng_ref, wukv_ref,
                     qgn_ref, qgp_ref, kgn_ref, kgp_ref,
                     q_ref, k_ref, v_ref, *, T):
    scale = (MLA_NOPE + MLA_ROPE) ** -0.5 * LOG2E
    lane = lax.broadcasted_iota(jnp.int32, (T, 128), 1)
    cos = rope_ref[0, :, 0:128]
    sin = rope_ref[0, :, 128:256]

    cq = _rms(cq_ref[0], MLA_Q_RANK) * qng_ref[...]
    qf = _dot(cq.astype(BF16), wuq_ref[...])
    ckv = _rms(ckv_ref[0], MLA_KV_RANK) * kvng_ref[...]
    kvf = _dot(ckv.astype(BF16), wukv_ref[...])
    kpe = jnp.where(lane < MLA_ROPE, misc_ref[0], 0.0)
    kpe = _rope64(_rms(kpe, MLA_ROPE) * kgp_ref[...], cos, sin, lane)
    for h in range(N_HEADS):
        o = h * 256
        qn = _rms(qf[:, o:o + 128], MLA_NOPE) * qgn_ref[...]
        qp = _rope64(_rms(qf[:, o + 128:o + 256], MLA_ROPE) * qgp_ref[...], cos, sin, lane)
        q_ref[0, :, o:o + 128] = (qn * scale).astype(BF16)
        q_ref[0, :, o + 128:o + 256] = (qp * scale).astype(BF16)
        kn = _rms(kvf[:, o:o + 128], MLA_NOPE) * kgn_ref[...]
        k_ref[0, :, o:o + 128] = kn.astype(BF16)
        k_ref[0, :, o + 128:o + 256] = kpe.astype(BF16)
        v_ref[0, :, h * 128:(h + 1) * 128] = kvf[:, o + 128:o + 256].astype(BF16)


def _mla_prep(proj, rope, q_norm_g, w_uq, kv_norm_g, w_ukv, qk_norm_g):
    b, s, _ = proj.shape
    t = min(512, s)
    wuq = jnp.pad(w_uq, ((0, 0), (0, 0), (0, 64))).reshape(MLA_Q_RANK, N_HEADS * 256).astype(BF16)
    wukv = w_ukv.reshape(MLA_KV_RANK, N_HEADS * 256).astype(BF16)
    pad64 = lambda v: jnp.pad(v, (0, 64)).reshape(1, 128)
    full = lambda shape: pl.BlockSpec(shape, lambda bb, i: (0,) * len(shape))
    tok = lambda w: pl.BlockSpec((1, t, w), lambda bb, i: (bb, i, 0))
    return pl.pallas_call(
        functools.partial(_mla_prep_kernel, T=t),
        grid=(b, s // t),
        in_specs=[_seg_spec(t, MLA_Q_RANK, OFF_BCQ), _seg_spec(t, MLA_KV_RANK, OFF_BCKV),
                  _seg_spec(t, 128, OFF_MISC), pl.BlockSpec((1, t, 256), lambda bb, i: (bb, i, 0)),
                  full((1, MLA_Q_RANK)), full((MLA_Q_RANK, 1024)),
                  full((1, MLA_KV_RANK)), full((MLA_KV_RANK, 1024)),
                  full((1, 128)), full((1, 128)), full((1, 128)), full((1, 128))],
        out_specs=[tok(1024), tok(1024), tok(512)],
        out_shape=[jax.ShapeDtypeStruct((b, s, 1024), BF16),
                   jax.ShapeDtypeStruct((b, s, 1024), BF16),
                   jax.ShapeDtypeStruct((b, s, 512), BF16)],
        compiler_params=_cparams(("parallel", "parallel")),
        name="mla_prep",
    )(proj, proj, proj, rope, q_norm_g.reshape(1, -1), wuq, kv_norm_g.reshape(1, -1), wukv,
      qk_norm_g[0, :MLA_NOPE].reshape(1, 128), pad64(qk_norm_g[0, MLA_NOPE:]),
      qk_norm_g[1, :MLA_NOPE].reshape(1, 128), pad64(qk_norm_g[1, MLA_NOPE:]))


ROW_GROUPS = 2


def _attn_scratch(m, kc):
    return [pltpu.VMEM((m, 1), F32), pltpu.VMEM((m, 128), F32), pltpu.VMEM((m, 128), F32),
            pltpu.VMEM((m, kc), F32), pltpu.VMEM((m, kc), F32)]


def _attend_causal(q, k_ref, v_ref, n_past, kc, scratch, mask_fn):
    m_ref, l_ref, acc_ref, sa_ref, sb_ref = scratch
    m_ref[...] = jnp.full_like(m_ref, NEG)
    l_ref[...] = jnp.zeros_like(l_ref)
    acc_ref[...] = jnp.zeros_like(acc_ref)
    g_rows = q.shape[0] // ROW_GROUPS

    def scores(c, s_ref):
        k_blk = k_ref[0, pl.ds(pl.multiple_of(c * kc, kc), kc), :]
        for g in range(ROW_GROUPS):
            rows = slice(g * g_rows, (g + 1) * g_rows)
            s_ref[rows, :] = _dot_nt(q[rows], k_blk)

    def consume(c, s_ref, masked):
        k0 = pl.multiple_of(c * kc, kc)
        v_blk = v_ref[0, pl.ds(k0, kc), :]
        for g in range(ROW_GROUPS):
            rows = slice(g * g_rows, (g + 1) * g_rows)
            s = s_ref[rows, :]
            if masked:
                s = mask_fn(s, g * g_rows, k0)
            m_old = m_ref[rows, :]
            m_new = jnp.maximum(m_old, jnp.max(s, axis=-1, keepdims=True))
            alpha = jnp.exp2(m_old - m_new)
            p = jnp.exp2(s - m_new)
            part = p[:, 0:128]
            for j in range(1, kc // 128):
                part = part + p[:, j * 128:(j + 1) * 128]
            l_ref[rows, :] = alpha * l_ref[rows, :] + part
            acc_ref[rows, :] = alpha * acc_ref[rows, :] + _dot(p.astype(BF16), v_blk)
            m_ref[rows, :] = m_new

    def past_pair(j, carry):
        c = 2 * j
        scores(c + 1, sb_ref)
        consume(c, sa_ref, False)
        scores(c + 2, sa_ref)
        consume(c + 1, sb_ref, False)
        return carry

    scores(0, sa_ref)
    lax.fori_loop(0, n_past // 2, past_pair, 0)

    @pl.when(n_past % 2 == 0)
    def _():
        consume(n_past, sa_ref, True)

    @pl.when(n_past % 2 == 1)
    def _():
        scores(n_past, sb_ref)
        consume(n_past - 1, sa_ref, False)
        consume(n_past, sb_ref, True)

    return acc_ref[...] / jnp.sum(l_ref[...], axis=-1, keepdims=True)


def _mla_flash_kernel(q_ref, k_ref, v_ref, gate_ref, o_ref, *scratch, tq, kc):
    i = pl.program_id(2)
    q = q_ref[0]

    def causal(s, first_row, k0):
        row = lax.broadcasted_iota(jnp.int32, s.shape, 0) + (i * tq + first_row)
        col = lax.broadcasted_iota(jnp.int32, s.shape, 1) + k0
        return jnp.where(col <= row, s, NEG)

    n_full = (i * tq) // kc
    o = _attend_causal(q, k_ref, v_ref, n_full, kc, scratch, causal)
    o_ref[0] = (o * _silu(gate_ref[0])).astype(o_ref.dtype)


def _mla_flash(q, k, v, proj):
    b, s, _ = q.shape
    tq = min(1024, s)
    kc = min(1024, s)
    gate_blk = OFF_BG // 128
    return pl.pallas_call(
        functools.partial(_mla_flash_kernel, tq=tq, kc=kc),
        grid=(b, N_HEADS, s // tq),
        in_specs=[pl.BlockSpec((1, tq, 256), lambda bb, h, i: (bb, i, h)),
                  pl.BlockSpec((1, s, 256), lambda bb, h, i: (bb, 0, h)),
                  pl.BlockSpec((1, s, 128), lambda bb, h, i: (bb, 0, h)),
                  pl.BlockSpec((1, tq, 128), lambda bb, h, i: (bb, i, gate_blk + h))],
        out_specs=pl.BlockSpec((1, tq, 128), lambda bb, h, i: (bb, i, h)),
        out_shape=jax.ShapeDtypeStruct((b, s, GROUP_W), BF16),
        scratch_shapes=_attn_scratch(tq, kc),
        compiler_params=_cparams(("parallel", "parallel", "arbitrary")),
        name="mla_flash",
    )(q, k, v, proj)


def _nsa_prep_kernel(dq_ref, dkv_ref, rope_ref, qg_ref, kg_ref,
                     q_ref, ks_ref, vs_ref, kw_ref, vw_ref, kc_ref, vc_ref, *, T):
    scale = HEAD_DIM ** -0.5 * LOG2E
    lane = lax.broadcasted_iota(jnp.int32, (T, 128), 1)
    cos = rope_ref[0, :, 0:128]
    sin = rope_ref[0, :, 128:256]
    dq = dq_ref[0]
    for h in range(N_HEADS):
        qh = _rms(dq[:, h * 128:(h + 1) * 128], HEAD_DIM) * qg_ref[...]
        q_ref[0, :, h * 128:(h + 1) * 128] = _rope128(qh, cos, sin) * scale
    kv = dkv_ref[0]
    kc_ref[0] = kv[:, 0:128]
    vc_ref[0] = kv[:, 128:256]
    ks_ref[0, :, 0:128] = _rope128(_rms(kv[:, 256:384], HEAD_DIM) * kg_ref[1:2], cos, sin).astype(BF16)
    key_blk = (pl.program_id(1) * T + lax.broadcasted_iota(jnp.int32, (T, 128), 0)) >> 6
    ks_ref[0, :, 128:256] = (lane == key_blk).astype(BF16)
    vs_ref[0] = kv[:, 384:512].astype(BF16)
    kw_ref[0] = _rope128(_rms(kv[:, 512:640], HEAD_DIM) * kg_ref[2:3], cos, sin).astype(BF16)
    vw_ref[0] = kv[:, 640:768].astype(BF16)


def _nsa_prep(proj, rope, q_norm_g, k_norm_g):
    b, s, _ = proj.shape
    t = min(512, s)
    full = lambda shape: pl.BlockSpec(shape, lambda bb, i: (0,) * len(shape))
    tok = lambda w: pl.BlockSpec((1, t, w), lambda bb, i: (bb, i, 0))
    sds = lambda w, dt: jax.ShapeDtypeStruct((b, s, w), dt)
    return pl.pallas_call(
        functools.partial(_nsa_prep_kernel, T=t),
        grid=(b, s // t),
        in_specs=[_seg_spec(t, GROUP_W, OFF_DQ), _seg_spec(t, 768, OFF_DKV),
                  pl.BlockSpec((1, t, 256), lambda bb, i: (bb, i, 1)),
                  full((1, 128)), full((3, 128))],
        out_specs=[tok(512), tok(256)] + [tok(128)] * 5,
        out_shape=[sds(512, F32), sds(256, BF16), sds(128, BF16), sds(128, BF16), sds(128, BF16),
                   sds(128, F32), sds(128, F32)],
        compiler_params=_cparams(("parallel", "parallel")),
        name="nsa_prep",
    )(proj, proj, rope, q_norm_g.reshape(1, HEAD_DIM), k_norm_g)


def _nsa_cmp_kernel(kt_ref, vt_ref, pe_ref, w1_ref, b1_ref, w2_ref, b2_ref, kg_ref, pos_ref,
                    invf_ref, kc_ref, vc_ref, *, NC):
    half = CMP_STRIDE * HEAD_DIM
    outs = []
    for j, t_ref in enumerate((kt_ref, vt_ref)):
        t2 = t_ref[0].astype(BF16)
        first = _dot(t2, w1_ref[j, :half, :])
        second = pltpu.roll(_dot(t2, w1_ref[j, half:, :]), NC - 1, 0)
        pe8 = jnp.broadcast_to(pe_ref[j], (8, CMP_LEN * HEAD_DIM)).astype(BF16)
        bias = _dot(pe8, w1_ref[j])[0:1] + b1_ref[j]
        hid = _silu(first + second + bias)
        outs.append(_dot(hid.astype(BF16), w2_ref[j]) + b2_ref[j])
    k_c, v_c = outs
    lane = lax.broadcasted_iota(jnp.int32, (NC, 128), 1)
    ang = pos_ref[0] * invf_ref[...]
    sin = jnp.sin(ang)
    sin = jnp.where(lane < 64, -sin, sin)
    kc_ref[0] = _rope128(_rms(k_c, HEAD_DIM) * kg_ref[...], jnp.cos(ang), sin)
    vc_ref[0] = v_c


def _nsa_compress(kc_raw, vc_raw, posc, cmp_pe, cmp_w1, cmp_b1, cmp_w2, cmp_b2, kg0):
    b, s, _ = kc_raw.shape
    nc = s // CMP_STRIDE
    kt = kc_raw.reshape(b, nc, CMP_STRIDE * HEAD_DIM)
    vt = vc_raw.reshape(b, nc, CMP_STRIDE * HEAD_DIM)
    full = lambda shape: pl.BlockSpec(shape, lambda bb: (0,) * len(shape))
    per_b = lambda shape: pl.BlockSpec((1,) + shape, lambda bb: (bb, 0, 0))
    return pl.pallas_call(
        functools.partial(_nsa_cmp_kernel, NC=nc),
        grid=(b,),
        in_specs=[per_b((nc, CMP_STRIDE * HEAD_DIM)), per_b((nc, CMP_STRIDE * HEAD_DIM)),
                  full((2, 1, CMP_LEN * HEAD_DIM)), full((2, CMP_LEN * HEAD_DIM, CMP_HIDDEN)),
                  full((2, 1, CMP_HIDDEN)), full((2, CMP_HIDDEN, HEAD_DIM)),
                  full((2, 1, HEAD_DIM)), full((1, HEAD_DIM)), per_b((nc, 128)), full((1, 128))],
        out_specs=[per_b((nc, HEAD_DIM)), per_b((nc, HEAD_DIM))],
        out_shape=[jax.ShapeDtypeStruct((b, nc, HEAD_DIM), F32),
                   jax.ShapeDtypeStruct((b, nc, HEAD_DIM), F32)],
        compiler_params=_cparams(("parallel",)),
        name="nsa_compress",
    )(kt, vt, cmp_pe.reshape(2, 1, CMP_LEN * HEAD_DIM), cmp_w1.astype(BF16),
      cmp_b1.reshape(2, 1, CMP_HIDDEN), cmp_w2.astype(BF16), cmp_b2.reshape(2, 1, HEAD_DIM),
      kg0.reshape(1, HEAD_DIM), posc, _inv_freq_row(HEAD_DIM))


def _nsa_attn_kernel(q_ref, kc_ref, vc_ref, ks_ref, vs_ref, kw_ref, vw_ref, misc_ref, gate_ref,
                     o_ref, *scratch, S, NC, KC, WK):
    Q = Q_BLOCK
    R = N_HEADS
    i = pl.program_id(1)
    n_sel = S // SEL_LEN
    n_cmp = (S - CMP_LEN) // CMP_STRIDE + 1
    top_k = min(SEL_TOPK, n_sel)

    qf = q_ref[0]
    q4 = jnp.concatenate([qf[:, r * 128:(r + 1) * 128] for r in range(R)], axis=0)
    q4b = q4.astype(BF16)

    q_hi, q_lo = _split_bf16(q4)
    k_hi, k_lo = _split_bf16(kc_ref[0])
    s_c = (_dot_nt(jnp.concatenate([q_hi, q_lo], axis=1), jnp.concatenate([k_hi, k_hi], axis=1))
           + _dot_nt(q_hi, k_lo))
    t_c = i * Q + (lax.broadcasted_iota(jnp.int32, (R * Q, NC), 0) & (Q - 1))
    c_ix = lax.broadcasted_iota(jnp.int32, (R * Q, NC), 1)
    valid_c = (c_ix * CMP_STRIDE + (CMP_LEN - 1) <= t_c) & (c_ix < n_cmp)
    s_c = jnp.where(valid_c, s_c, NEG)
    p_c = jnp.where(valid_c, jnp.exp2(s_c - jnp.max(s_c, axis=-1, keepdims=True)), 0.0)
    p_c = p_c / jnp.maximum(jnp.sum(p_c, axis=-1, keepdims=True), 1e-30)
    o_c = _bdot(p_c, vc_ref[0])
    assert NC & (NC - 1) == 0
    c_o = (lax.broadcasted_iota(jnp.int32, (2 * NC, 128), 0) & (NC - 1)) * CMP_STRIDE
    n_o = lax.broadcasted_iota(jnp.int32, (2 * NC, 128), 1) * SEL_LEN
    overlap2 = ((c_o < n_o + SEL_LEN) & (c_o + (CMP_LEN - 1) >= n_o)).astype(BF16)
    imp4 = _dot(jnp.concatenate(_split_bf16(p_c), axis=1), overlap2)
    imp = (imp4[0:Q] + imp4[Q:2 * Q] + imp4[2 * Q:3 * Q] + imp4[3 * Q:4 * Q]).T

    NP = min(128, -(-n_sel // 8) * 8)
    t_q = i * Q + lax.broadcasted_iota(jnp.int32, (NP, Q), 1)
    n_ix = lax.broadcasted_iota(jnp.int32, (NP, Q), 0)
    cur = t_q >> 6
    valid_s = (n_ix * SEL_LEN <= t_q) & (n_ix < n_sel)
    forced = (n_ix == 0) | (n_ix == cur) | (n_ix == cur - 1)
    val = jnp.where(valid_s, imp[:NP], -1.0)
    val = jnp.where(forced & valid_s, FORCE, val)
    val = jnp.where(n_ix < n_sel, val, -2.0)
    sel_t = jnp.zeros((NP, Q), F32)
    n_f = n_ix.astype(F32)
    for _ in range(top_k):
        mx = jnp.max(val, axis=0, keepdims=True)
        first = jnp.min(jnp.where(val == mx, n_f, 1e9), axis=0, keepdims=True)
        hit = n_f == first
        sel_t = jnp.where(hit, 1.0, sel_t)
        val = jnp.where(hit, -3.0, val)
    sel_t = jnp.where(valid_s, sel_t, 0.0)
    if NP < 128:
        sel_t = jnp.concatenate([sel_t, jnp.zeros((128 - NP, Q), F32)], axis=0)
    w0 = pl.multiple_of(jnp.maximum(i * Q - WINDOW, 0), Q)
    hi_w = (i * Q - w0) + lax.broadcasted_iota(jnp.int32, (Q, WK), 0)
    col_w = lax.broadcasted_iota(jnp.int32, (Q, WK), 1)
    band = jnp.where((col_w <= hi_w) & (col_w > hi_w - WINDOW), 0.0, NEG)
    s_w = _dot_nt(q4b, kw_ref[0, pl.ds(w0, WK), :]) + jnp.concatenate([band] * R, axis=0)
    p_w = jnp.exp2(s_w - jnp.max(s_w, axis=-1, keepdims=True))
    o_w = (_dot(p_w.astype(BF16), vw_ref[0, pl.ds(w0, WK), :])
           / jnp.sum(p_w, axis=-1, keepdims=True))

    blk_bias = jnp.where(sel_t.T > 0.5, 0.0, NEG).astype(BF16)
    q_aug = jnp.concatenate([q4b, jnp.concatenate([blk_bias] * R, axis=0)], axis=1)

    def causal_mask(s, first_row, k0):
        t_k = i * Q + ((first_row + lax.broadcasted_iota(jnp.int32, s.shape, 0)) & (Q - 1))
        return jnp.where(k0 + lax.broadcasted_iota(jnp.int32, s.shape, 1) <= t_k, s, NEG)

    n_past = (i * Q) // KC
    o_s = _attend_causal(q_aug, ks_ref, vs_ref, n_past, KC, scratch, causal_mask)

    gates = jax.nn.sigmoid(misc_ref[0])
    gate = gate_ref[0]
    for r in range(R):
        g0 = gates[:, MISC_GL + 3 * r:MISC_GL + 3 * r + 1]
        g1 = gates[:, MISC_GL + 3 * r + 1:MISC_GL + 3 * r + 2]
        g2 = gates[:, MISC_GL + 3 * r + 2:MISC_GL + 3 * r + 3]
        rows = slice(r * Q, (r + 1) * Q)
        o = g0 * o_c[rows] + g1 * o_s[rows] + g2 * o_w[rows]
        o_ref[0, :, r * 128:(r + 1) * 128] = (
            o * _silu(gate[:, r * 128:(r + 1) * 128])).astype(o_ref.dtype)


def _nsa_attn(qf, k_c, v_c, ks, vs, kw, vw, proj):
    b, s, _ = qf.shape
    nc = s // CMP_STRIDE
    kc_len = min(1024, s)
    wk = min(WINDOW + Q_BLOCK, s)
    per_b = lambda n, w: pl.BlockSpec((1, n, w), lambda bb, i: (bb, 0, 0))
    return pl.pallas_call(
        functools.partial(_nsa_attn_kernel, S=s, NC=nc, KC=kc_len, WK=wk),
        grid=(b, s // Q_BLOCK),
        in_specs=[pl.BlockSpec((1, Q_BLOCK, GROUP_W), lambda bb, i: (bb, i, 0)),
                  per_b(nc, 128), per_b(nc, 128),
                  per_b(s, 256), per_b(s, 128), per_b(s, 128), per_b(s, 128),
                  _seg_spec(Q_BLOCK, 128, OFF_MISC), _seg_spec(Q_BLOCK, GROUP_W, OFF_DG)],
        out_specs=pl.BlockSpec((1, Q_BLOCK, GROUP_W), lambda bb, i: (bb, i, 0)),
        out_shape=jax.ShapeDtypeStruct((b, s, GROUP_W), BF16),
        scratch_shapes=_attn_scratch(N_HEADS * Q_BLOCK, kc_len),
        compiler_params=_cparams(("parallel", "arbitrary")),
        name="nsa_attn",
    )(qf, k_c, v_c, ks, vs, kw, vw, proj, proj)


def _reorder_w_in_kernel(w_ref, o_ref):
    o = np.concatenate([[0], np.cumsum(IN_SIZES)]).tolist()
    (a_qkv, a_decay, a_beta, a_gate, b_cq, b_ckv, b_kpe, b_gate, c_x, c_gate,
     d_q, d_kv, d_gl, d_gate) = range(len(IN_SIZES))
    for l in range(w_ref.shape[1]):
        seg = lambda k: w_ref[o[k]:o[k + 1], l, :]
        off = 0
        for k in (a_qkv, a_gate, b_gate, c_x, c_gate, d_q, d_gate, d_kv, b_cq):
            o_ref[l, off:off + IN_SIZES[k], :] = seg(k).astype(BF16)
            off += IN_SIZES[k]
        assert off == OFF_MISC
        zeros = jnp.zeros((128 - MLA_ROPE - 2 * N_HEADS - 3 * N_HEADS, w_ref.shape[2]), F32)
        misc = jnp.concatenate([seg(b_kpe), seg(a_decay), seg(a_beta), seg(d_gl), zeros], axis=0)
        o_ref[l, OFF_MISC:OFF_MISC + 128, :] = misc.astype(BF16)
        o_ref[l, OFF_BCKV:OFF_BCKV + MLA_KV_RANK, :] = seg(b_ckv).astype(BF16)


def _reorder_w_in(w_in):
    depth, d, d_in = w_in.shape
    tc = 128
    w_t = jnp.transpose(w_in, (2, 0, 1))
    return pl.pallas_call(
        _reorder_w_in_kernel,
        grid=(d // tc,),
        in_specs=[pl.BlockSpec((d_in, depth, tc), lambda i: (0, 0, i))],
        out_specs=pl.BlockSpec((depth, D_PROJ, tc), lambda i: (0, 0, i)),
        out_shape=jax.ShapeDtypeStruct((depth, D_PROJ, d), BF16),
        compiler_params=_cparams(("parallel",)),
        name="reorder_w_in",
    )(w_t)


def kernel(x, positions, norm_g, w_in, w_out, gdn_conv_w, gdn_a_log, gdn_dt_bias, gdn_norm_g,
           mla_q_norm_g, mla_w_uq, mla_kv_norm_g, mla_w_ukv, mla_qk_norm_g,
           lru_conv_w, lru_conv_b, lru_w_r, lru_b_r, lru_w_i, lru_b_i, lru_lambda,
           nsa_q_norm_g, nsa_k_norm_g, nsa_cmp_pe, nsa_cmp_w1, nsa_cmp_b1, nsa_cmp_w2, nsa_cmp_b2):
    b, s, d = x.shape
    depth = w_in.shape[0]
    w_in_r = _reorder_w_in(w_in)
    w_out_b = w_out.astype(BF16)
    posf = positions.astype(F32)
    rope = _rope_tables(jnp.broadcast_to(posf[:, :, None], (b, s, 128)))
    nc = s // CMP_STRIDE
    pos_end = jnp.pad(posf.reshape(b, nc, CMP_STRIDE)[:, 1:, CMP_STRIDE - 1], ((0, 0), (0, 1)))
    posc = jnp.broadcast_to(pos_end[:, :, None], (b, nc, 128))

    x2d = x.reshape(b * s, d)
    for l in range(depth):
        proj = _inproj(x2d, norm_g[l].reshape(1, d), w_in_r, l).reshape(b, s, D_PROJ)
        y_a = _gdn(proj, gdn_conv_w[l], gdn_a_log[l], gdn_dt_bias[l], gdn_norm_g[l])
        q_b, k_b, v_b = _mla_prep(proj, rope, mla_q_norm_g[l], mla_w_uq[l], mla_kv_norm_g[l],
                                  mla_w_ukv[l], mla_qk_norm_g[l])
        y_b = _mla_flash(q_b, k_b, v_b, proj)
        y_c = _lru(proj, lru_conv_w[l], lru_conv_b[l], lru_w_r[l], lru_b_r[l], lru_w_i[l],
                   lru_b_i[l], lru_lambda[l])
        q_d, ks, vs, kw, vw, kc_raw, vc_raw = _nsa_prep(proj, rope, nsa_q_norm_g[l],
                                                        nsa_k_norm_g[l])
        k_c, v_c = _nsa_compress(kc_raw, vc_raw, posc, nsa_cmp_pe[l], nsa_cmp_w1[l],
                                 nsa_cmp_b1[l], nsa_cmp_w2[l], nsa_cmp_b2[l], nsa_k_norm_g[l, 0])
        y_d = _nsa_attn(q_d, k_c, v_c, ks, vs, kw, vw, proj)
        ys = [y.reshape(b * s, GROUP_W) for y in (y_a, y_b, y_c, y_d)]
        x2d = _outproj(ys, w_out_b[l], x2d)
    return x2d.reshape(b, s, d)
```

```python
import functools
import math

import numpy as np
import jax
import jax.numpy as jnp
from jax import lax
from jax.experimental import pallas as pl
from jax.experimental.pallas import tpu as pltpu

F32 = jnp.float32
BF16 = jnp.bfloat16
HIGHEST = lax.Precision.HIGHEST

D_MODEL = 2048
GROUP_W = 512
HEAD_DIM = 128
N_HEADS = 4
RMS_EPS = 1e-6
ROPE_THETA = 10000.0
NEG = -1e30
LOG2E = math.log2(math.e)
GDN_CONV = 4
GDN_CHUNK = 128
MLA_Q_RANK = 384
MLA_KV_RANK = 256
MLA_NOPE = 128
MLA_ROPE = 64
LRU_CONV = 4
LRU_C = 8.0
CMP_LEN = 32
CMP_STRIDE = 16
CMP_HIDDEN = 256
SEL_LEN = 64
SEL_TOPK = 16
WINDOW = 512
FORCE = 1e9
Q_BLOCK = 256

IN_SIZES = (3 * GROUP_W, N_HEADS, N_HEADS, GROUP_W,
            MLA_Q_RANK, MLA_KV_RANK, MLA_ROPE, GROUP_W,
            GROUP_W, GROUP_W,
            GROUP_W, 6 * HEAD_DIM, 3 * N_HEADS, GROUP_W)

OFF_AQ, OFF_AK, OFF_AV, OFF_AG = 0, 512, 1024, 1536
OFF_BG, OFF_CX, OFF_CG, OFF_DQ, OFF_DG = 2048, 2560, 3072, 3584, 4096
OFF_DKV, OFF_BCQ, OFF_MISC, OFF_BCKV = 4608, 5376, 5760, 5888
D_PROJ = 6144
MISC_KPE, MISC_DECAY, MISC_BETA, MISC_GL = 0, 64, 68, 72

VMEM_LIMIT = 56 * 1024 * 1024


def _cparams(sem):
    return pltpu.CompilerParams(dimension_semantics=sem, vmem_limit_bytes=VMEM_LIMIT)


def _dot(a, b, precision=None):
    return lax.dot_general(a, b, (((1,), (0,)), ((), ())), precision=precision,
                           preferred_element_type=F32)


def _dot_nt(a, b, precision=None):
    return lax.dot_general(a, b, (((1,), (1,)), ((), ())), precision=precision,
                           preferred_element_type=F32)


def _bdot(a, b):
    return _dot(a.astype(BF16), b.astype(BF16))


def _bdot_nt(a, b):
    return _dot_nt(a.astype(BF16), b.astype(BF16))


def _split_bf16(x):
    hi = x.astype(BF16)
    return hi, (x - hi.astype(F32)).astype(BF16)


def _silu(x):
    h = 0.5 * x
    return h + h * jnp.tanh(h)


def _softplus(x):
    return jnp.maximum(x, 0.0) + jnp.log1p(jnp.exp(-jnp.abs(x)))


def _rms(x, n):
    return x * lax.rsqrt(jnp.sum(x * x, axis=-1, keepdims=True) * (1.0 / n) + RMS_EPS)


def _rms_mxu(x, n):
    w = x.shape[1]
    ssq = _dot((x * x).astype(BF16), jnp.ones((w, 128), BF16))
    r = lax.rsqrt(ssq * (1.0 / n) + RMS_EPS)
    if w == 128:
        return x * r
    return jnp.concatenate([x[:, j * 128:(j + 1) * 128] * r for j in range(w // 128)], axis=1)


def _sigmoid(x):
    return 0.5 + 0.5 * jnp.tanh(0.5 * x)


def _inproj_kernel(x_ref, g_ref, w_ref, o_ref, h_ref, *, tm):
    @pl.when(pl.program_id(1) == 0)
    def _():
        rows = min(256, tm)
        for r in range(tm // rows):
            x = x_ref[r * rows:(r + 1) * rows, :]
            h_ref[r * rows:(r + 1) * rows, :] = (_rms(x, D_MODEL) * g_ref[...]).astype(BF16)

    o_ref[...] = _dot_nt(h_ref[...], w_ref[...])


def _inproj(x2d, g, w_t, layer):
    m = x2d.shape[0]
    tm = min(1024, m)
    tn = 1024
    return pl.pallas_call(
        functools.partial(_inproj_kernel, tm=tm),
        grid=(m // tm, D_PROJ // tn),
        in_specs=[pl.BlockSpec((tm, D_MODEL), lambda i, j: (i, 0)),
                  pl.BlockSpec((1, D_MODEL), lambda i, j: (0, 0)),
                  pl.BlockSpec((None, tn, D_MODEL), lambda i, j: (layer, j, 0))],
        out_specs=pl.BlockSpec((tm, tn), lambda i, j: (i, j)),
        out_shape=jax.ShapeDtypeStruct((m, D_PROJ), F32),
        scratch_shapes=[pltpu.VMEM((tm, D_MODEL), BF16)],
        compiler_params=_cparams(("parallel", "arbitrary")),
        name="inproj",
    )(x2d, g, w_t)


def _outproj_kernel(ya_ref, yb_ref, yc_ref, yd_ref, wa_ref, wb_ref, wc_ref, wd_ref, x_ref, o_ref):
    acc = x_ref[...]
    for y_ref, w_ref in ((ya_ref, wa_ref), (yb_ref, wb_ref), (yc_ref, wc_ref), (yd_ref, wd_ref)):
        acc = acc + _dot(y_ref[...].astype(BF16), w_ref[...])
    o_ref[...] = acc


def _outproj(ys, w, x2d):
    m = x2d.shape[0]
    tm = min(1024, m)
    tn = 1024
    y_specs = [pl.BlockSpec((tm, GROUP_W), lambda i, j: (i, 0)) for _ in range(4)]
    w_specs = [pl.BlockSpec((GROUP_W, tn), functools.partial(lambda i, j, g: (g, j), g=g))
               for g in range(4)]
    return pl.pallas_call(
        _outproj_kernel,
        grid=(m // tm, D_MODEL // tn),
        in_specs=y_specs + w_specs + [pl.BlockSpec((tm, tn), lambda i, j: (i, j))],
        out_specs=pl.BlockSpec((tm, tn), lambda i, j: (i, j)),
        out_shape=jax.ShapeDtypeStruct((m, D_MODEL), F32),
        compiler_params=_cparams(("parallel", "arbitrary")),
        name="outproj",
    )(*ys, w, w, w, w, x2d)


def _causal_conv4(xin, ext_ref, w):
    t = xin.shape[0]
    ext_ref[8:8 + t, :] = xin
    ext = ext_ref[...]
    acc = ext * w[0:1]
    for j in range(1, 4):
        acc = pltpu.roll(acc, 1, 0) + ext * w[j:j + 1]
    ext_ref[0:8, :] = xin[t - 8:t]
    return acc[8:8 + t]


def _gdn_kernel(q_ref, k_ref, v_ref, gate_ref, misc_ref, cw_ref, alog_ref, dtb_ref, ng_ref,
                o_ref, hist_ref, state_ref, *, T):
    C = GDN_CHUNK

    @pl.when(pl.program_id(1) == 0)
    def _():
        hist_ref[:, 0:8, :] = jnp.zeros((3, 8, GROUP_W), F32)
        state_ref[...] = jnp.zeros_like(state_ref)

    cw = cw_ref[...]
    conv = []
    for idx, ref in enumerate((q_ref, k_ref, v_ref)):
        xin = ref[0]
        acc = _causal_conv4(xin, hist_ref.at[idx], cw[:, idx * GROUP_W:(idx + 1) * GROUP_W])
        conv.append(_silu(acc))
    qc, kc, vc = conv

    misc = misc_ref[0]
    gfull = -jnp.exp(alog_ref[...]) * _softplus(misc + dtb_ref[...])
    betafull = jax.nn.sigmoid(misc)
    gate = gate_ref[0]

    row = lax.broadcasted_iota(jnp.int32, (C, C), 0)
    col = lax.broadcasted_iota(jnp.int32, (C, C), 1)
    lower = row >= col
    strict = row > col
    ltri = lower.astype(F32)
    row2 = lax.broadcasted_iota(jnp.int32, (C, 2 * C), 0)
    lane2 = lax.broadcasted_iota(jnp.int32, (C, 2 * C), 1)
    left = lane2 < C
    col2 = lane2 & (C - 1)
    leaf_bits = 4
    same_blk = {bits: (row2 >> bits) == (col2 >> bits) for bits in range(leaf_bits, 8)}
    assert C == HEAD_DIM == 1 << 7

    def pairdot(x, y):
        yb = y.astype(BF16)
        zero = jnp.zeros_like(yb)
        bd = jnp.concatenate([jnp.where(left, yb, zero), jnp.where(left, zero, yb)], axis=0)
        return _dot(x.astype(BF16), bd)

    n_chunks = T // C
    hds = []
    for c in range(n_chunks):
        r0 = c * C
        gall = _dot(ltri, gfull[r0:r0 + C], HIGHEST)
        hd = []
        hds.append(hd)
        for h in range(N_HEADS):
            l0 = h * HEAD_DIM
            q = qc[r0:r0 + C, l0:l0 + HEAD_DIM]
            k = kc[r0:r0 + C, l0:l0 + HEAD_DIM]
            v = vc[r0:r0 + C, l0:l0 + HEAD_DIM]
            ones = jnp.ones((HEAD_DIM, 128), BF16)
            q = q * (lax.rsqrt(_dot((q * q).astype(BF16), ones) + RMS_EPS) * (HEAD_DIM ** -0.5))
            k = k * lax.rsqrt(_dot((k * k).astype(BF16), ones) + RMS_EPS)
            beta = betafull[r0:r0 + C, MISC_BETA + h:MISC_BETA + h + 1]
            gc = jnp.broadcast_to(gall[:, MISC_DECAY + h:MISC_DECAY + h + 1], (C, C))
            decay = jnp.exp(jnp.where(lower, gc - gc.T, NEG))
            kb = k * beta
            a = jnp.where(strict, _bdot_nt(kb, k) * decay, 0.0)
            hd.append((q, k, v, beta, gc, decay, kb, a))

    a2s = [jnp.concatenate([hds[c][h0][7], hds[c][h0 + 1][7]], axis=1)
           for c in range(n_chunks) for h0 in range(0, N_HEADS, 2)]
    ps = [jnp.where(same_blk[leaf_bits], -a2, 0.0) for a2 in a2s]
    tm2 = list(ps)
    for _ in range(leaf_bits - 1):
        ps = [pairdot(p, p) for p in ps]
        tm2 = [tm + p + pairdot(tm, p) for tm, p in zip(tm2, ps)]
    for bits in range(leaf_bits + 1, 8):
        join = same_blk[bits] & jnp.logical_not(same_blk[bits - 1])
        tas = [jnp.where(join, a2, 0.0) for a2 in a2s]
        tas = [off + pairdot(tm, off) for tm, off in zip(tm2, tas)]
        tm2 = [tm - (ta + pairdot(ta, tm)) for tm, ta in zip(tm2, tas)]

    for c in range(n_chunks):
        r0 = c * C
        hd = hds[c]
        tms = []
        for tm in tm2[2 * c:2 * c + 2]:
            tms += [tm[:, :C], tm[:, C:]]
        for h in range(N_HEADS):
            l0 = h * HEAD_DIM
            q, k, v, beta, gc, decay, kb, _ = hd[h]
            eg = jnp.exp(gc)
            rhs = jnp.concatenate([v * beta, kb * eg], axis=-1)
            uw = rhs + _bdot(tms[h], rhs)
            u = uw[:, :HEAD_DIM]
            w = uw[:, HEAD_DIM:]
            qk = _bdot_nt(q, k) * decay
            q_dec = q * eg
            g_last = gc[C - 1:C, :]
            k_end = k * jnp.exp(g_last - gc)
            c_dec = jnp.exp(g_last)
            state = state_ref[h]
            v_new = u - _bdot(w, state)
            o = _bdot(q_dec, state) + _bdot(qk, v_new)
            state_ref[h] = state * c_dec + lax.dot_general(
                k_end.astype(BF16), v_new.astype(BF16), (((0,), (0,)), ((), ())),
                preferred_element_type=F32)
            o = _rms(o, HEAD_DIM) * ng_ref[...]
            o_ref[0, r0:r0 + C, l0:l0 + HEAD_DIM] = (
                o * _silu(gate[r0:r0 + C, l0:l0 + HEAD_DIM])).astype(o_ref.dtype)


def _seg_spec(t, width, off):
    blk = off // width
    assert blk * width == off
    return pl.BlockSpec((1, t, width), lambda b, i: (b, i, blk))


def _gdn(proj, conv_w, a_log, dt_bias, norm_g):
    b, s, _ = proj.shape
    t = min(512, s)
    pad = lambda vec, off: jnp.zeros((1, 128), F32).at[0, off:off + N_HEADS].set(vec)
    full = lambda shape: pl.BlockSpec(shape, lambda bb, i: (0,) * len(shape))
    return pl.pallas_call(
        functools.partial(_gdn_kernel, T=t),
        grid=(b, s // t),
        in_specs=[_seg_spec(t, GROUP_W, OFF_AQ), _seg_spec(t, GROUP_W, OFF_AK),
                  _seg_spec(t, GROUP_W, OFF_AV), _seg_spec(t, GROUP_W, OFF_AG),
                  _seg_spec(t, 128, OFF_MISC),
                  full((GDN_CONV, 3 * GROUP_W)), full((1, 128)), full((1, 128)), full((1, 128))],
        out_specs=pl.BlockSpec((1, t, GROUP_W), lambda bb, i: (bb, i, 0)),
        out_shape=jax.ShapeDtypeStruct((b, s, GROUP_W), BF16),
        scratch_shapes=[pltpu.VMEM((3, t + 8, GROUP_W), F32),
                        pltpu.VMEM((N_HEADS, HEAD_DIM, HEAD_DIM), F32)],
        compiler_params=_cparams(("parallel", "arbitrary")),
        name="gdn",
    )(proj, proj, proj, proj, proj, conv_w, pad(a_log, MISC_DECAY), pad(dt_bias, MISC_DECAY),
      norm_g.reshape(1, HEAD_DIM))


def _lru_kernel(x_ref, gate_ref, cw_ref, cb_ref, wr_ref, br_ref, wi_ref, bi_ref, lam_ref,
                o_ref, hist_ref, h_ref, *, T):
    @pl.when(pl.program_id(1) == 0)
    def _():
        hist_ref[0:8, :] = jnp.zeros((8, GROUP_W), F32)
        h_ref[...] = jnp.zeros_like(h_ref)

    xin = x_ref[0]
    xc = _causal_conv4(xin, hist_ref, cw_ref[...]) + cb_ref[...]
    rs, igs = [], []
    for h in range(N_HEADS):
        xh = xc[:, h * HEAD_DIM:(h + 1) * HEAD_DIM].astype(BF16)
        rs.append(_dot(xh, wr_ref[h]))
        igs.append(_dot(xh, wi_ref[h]))
    r = _sigmoid(jnp.concatenate(rs, axis=-1) + br_ref[...])
    ig = _sigmoid(jnp.concatenate(igs, axis=-1) + bi_ref[...])
    log_a = -LRU_C * r * _softplus(-lam_ref[...])
    a = jnp.exp(log_a)
    bb = jnp.sqrt(-jnp.tanh(log_a) * (a * a + 1.0)) * (ig * xc)
    a3 = a.reshape(T // 8, 8, GROUP_W)
    b3 = bb.reshape(T // 8, 8, GROUP_W)
    sub = lax.broadcasted_iota(jnp.int32, (T // 8, 8, GROUP_W), 1)
    for d in (1, 2, 4):
        keep = sub >= d
        a_sh = jnp.where(keep, pltpu.roll(a3, d, 1), 1.0)
        b_sh = jnp.where(keep, pltpu.roll(b3, d, 1), 0.0)
        b3 = a3 * b_sh + b3
        a3 = a3 * a_sh
    h = h_ref[...]
    groups = []
    for g in range(T // 8):
        hg = b3[g] + a3[g] * h
        h = hg[7:8]
        groups.append(hg)
    h_ref[...] = h
    o_ref[0] = (jnp.concatenate(groups, axis=0) * _silu(gate_ref[0])).astype(o_ref.dtype)


def _lru(proj, conv_w, conv_b, w_r, b_r, w_i, b_i, lam):
    b, s, _ = proj.shape
    t = min(512, s)
    full = lambda shape: pl.BlockSpec(shape, lambda bb, i: (0,) * len(shape))
    vec = lambda v: v.reshape(1, GROUP_W)
    return pl.pallas_call(
        functools.partial(_lru_kernel, T=t),
        grid=(b, s // t),
        in_specs=[_seg_spec(t, GROUP_W, OFF_CX), _seg_spec(t, GROUP_W, OFF_CG),
                  full((LRU_CONV, GROUP_W)), full((1, GROUP_W)),
                  full((N_HEADS, HEAD_DIM, HEAD_DIM)), full((1, GROUP_W)),
                  full((N_HEADS, HEAD_DIM, HEAD_DIM)), full((1, GROUP_W)), full((1, GROUP_W))],
        out_specs=pl.BlockSpec((1, t, GROUP_W), lambda bb, i: (bb, i, 0)),
        out_shape=jax.ShapeDtypeStruct((b, s, GROUP_W), BF16),
        scratch_shapes=[pltpu.VMEM((t + 8, GROUP_W), F32), pltpu.VMEM((1, GROUP_W), F32)],
        compiler_params=_cparams(("parallel", "arbitrary")),
        name="rglru",
    )(proj, proj, conv_w, vec(conv_b), w_r.astype(BF16), vec(b_r), w_i.astype(BF16), vec(b_i),
      vec(lam))


def _inv_freq_row(d):
    i = np.arange(128) % (d // 2)
    return jnp.asarray((ROPE_THETA ** (-(2.0 * i) / d)).astype(np.float32).reshape(1, 128))


def _rope128(x, cos, sin_signed):
    return x * cos + pltpu.roll(x, 64, 1) * sin_signed


def _rope64(x, cos, sin_masked, lane):
    rot = jnp.where(lane < 32, -pltpu.roll(x, 96, 1), pltpu.roll(x, 32, 1))
    return x * cos + rot * sin_masked


def _rope_tables_kernel(pos_ref, invf_ref, o_ref):
    lane = lax.broadcasted_iota(jnp.int32, pos_ref.shape[1:], 1)
    ang = pos_ref[0] * invf_ref[...]
    c = jnp.cos(ang)
    s = jnp.sin(ang)
    c32, c96 = pltpu.roll(c, 32, 1), pltpu.roll(c, 96, 1)
    s32, s96 = pltpu.roll(s, 32, 1), pltpu.roll(s, 96, 1)
    o_ref[0, :, 0:128] = jnp.where(lane < 32, c, c32)
    o_ref[0, :, 128:256] = jnp.where(lane < 32, s, jnp.where(lane < MLA_ROPE, s32, 0.0))
    o_ref[0, :, 256:384] = jnp.where(lane < 64, c96, c32)
    o_ref[0, :, 384:512] = jnp.where(lane < 64, -s96, s32)


def _rope_tables(posb):
    b, s, _ = posb.shape
    t = min(512, s)
    full = lambda shape: pl.BlockSpec(shape, lambda bb, i: (0,) * len(shape))
    return pl.pallas_call(
        _rope_tables_kernel,
        grid=(b, s // t),
        in_specs=[pl.BlockSpec((1, t, 128), lambda bb, i: (bb, i, 0)), full((1, 128))],
        out_specs=pl.BlockSpec((1, t, 512), lambda bb, i: (bb, i, 0)),
        out_shape=jax.ShapeDtypeStruct((b, s, 512), F32),
        compiler_params=_cparams(("parallel", "parallel")),
        name="rope_tables",
    )(posb, jnp.concatenate([_inv_freq_row(MLA_ROPE)[:, :32], _inv_freq_row(HEAD_DIM)[:, :64],
                             jnp.zeros((1, 32), F32)], axis=1))


def _mla_prep_kernel(cq_ref, ckv_ref, misc_ref, rope_ref, qng_ref, wuq_ref, kvng_ref, wukv_ref,
                     qgn_ref, qgp_ref, kgn_ref, kgp_ref,
                     q_ref, k_ref, v_ref, *, T):
    scale = (MLA_NOPE + MLA_ROPE) ** -0.5 * LOG2E
    lane = lax.broadcasted_iota(jnp.int32, (T, 128), 1)
    cos = rope_ref[0, :, 0:128]
    sin = rope_ref[0, :, 128:256]

    cq = _rms(cq_ref[0], MLA_Q_RANK) * qng_ref[...]
    qf = _dot(cq.astype(BF16), wuq_ref[...])
    ckv = _rms(ckv_ref[0], MLA_KV_RANK) * kvng_ref[...]
    kvf = _dot(ckv.astype(BF16), wukv_ref[...])
    kpe = jnp.where(lane < MLA_ROPE, misc_ref[0], 0.0)
    kpe = _rope64(_rms(kpe, MLA_ROPE) * kgp_ref[...], cos, sin, lane)
    for h in range(N_HEADS):
        o = h * 256
        qn = _rms_mxu(qf[:, o:o + 128], MLA_NOPE) * qgn_ref[...]
        qp = _rope64(_rms_mxu(qf[:, o + 128:o + 256], MLA_ROPE) * qgp_ref[...], cos, sin, lane)
        q_ref[0, :, o:o + 128] = (qn * scale).astype(BF16)
        q_ref[0, :, o + 128:o + 256] = (qp * scale).astype(BF16)
        kn = _rms(kvf[:, o:o + 128], MLA_NOPE) * kgn_ref[...]
        k_ref[0, :, o:o + 128] = kn.astype(BF16)
        k_ref[0, :, o + 128:o + 256] = kpe.astype(BF16)
        v_ref[0, :, h * 128:(h + 1) * 128] = kvf[:, o + 128:o + 256].astype(BF16)


def _mla_prep(proj, rope, q_norm_g, w_uq, kv_norm_g, w_ukv, qk_norm_g):
    b, s, _ = proj.shape
    t = min(512, s)
    wuq = jnp.pad(w_uq, ((0, 0), (0, 0), (0, 64))).reshape(MLA_Q_RANK, N_HEADS * 256).astype(BF16)
    wukv = w_ukv.reshape(MLA_KV_RANK, N_HEADS * 256).astype(BF16)
    pad64 = lambda v: jnp.pad(v, (0, 64)).reshape(1, 128)
    full = lambda shape: pl.BlockSpec(shape, lambda bb, i: (0,) * len(shape))
    tok = lambda w: pl.BlockSpec((1, t, w), lambda bb, i: (bb, i, 0))
    return pl.pallas_call(
        functools.partial(_mla_prep_kernel, T=t),
        grid=(b, s // t),
        in_specs=[_seg_spec(t, MLA_Q_RANK, OFF_BCQ), _seg_spec(t, MLA_KV_RANK, OFF_BCKV),
                  _seg_spec(t, 128, OFF_MISC), pl.BlockSpec((1, t, 256), lambda bb, i: (bb, i, 0)),
                  full((1, MLA_Q_RANK)), full((MLA_Q_RANK, 1024)),
                  full((1, MLA_KV_RANK)), full((MLA_KV_RANK, 1024)),
                  full((1, 128)), full((1, 128)), full((1, 128)), full((1, 128))],
        out_specs=[tok(1024), tok(1024), tok(512)],
        out_shape=[jax.ShapeDtypeStruct((b, s, 1024), BF16),
                   jax.ShapeDtypeStruct((b, s, 1024), BF16),
                   jax.ShapeDtypeStruct((b, s, 512), BF16)],
        compiler_params=_cparams(("parallel", "parallel")),
        name="mla_prep",
    )(proj, proj, proj, rope, q_norm_g.reshape(1, -1), wuq, kv_norm_g.reshape(1, -1), wukv,
      qk_norm_g[0, :MLA_NOPE].reshape(1, 128), pad64(qk_norm_g[0, MLA_NOPE:]),
      qk_norm_g[1, :MLA_NOPE].reshape(1, 128), pad64(qk_norm_g[1, MLA_NOPE:]))


ROW_GROUPS = 2


def _attn_scratch(m, kc):
    return [pltpu.VMEM((m, 1), F32), pltpu.VMEM((m, 128), F32), pltpu.VMEM((m, 128), F32),
            pltpu.VMEM((m, kc), F32), pltpu.VMEM((m, kc), F32)]


def _attend_causal(q, k_ref, v_ref, n_past, kc, scratch, mask_fn):
    m_ref, l_ref, acc_ref, sa_ref, sb_ref = scratch
    m_ref[...] = jnp.full_like(m_ref, NEG)
    l_ref[...] = jnp.zeros_like(l_ref)
    acc_ref[...] = jnp.zeros_like(acc_ref)
    g_rows = q.shape[0] // ROW_GROUPS

    def scores(c, s_ref):
        k_blk = k_ref[0, pl.ds(pl.multiple_of(c * kc, kc), kc), :]
        for g in range(ROW_GROUPS):
            rows = slice(g * g_rows, (g + 1) * g_rows)
            s_ref[rows, :] = _dot_nt(q[rows], k_blk)

    def consume(c, s_ref, masked):
        k0 = pl.multiple_of(c * kc, kc)
        v_blk = v_ref[0, pl.ds(k0, kc), :]
        for g in range(ROW_GROUPS):
            rows = slice(g * g_rows, (g + 1) * g_rows)
            s = s_ref[rows, :]
            if masked:
                s = mask_fn(s, g * g_rows, k0)
            m_old = m_ref[rows, :]
            m_new = jnp.maximum(m_old, jnp.max(s, axis=-1, keepdims=True))
            alpha = jnp.exp2(m_old - m_new)
            p = jnp.exp2(s - m_new)
            part = p[:, 0:128]
            for j in range(1, kc // 128):
                part = part + p[:, j * 128:(j + 1) * 128]
            l_ref[rows, :] = alpha * l_ref[rows, :] + part
            acc_ref[rows, :] = alpha * acc_ref[rows, :] + _dot(p.astype(BF16), v_blk)
            m_ref[rows, :] = m_new

    def past_pair(j, carry):
        c = 2 * j
        scores(c + 1, sb_ref)
        consume(c, sa_ref, False)
        scores(c + 2, sa_ref)
        consume(c + 1, sb_ref, False)
        return carry

    scores(0, sa_ref)
    lax.fori_loop(0, n_past // 2, past_pair, 0)

    @pl.when(n_past % 2 == 0)
    def _():
        consume(n_past, sa_ref, True)

    @pl.when(n_past % 2 == 1)
    def _():
        scores(n_past, sb_ref)
        consume(n_past - 1, sa_ref, False)
        consume(n_past, sb_ref, True)

    return acc_ref[...] / jnp.sum(l_ref[...], axis=-1, keepdims=True)


def _mla_flash_kernel(q_ref, k_ref, v_ref, gate_ref, o_ref, *scratch, tq, kc):
    i = pl.program_id(2)
    q = q_ref[0]

    def causal(s, first_row, k0):
        row = lax.broadcasted_iota(jnp.int32, s.shape, 0) + (i * tq + first_row)
        col = lax.broadcasted_iota(jnp.int32, s.shape, 1) + k0
        return jnp.where(col <= row, s, NEG)

    n_full = (i * tq) // kc
    o = _attend_causal(q, k_ref, v_ref, n_full, kc, scratch, causal)
    o_ref[0] = (o * _silu(gate_ref[0])).astype(o_ref.dtype)


def _mla_flash(q, k, v, proj):
    b, s, _ = q.shape
    tq = min(1024, s)
    kc = min(1024, s)
    gate_blk = OFF_BG // 128
    return pl.pallas_call(
        functools.partial(_mla_flash_kernel, tq=tq, kc=kc),
        grid=(b, N_HEADS, s // tq),
        in_specs=[pl.BlockSpec((1, tq, 256), lambda bb, h, i: (bb, i, h)),
                  pl.BlockSpec((1, s, 256), lambda bb, h, i: (bb, 0, h)),
                  pl.BlockSpec((1, s, 128), lambda bb, h, i: (bb, 0, h)),
                  pl.BlockSpec((1, tq, 128), lambda bb, h, i: (bb, i, gate_blk + h))],
        out_specs=pl.BlockSpec((1, tq, 128), lambda bb, h, i: (bb, i, h)),
        out_shape=jax.ShapeDtypeStruct((b, s, GROUP_W), BF16),
        scratch_shapes=_attn_scratch(tq, kc),
        compiler_params=_cparams(("parallel", "parallel", "arbitrary")),
        name="mla_flash",
    )(q, k, v, proj)


def _nsa_prep_kernel(dq_ref, dkv_ref, rope_ref, qg_ref, kg_ref,
                     q_ref, ks_ref, vs_ref, kw_ref, vw_ref, kc_ref, vc_ref, *, T):
    scale = HEAD_DIM ** -0.5 * LOG2E
    lane = lax.broadcasted_iota(jnp.int32, (T, 128), 1)
    cos = rope_ref[0, :, 0:128]
    sin = rope_ref[0, :, 128:256]
    dq = dq_ref[0]
    for h in range(N_HEADS):
        qh = _rms_mxu(dq[:, h * 128:(h + 1) * 128], HEAD_DIM) * qg_ref[...]
        q_ref[0, :, h * 128:(h + 1) * 128] = _rope128(qh, cos, sin) * scale
    kv = dkv_ref[0]
    kc_ref[0] = kv[:, 0:128]
    vc_ref[0] = kv[:, 128:256]
    ks_ref[0, :, 0:128] = _rope128(_rms_mxu(kv[:, 256:384], HEAD_DIM) * kg_ref[1:2], cos,
                                   sin).astype(BF16)
    key_blk = (pl.program_id(1) * T + lax.broadcasted_iota(jnp.int32, (T, 128), 0)) >> 6
    ks_ref[0, :, 128:256] = (lane == key_blk).astype(BF16)
    vs_ref[0] = kv[:, 384:512].astype(BF16)
    kw_ref[0] = _rope128(_rms_mxu(kv[:, 512:640], HEAD_DIM) * kg_ref[2:3], cos, sin).astype(BF16)
    vw_ref[0] = kv[:, 640:768].astype(BF16)


def _nsa_prep(proj, rope, q_norm_g, k_norm_g):
    b, s, _ = proj.shape
    t = min(512, s)
    full = lambda shape: pl.BlockSpec(shape, lambda bb, i: (0,) * len(shape))
    tok = lambda w: pl.BlockSpec((1, t, w), lambda bb, i: (bb, i, 0))
    sds = lambda w, dt: jax.ShapeDtypeStruct((b, s, w), dt)
    return pl.pallas_call(
        functools.partial(_nsa_prep_kernel, T=t),
        grid=(b, s // t),
        in_specs=[_seg_spec(t, GROUP_W, OFF_DQ), _seg_spec(t, 768, OFF_DKV),
                  pl.BlockSpec((1, t, 256), lambda bb, i: (bb, i, 1)),
                  full((1, 128)), full((3, 128))],
        out_specs=[tok(512), tok(256)] + [tok(128)] * 5,
        out_shape=[sds(512, F32), sds(256, BF16), sds(128, BF16), sds(128, BF16), sds(128, BF16),
                   sds(128, F32), sds(128, F32)],
        compiler_params=_cparams(("parallel", "parallel")),
        name="nsa_prep",
    )(proj, proj, rope, q_norm_g.reshape(1, HEAD_DIM), k_norm_g)


def _nsa_cmp_kernel(kt_ref, vt_ref, pe_ref, w1_ref, b1_ref, w2_ref, b2_ref, kg_ref, pos_ref,
                    invf_ref, kc_ref, vc_ref, *, NC):
    half = CMP_STRIDE * HEAD_DIM
    outs = []
    for j, t_ref in enumerate((kt_ref, vt_ref)):
        t2 = t_ref[0].astype(BF16)
        first = _dot(t2, w1_ref[j, :half, :])
        second = pltpu.roll(_dot(t2, w1_ref[j, half:, :]), NC - 1, 0)
        pe8 = jnp.broadcast_to(pe_ref[j], (8, CMP_LEN * HEAD_DIM)).astype(BF16)
        bias = _dot(pe8, w1_ref[j])[0:1] + b1_ref[j]
        hid = _silu(first + second + bias)
        outs.append(_dot(hid.astype(BF16), w2_ref[j]) + b2_ref[j])
    k_c, v_c = outs
    lane = lax.broadcasted_iota(jnp.int32, (NC, 128), 1)
    ang = pos_ref[0] * invf_ref[...]
    sin = jnp.sin(ang)
    sin = jnp.where(lane < 64, -sin, sin)
    kc_ref[0] = _rope128(_rms(k_c, HEAD_DIM) * kg_ref[...], jnp.cos(ang), sin)
    vc_ref[0] = v_c


def _nsa_compress(kc_raw, vc_raw, posc, cmp_pe, cmp_w1, cmp_b1, cmp_w2, cmp_b2, kg0):
    b, s, _ = kc_raw.shape
    nc = s // CMP_STRIDE
    kt = kc_raw.reshape(b, nc, CMP_STRIDE * HEAD_DIM)
    vt = vc_raw.reshape(b, nc, CMP_STRIDE * HEAD_DIM)
    full = lambda shape: pl.BlockSpec(shape, lambda bb: (0,) * len(shape))
    per_b = lambda shape: pl.BlockSpec((1,) + shape, lambda bb: (bb, 0, 0))
    return pl.pallas_call(
        functools.partial(_nsa_cmp_kernel, NC=nc),
        grid=(b,),
        in_specs=[per_b((nc, CMP_STRIDE * HEAD_DIM)), per_b((nc, CMP_STRIDE * HEAD_DIM)),
                  full((2, 1, CMP_LEN * HEAD_DIM)), full((2, CMP_LEN * HEAD_DIM, CMP_HIDDEN)),
                  full((2, 1, CMP_HIDDEN)), full((2, CMP_HIDDEN, HEAD_DIM)),
                  full((2, 1, HEAD_DIM)), full((1, HEAD_DIM)), per_b((nc, 128)), full((1, 128))],
        out_specs=[per_b((nc, HEAD_DIM)), per_b((nc, HEAD_DIM))],
        out_shape=[jax.ShapeDtypeStruct((b, nc, HEAD_DIM), F32),
                   jax.ShapeDtypeStruct((b, nc, HEAD_DIM), F32)],
        compiler_params=_cparams(("parallel",)),
        name="nsa_compress",
    )(kt, vt, cmp_pe.reshape(2, 1, CMP_LEN * HEAD_DIM), cmp_w1.astype(BF16),
      cmp_b1.reshape(2, 1, CMP_HIDDEN), cmp_w2.astype(BF16), cmp_b2.reshape(2, 1, HEAD_DIM),
      kg0.reshape(1, HEAD_DIM), posc, _inv_freq_row(HEAD_DIM))


def _nsa_attn_kernel(q_ref, kc_ref, vc_ref, ks_ref, vs_ref, kw_ref, vw_ref, misc_ref, gate_ref,
                     o_ref, *scratch, S, NC, KC, WK):
    Q = Q_BLOCK
    R = N_HEADS
    i = pl.program_id(1)
    n_sel = S // SEL_LEN
    n_cmp = (S - CMP_LEN) // CMP_STRIDE + 1
    top_k = min(SEL_TOPK, n_sel)

    qf = q_ref[0]
    q4 = jnp.concatenate([qf[:, r * 128:(r + 1) * 128] for r in range(R)], axis=0)
    q4b = q4.astype(BF16)

    q_hi, q_lo = _split_bf16(q4)
    k_hi, k_lo = _split_bf16(kc_ref[0])
    s_c = (_dot_nt(jnp.concatenate([q_hi, q_lo], axis=1), jnp.concatenate([k_hi, k_hi], axis=1))
           + _dot_nt(q_hi, k_lo))
    t_c = i * Q + (lax.broadcasted_iota(jnp.int32, (R * Q, NC), 0) & (Q - 1))
    c_ix = lax.broadcasted_iota(jnp.int32, (R * Q, NC), 1)
    valid_c = (c_ix * CMP_STRIDE + (CMP_LEN - 1) <= t_c) & (c_ix < n_cmp)
    s_c = jnp.where(valid_c, s_c, NEG)
    p_c = jnp.where(valid_c, jnp.exp2(s_c - jnp.max(s_c, axis=-1, keepdims=True)), 0.0)
    p_c = p_c / jnp.maximum(jnp.sum(p_c, axis=-1, keepdims=True), 1e-30)
    o_c = _bdot(p_c, vc_ref[0])
    assert NC & (NC - 1) == 0
    c_o = (lax.broadcasted_iota(jnp.int32, (2 * NC, 128), 0) & (NC - 1)) * CMP_STRIDE
    n_o = lax.broadcasted_iota(jnp.int32, (2 * NC, 128), 1) * SEL_LEN
    overlap2 = ((c_o < n_o + SEL_LEN) & (c_o + (CMP_LEN - 1) >= n_o)).astype(BF16)
    imp4 = _dot(jnp.concatenate(_split_bf16(p_c), axis=1), overlap2)
    imp = (imp4[0:Q] + imp4[Q:2 * Q] + imp4[2 * Q:3 * Q] + imp4[3 * Q:4 * Q]).T

    NP = min(128, -(-n_sel // 8) * 8)
    t_q = i * Q + lax.broadcasted_iota(jnp.int32, (NP, Q), 1)
    n_ix = lax.broadcasted_iota(jnp.int32, (NP, Q), 0)
    cur = t_q >> 6
    valid_s = (n_ix * SEL_LEN <= t_q) & (n_ix < n_sel)
    forced = (n_ix == 0) | (n_ix == cur) | (n_ix == cur - 1)
    val = jnp.where(valid_s, imp[:NP], -1.0)
    val = jnp.where(forced & valid_s, -3.0, val)
    val = jnp.where(n_ix < n_sel, val, -2.0)
    sel_t = jnp.where(forced & valid_s, 1.0, 0.0)
    n_f = n_ix.astype(F32)
    for _ in range(max(top_k - 3, 0)):
        mx = jnp.max(val, axis=0, keepdims=True)
        first = jnp.min(jnp.where(val == mx, n_f, 1e9), axis=0, keepdims=True)
        hit = n_f == first
        sel_t = jnp.where(hit, 1.0, sel_t)
        val = jnp.where(hit, -3.0, val)
    sel_t = jnp.where(valid_s, sel_t, 0.0)
    if NP < 128:
        sel_t = jnp.concatenate([sel_t, jnp.zeros((128 - NP, Q), F32)], axis=0)
    w0 = pl.multiple_of(jnp.maximum(i * Q - WINDOW, 0), Q)
    hi_w = (i * Q - w0) + lax.broadcasted_iota(jnp.int32, (Q, WK), 0)
    col_w = lax.broadcasted_iota(jnp.int32, (Q, WK), 1)
    band = jnp.where((col_w <= hi_w) & (col_w > hi_w - WINDOW), 0.0, NEG)
    s_w = _dot_nt(q4b, kw_ref[0, pl.ds(w0, WK), :]) + jnp.concatenate([band] * R, axis=0)
    p_w = jnp.exp2(s_w - jnp.max(s_w, axis=-1, keepdims=True))
    o_w = (_dot(p_w.astype(BF16), vw_ref[0, pl.ds(w0, WK), :])
           / jnp.sum(p_w, axis=-1, keepdims=True))

    blk_bias = jnp.where(sel_t.T > 0.5, 0.0, NEG).astype(BF16)
    q_aug = jnp.concatenate([q4b, jnp.concatenate([blk_bias] * R, axis=0)], axis=1)

    def causal_mask(s, first_row, k0):
        t_k = i * Q + ((first_row + lax.broadcasted_iota(jnp.int32, s.shape, 0)) & (Q - 1))
        return jnp.where(k0 + lax.broadcasted_iota(jnp.int32, s.shape, 1) <= t_k, s, NEG)

    n_past = (i * Q) // KC
    o_s = _attend_causal(q_aug, ks_ref, vs_ref, n_past, KC, scratch, causal_mask)

    gates = jax.nn.sigmoid(misc_ref[0])
    gate = gate_ref[0]
    for r in range(R):
        g0 = gates[:, MISC_GL + 3 * r:MISC_GL + 3 * r + 1]
        g1 = gates[:, MISC_GL + 3 * r + 1:MISC_GL + 3 * r + 2]
        g2 = gates[:, MISC_GL + 3 * r + 2:MISC_GL + 3 * r + 3]
        rows = slice(r * Q, (r + 1) * Q)
        o = g0 * o_c[rows] + g1 * o_s[rows] + g2 * o_w[rows]
        o_ref[0, :, r * 128:(r + 1) * 128] = (
            o * _silu(gate[:, r * 128:(r + 1) * 128])).astype(o_ref.dtype)


def _nsa_attn(qf, k_c, v_c, ks, vs, kw, vw, proj):
    b, s, _ = qf.shape
    nc = s // CMP_STRIDE
    kc_len = min(1024, s)
    wk = min(WINDOW + Q_BLOCK, s)
    per_b = lambda n, w: pl.BlockSpec((1, n, w), lambda bb, i: (bb, 0, 0))
    return pl.pallas_call(
        functools.partial(_nsa_attn_kernel, S=s, NC=nc, KC=kc_len, WK=wk),
        grid=(b, s // Q_BLOCK),
        in_specs=[pl.BlockSpec((1, Q_BLOCK, GROUP_W), lambda bb, i: (bb, i, 0)),
                  per_b(nc, 128), per_b(nc, 128),
                  per_b(s, 256), per_b(s, 128), per_b(s, 128), per_b(s, 128),
                  _seg_spec(Q_BLOCK, 128, OFF_MISC), _seg_spec(Q_BLOCK, GROUP_W, OFF_DG)],
        out_specs=pl.BlockSpec((1, Q_BLOCK, GROUP_W), lambda bb, i: (bb, i, 0)),
        out_shape=jax.ShapeDtypeStruct((b, s, GROUP_W), BF16),
        scratch_shapes=_attn_scratch(N_HEADS * Q_BLOCK, kc_len),
        compiler_params=_cparams(("parallel", "arbitrary")),
        name="nsa_attn",
    )(qf, k_c, v_c, ks, vs, kw, vw, proj, proj)


def _reorder_w_in_kernel(w_ref, o_ref):
    o = np.concatenate([[0], np.cumsum(IN_SIZES)]).tolist()
    (a_qkv, a_decay, a_beta, a_gate, b_cq, b_ckv, b_kpe, b_gate, c_x, c_gate,
     d_q, d_kv, d_gl, d_gate) = range(len(IN_SIZES))
    for l in range(w_ref.shape[1]):
        seg = lambda k: w_ref[o[k]:o[k + 1], l, :]
        off = 0
        for k in (a_qkv, a_gate, b_gate, c_x, c_gate, d_q, d_gate, d_kv, b_cq):
            o_ref[l, off:off + IN_SIZES[k], :] = seg(k).astype(BF16)
            off += IN_SIZES[k]
        assert off == OFF_MISC
        zeros = jnp.zeros((128 - MLA_ROPE - 2 * N_HEADS - 3 * N_HEADS, w_ref.shape[2]), F32)
        misc = jnp.concatenate([seg(b_kpe), seg(a_decay), seg(a_beta), seg(d_gl), zeros], axis=0)
        o_ref[l, OFF_MISC:OFF_MISC + 128, :] = misc.astype(BF16)
        o_ref[l, OFF_BCKV:OFF_BCKV + MLA_KV_RANK, :] = seg(b_ckv).astype(BF16)


def _reorder_w_in(w_in):
    depth, d, d_in = w_in.shape
    tc = 128
    w_t = jnp.transpose(w_in, (2, 0, 1))
    return pl.pallas_call(
        _reorder_w_in_kernel,
        grid=(d // tc,),
        in_specs=[pl.BlockSpec((d_in, depth, tc), lambda i: (0, 0, i))],
        out_specs=pl.BlockSpec((depth, D_PROJ, tc), lambda i: (0, 0, i)),
        out_shape=jax.ShapeDtypeStruct((depth, D_PROJ, d), BF16),
        compiler_params=_cparams(("parallel",)),
        name="reorder_w_in",
    )(w_t)


def kernel(x, positions, norm_g, w_in, w_out, gdn_conv_w, gdn_a_log, gdn_dt_bias, gdn_norm_g,
           mla_q_norm_g, mla_w_uq, mla_kv_norm_g, mla_w_ukv, mla_qk_norm_g,
           lru_conv_w, lru_conv_b, lru_w_r, lru_b_r, lru_w_i, lru_b_i, lru_lambda,
           nsa_q_norm_g, nsa_k_norm_g, nsa_cmp_pe, nsa_cmp_w1, nsa_cmp_b1, nsa_cmp_w2, nsa_cmp_b2):
    b, s, d = x.shape
    depth = w_in.shape[0]
    w_in_r = _reorder_w_in(w_in)
    w_out_b = w_out.astype(BF16)
    posf = positions.astype(F32)
    rope = _rope_tables(jnp.broadcast_to(posf[:, :, None], (b, s, 128)))
    nc = s // CMP_STRIDE
    pos_end = jnp.pad(posf.reshape(b, nc, CMP_STRIDE)[:, 1:, CMP_STRIDE - 1], ((0, 0), (0, 1)))
    posc = jnp.broadcast_to(pos_end[:, :, None], (b, nc, 128))

    x2d = x.reshape(b * s, d)
    for l in range(depth):
        proj = _inproj(x2d, norm_g[l].reshape(1, d), w_in_r, l).reshape(b, s, D_PROJ)
        y_a = _gdn(proj, gdn_conv_w[l], gdn_a_log[l], gdn_dt_bias[l], gdn_norm_g[l])
        q_b, k_b, v_b = _mla_prep(proj, rope, mla_q_norm_g[l], mla_w_uq[l], mla_kv_norm_g[l],
                                  mla_w_ukv[l], mla_qk_norm_g[l])
        y_b = _mla_flash(q_b, k_b, v_b, proj)
        y_c = _lru(proj, lru_conv_w[l], lru_conv_b[l], lru_w_r[l], lru_b_r[l], lru_w_i[l],
                   lru_b_i[l], lru_lambda[l])
        q_d, ks, vs, kw, vw, kc_raw, vc_raw = _nsa_prep(proj, rope, nsa_q_norm_g[l],
                                                        nsa_k_norm_g[l])
        k_c, v_c = _nsa_compress(kc_raw, vc_raw, posc, nsa_cmp_pe[l], nsa_cmp_w1[l],
                                 nsa_cmp_b1[l], nsa_cmp_w2[l], nsa_cmp_b2[l], nsa_k_norm_g[l, 0])
        y_d = _nsa_attn(q_d, k_c, v_c, ks, vs, kw, vw, proj)
        ys = [y.reshape(b * s, GROUP_W) for y in (y_a, y_b, y_c, y_d)]
        x2d = _outproj(ys, w_out_b[l], x2d)
    return x2d.reshape(b, s, d)
```

```python
import functools
import math

import numpy as np
import jax
import jax.numpy as jnp
from jax import lax
from jax.experimental import pallas as pl
from jax.experimental.pallas import tpu as pltpu

F32 = jnp.float32
BF16 = jnp.bfloat16
HIGHEST = lax.Precision.HIGHEST

D_MODEL = 2048
GROUP_W = 512
HEAD_DIM = 128
N_HEADS = 4
RMS_EPS = 1e-6
ROPE_THETA = 10000.0
NEG = -1e30
LOG2E = math.log2(math.e)
GDN_CONV = 4
GDN_CHUNK = 128
MLA_Q_RANK = 384
MLA_KV_RANK = 256
MLA_NOPE = 128
MLA_ROPE = 64
LRU_CONV = 4
LRU_C = 8.0
CMP_LEN = 32
CMP_STRIDE = 16
CMP_HIDDEN = 256
SEL_LEN = 64
SEL_TOPK = 16
WINDOW = 512
FORCE = 1e9
Q_BLOCK = 256

IN_SIZES = (3 * GROUP_W, N_HEADS, N_HEADS, GROUP_W,
            MLA_Q_RANK, MLA_KV_RANK, MLA_ROPE, GROUP_W,
            GROUP_W, GROUP_W,
            GROUP_W, 6 * HEAD_DIM, 3 * N_HEADS, GROUP_W)

OFF_AQ, OFF_AK, OFF_AV, OFF_AG = 0, 512, 1024, 1536
OFF_BG, OFF_CX, OFF_CG, OFF_DQ, OFF_DG = 2048, 2560, 3072, 3584, 4096
OFF_DKV, OFF_BCQ, OFF_MISC, OFF_BCKV = 4608, 5376, 5760, 5888
D_PROJ = 6144
MISC_KPE, MISC_DECAY, MISC_BETA, MISC_GL = 0, 64, 68, 72

VMEM_LIMIT = 56 * 1024 * 1024


def _cparams(sem):
    return pltpu.CompilerParams(dimension_semantics=sem, vmem_limit_bytes=VMEM_LIMIT)


def _dot(a, b, precision=None):
    return lax.dot_general(a, b, (((1,), (0,)), ((), ())), precision=precision,
                           preferred_element_type=F32)


def _dot_nt(a, b, precision=None):
    return lax.dot_general(a, b, (((1,), (1,)), ((), ())), precision=precision,
                           preferred_element_type=F32)


def _bdot(a, b):
    return _dot(a.astype(BF16), b.astype(BF16))


def _bdot_nt(a, b):
    return _dot_nt(a.astype(BF16), b.astype(BF16))


def _split_bf16(x):
    hi = x.astype(BF16)
    return hi, (x - hi.astype(F32)).astype(BF16)


def _silu(x):
    h = 0.5 * x
    return h + h * jnp.tanh(h)


def _softplus(x):
    return jnp.maximum(x, 0.0) + jnp.log1p(jnp.exp(-jnp.abs(x)))


def _rms(x, n):
    return x * lax.rsqrt(jnp.sum(x * x, axis=-1, keepdims=True) * (1.0 / n) + RMS_EPS)


def _rms_mxu(x, n):
    w = x.shape[1]
    ssq = _dot((x * x).astype(BF16), jnp.ones((w, 128), BF16))
    r = lax.rsqrt(ssq * (1.0 / n) + RMS_EPS)
    if w == 128:
        return x * r
    return jnp.concatenate([x[:, j * 128:(j + 1) * 128] * r for j in range(w // 128)], axis=1)


def _sigmoid(x):
    return 0.5 + 0.5 * jnp.tanh(0.5 * x)


def _inproj_kernel(x_ref, g_ref, w_ref, o_ref, h_ref, *, tm):
    @pl.when(pl.program_id(1) == 0)
    def _():
        rows = min(256, tm)
        for r in range(tm // rows):
            x = x_ref[r * rows:(r + 1) * rows, :]
            h_ref[r * rows:(r + 1) * rows, :] = (_rms(x, D_MODEL) * g_ref[...]).astype(BF16)

    o_ref[...] = _dot_nt(h_ref[...], w_ref[...])


def _inproj(x2d, g, w_t, layer):
    m = x2d.shape[0]
    tm = min(1024, m)
    tn = 1024
    return pl.pallas_call(
        functools.partial(_inproj_kernel, tm=tm),
        grid=(m // tm, D_PROJ // tn),
        in_specs=[pl.BlockSpec((tm, D_MODEL), lambda i, j: (i, 0)),
                  pl.BlockSpec((1, D_MODEL), lambda i, j: (0, 0)),
                  pl.BlockSpec((None, tn, D_MODEL), lambda i, j: (layer, j, 0))],
        out_specs=pl.BlockSpec((tm, tn), lambda i, j: (i, j)),
        out_shape=jax.ShapeDtypeStruct((m, D_PROJ), F32),
        scratch_shapes=[pltpu.VMEM((tm, D_MODEL), BF16)],
        compiler_params=_cparams(("parallel", "arbitrary")),
        name="inproj",
    )(x2d, g, w_t)


def _outproj_kernel(ya_ref, yb_ref, yc_ref, yd_ref, wa_ref, wb_ref, wc_ref, wd_ref, x_ref, o_ref):
    acc = x_ref[...]
    for y_ref, w_ref in ((ya_ref, wa_ref), (yb_ref, wb_ref), (yc_ref, wc_ref), (yd_ref, wd_ref)):
        acc = acc + _dot(y_ref[...].astype(BF16), w_ref[...])
    o_ref[...] = acc


def _outproj(ys, w, x2d):
    m = x2d.shape[0]
    tm = min(1024, m)
    tn = 1024
    y_specs = [pl.BlockSpec((tm, GROUP_W), lambda i, j: (i, 0)) for _ in range(4)]
    w_specs = [pl.BlockSpec((GROUP_W, tn), functools.partial(lambda i, j, g: (g, j), g=g))
               for g in range(4)]
    return pl.pallas_call(
        _outproj_kernel,
        grid=(m // tm, D_MODEL // tn),
        in_specs=y_specs + w_specs + [pl.BlockSpec((tm, tn), lambda i, j: (i, j))],
        out_specs=pl.BlockSpec((tm, tn), lambda i, j: (i, j)),
        out_shape=jax.ShapeDtypeStruct((m, D_MODEL), F32),
        compiler_params=_cparams(("parallel", "arbitrary")),
        name="outproj",
    )(*ys, w, w, w, w, x2d)


def _causal_conv4(xin, ext_ref, w):
    t = xin.shape[0]
    ext_ref[8:8 + t, :] = xin
    ext = ext_ref[...]
    acc = ext * w[0:1]
    for j in range(1, 4):
        acc = pltpu.roll(acc, 1, 0) + ext * w[j:j + 1]
    ext_ref[0:8, :] = xin[t - 8:t]
    return acc[8:8 + t]


def _token_block_call(parts, b, s, t, name):
    n_in = [len(p["in_specs"]) for p in parts]
    n_out = [len(p["out_specs"]) for p in parts]
    n_scr = [len(p["scratch"]) for p in parts]

    def split(refs, counts):
        groups, pos = [], 0
        for n in counts:
            groups.append(refs[pos:pos + n])
            pos += n
        return groups

    def body(*refs):
        ins = split(refs[:sum(n_in)], n_in)
        outs = split(refs[sum(n_in):sum(n_in) + sum(n_out)], n_out)
        scrs = split(refs[sum(n_in) + sum(n_out):], n_scr)

        @pl.when(pl.program_id(1) == 0)
        def _():
            for p, scr in zip(parts, scrs):
                if p.get("init") is not None:
                    p["init"](*scr)

        calls = [functools.partial(p["body"], *i, *o, *scr)
                 for p, i, o, scr in zip(parts, ins, outs, scrs)]
        hosts = [c for p, c in zip(parts, calls) if p.get("host")]
        guests = [c for p, c in zip(parts, calls) if not p.get("host")]

        def run_guests():
            for c in guests:
                c()

        if hosts:
            (host,) = hosts
            host(interleave=run_guests)
        else:
            run_guests()

    res = pl.pallas_call(
        body,
        grid=(b, s // t),
        in_specs=[x for p in parts for x in p["in_specs"]],
        out_specs=[x for p in parts for x in p["out_specs"]],
        out_shape=[x for p in parts for x in p["out_shape"]],
        scratch_shapes=[x for p in parts for x in p["scratch"]],
        compiler_params=_cparams(("parallel", "arbitrary")),
        name=name,
    )(*[a for p in parts for a in p["args"]])
    return split(list(res), n_out)


def _gdn_init(hist_ref, state_ref):
    hist_ref[:, 0:8, :] = jnp.zeros((3, 8, GROUP_W), F32)
    state_ref[...] = jnp.zeros_like(state_ref)


def _gdn_kernel(q_ref, k_ref, v_ref, gate_ref, misc_ref, cw_ref, alog_ref, dtb_ref, ng_ref,
                o_ref, hist_ref, state_ref, *, T, interleave=None):
    C = GDN_CHUNK
    cw = cw_ref[...]
    conv = []
    for idx, ref in enumerate((q_ref, k_ref, v_ref)):
        xin = ref[0]
        acc = _causal_conv4(xin, hist_ref.at[idx], cw[:, idx * GROUP_W:(idx + 1) * GROUP_W])
        conv.append(_silu(acc))
    qc, kc, vc = conv

    misc = misc_ref[0]
    gfull = -jnp.exp(alog_ref[...]) * _softplus(misc + dtb_ref[...])
    betafull = jax.nn.sigmoid(misc)
    gate = gate_ref[0]

    row = lax.broadcasted_iota(jnp.int32, (C, C), 0)
    col = lax.broadcasted_iota(jnp.int32, (C, C), 1)
    lower = row >= col
    strict = row > col
    ltri = lower.astype(F32)
    row2 = lax.broadcasted_iota(jnp.int32, (C, 2 * C), 0)
    lane2 = lax.broadcasted_iota(jnp.int32, (C, 2 * C), 1)
    left = lane2 < C
    col2 = lane2 & (C - 1)
    leaf_bits = 4
    same_blk = {bits: (row2 >> bits) == (col2 >> bits) for bits in range(leaf_bits, 8)}
    assert C == HEAD_DIM == 1 << 7

    def pairdot(x, y):
        yb = y.astype(BF16)
        zero = jnp.zeros_like(yb)
        bd = jnp.concatenate([jnp.where(left, yb, zero), jnp.where(left, zero, yb)], axis=0)
        return _dot(x.astype(BF16), bd)

    n_chunks = T // C
    hds = []
    for c in range(n_chunks):
        r0 = c * C
        gall = _dot(ltri, gfull[r0:r0 + C], HIGHEST)
        hd = []
        hds.append(hd)
        for h in range(N_HEADS):
            l0 = h * HEAD_DIM
            q = qc[r0:r0 + C, l0:l0 + HEAD_DIM]
            k = kc[r0:r0 + C, l0:l0 + HEAD_DIM]
            v = vc[r0:r0 + C, l0:l0 + HEAD_DIM]
            ones = jnp.ones((HEAD_DIM, 128), BF16)
            q = q * (lax.rsqrt(_dot((q * q).astype(BF16), ones) + RMS_EPS) * (HEAD_DIM ** -0.5))
            k = k * lax.rsqrt(_dot((k * k).astype(BF16), ones) + RMS_EPS)
            beta = betafull[r0:r0 + C, MISC_BETA + h:MISC_BETA + h + 1]
            gc = jnp.broadcast_to(gall[:, MISC_DECAY + h:MISC_DECAY + h + 1], (C, C))
            decay = jnp.exp(jnp.where(lower, gc - gc.T, NEG))
            kb = k * beta
            a = jnp.where(strict, _bdot_nt(kb, k) * decay, 0.0)
            hd.append((q, k, v, beta, gc, decay, kb, a))

    if interleave is not None:
        interleave()

    a2s = [jnp.concatenate([hds[c][h0][7], hds[c][h0 + 1][7]], axis=1)
           for c in range(n_chunks) for h0 in range(0, N_HEADS, 2)]
    ps = [jnp.where(same_blk[leaf_bits], -a2, 0.0) for a2 in a2s]
    tm2 = list(ps)
    for _ in range(leaf_bits - 1):
        ps = [pairdot(p, p) for p in ps]
        tm2 = [tm + p + pairdot(tm, p) for tm, p in zip(tm2, ps)]
    for bits in range(leaf_bits + 1, 8):
        join = same_blk[bits] & jnp.logical_not(same_blk[bits - 1])
        tas = [jnp.where(join, a2, 0.0) for a2 in a2s]
        tas = [off + pairdot(tm, off) for tm, off in zip(tm2, tas)]
        tm2 = [tm - (ta + pairdot(ta, tm)) for tm, ta in zip(tm2, tas)]

    for c in range(n_chunks):
        r0 = c * C
        hd = hds[c]
        tms = []
        for tm in tm2[2 * c:2 * c + 2]:
            tms += [tm[:, :C], tm[:, C:]]
        for h in range(N_HEADS):
            l0 = h * HEAD_DIM
            q, k, v, beta, gc, decay, kb, _ = hd[h]
            eg = jnp.exp(gc)
            rhs = jnp.concatenate([v * beta, kb * eg], axis=-1)
            uw = rhs + _bdot(tms[h], rhs)
            u = uw[:, :HEAD_DIM]
            w = uw[:, HEAD_DIM:]
            qk = _bdot_nt(q, k) * decay
            q_dec = q * eg
            g_last = gc[C - 1:C, :]
            k_end = k * jnp.exp(g_last - gc)
            c_dec = jnp.exp(g_last)
            state = state_ref[h]
            v_new = u - _bdot(w, state)
            o = _bdot(q_dec, state) + _bdot(qk, v_new)
            state_ref[h] = state * c_dec + lax.dot_general(
                k_end.astype(BF16), v_new.astype(BF16), (((0,), (0,)), ((), ())),
                preferred_element_type=F32)
            o = _rms(o, HEAD_DIM) * ng_ref[...]
            o_ref[0, r0:r0 + C, l0:l0 + HEAD_DIM] = (
                o * _silu(gate[r0:r0 + C, l0:l0 + HEAD_DIM])).astype(o_ref.dtype)


def _seg_spec(t, width, off):
    blk = off // width
    assert blk * width == off
    return pl.BlockSpec((1, t, width), lambda b, i: (b, i, blk))


def _token_block(s):
    return min(512, s)


def _gdn_part(proj, conv_w, a_log, dt_bias, norm_g):
    b, s, _ = proj.shape
    t = _token_block(s)
    pad = lambda vec, off: jnp.zeros((1, 128), F32).at[0, off:off + N_HEADS].set(vec)
    full = lambda shape: pl.BlockSpec(shape, lambda bb, i: (0,) * len(shape))
    return dict(
        body=functools.partial(_gdn_kernel, T=t),
        init=_gdn_init,
        in_specs=[_seg_spec(t, GROUP_W, OFF_AQ), _seg_spec(t, GROUP_W, OFF_AK),
                  _seg_spec(t, GROUP_W, OFF_AV), _seg_spec(t, GROUP_W, OFF_AG),
                  _seg_spec(t, 128, OFF_MISC),
                  full((GDN_CONV, 3 * GROUP_W)), full((1, 128)), full((1, 128)), full((1, 128))],
        out_specs=[pl.BlockSpec((1, t, GROUP_W), lambda bb, i: (bb, i, 0))],
        out_shape=[jax.ShapeDtypeStruct((b, s, GROUP_W), BF16)],
        scratch=[pltpu.VMEM((3, t + 8, GROUP_W), F32),
                 pltpu.VMEM((N_HEADS, HEAD_DIM, HEAD_DIM), F32)],
        args=(proj, proj, proj, proj, proj, conv_w, pad(a_log, MISC_DECAY),
              pad(dt_bias, MISC_DECAY), norm_g.reshape(1, HEAD_DIM)))


def _gdn(proj, *weights):
    b, s, _ = proj.shape
    return _token_block_call([_gdn_part(proj, *weights)], b, s, _token_block(s), "gdn")[0][0]


def _lru_init(hist_ref, h_ref):
    hist_ref[0:8, :] = jnp.zeros((8, GROUP_W), F32)
    h_ref[...] = jnp.zeros_like(h_ref)


def _lru_kernel(x_ref, gate_ref, cw_ref, cb_ref, wr_ref, br_ref, wi_ref, bi_ref, lam_ref,
                o_ref, hist_ref, h_ref, *, T):
    xin = x_ref[0]
    xc = _causal_conv4(xin, hist_ref, cw_ref[...]) + cb_ref[...]
    rs, igs = [], []
    for h in range(N_HEADS):
        xh = xc[:, h * HEAD_DIM:(h + 1) * HEAD_DIM].astype(BF16)
        rs.append(_dot(xh, wr_ref[h]))
        igs.append(_dot(xh, wi_ref[h]))
    r = _sigmoid(jnp.concatenate(rs, axis=-1) + br_ref[...])
    ig = _sigmoid(jnp.concatenate(igs, axis=-1) + bi_ref[...])
    log_a = -LRU_C * r * _softplus(-lam_ref[...])
    a = jnp.exp(log_a)
    bb = jnp.sqrt(-jnp.tanh(log_a) * (a * a + 1.0)) * (ig * xc)
    a3 = a.reshape(T // 8, 8, GROUP_W)
    b3 = bb.reshape(T // 8, 8, GROUP_W)
    sub = lax.broadcasted_iota(jnp.int32, (T // 8, 8, GROUP_W), 1)
    for d in (1, 2, 4):
        keep = sub >= d
        a_sh = jnp.where(keep, pltpu.roll(a3, d, 1), 1.0)
        b_sh = jnp.where(keep, pltpu.roll(b3, d, 1), 0.0)
        b3 = a3 * b_sh + b3
        a3 = a3 * a_sh
    h = h_ref[...]
    groups = []
    for g in range(T // 8):
        hg = b3[g] + a3[g] * h
        h = hg[7:8]
        groups.append(hg)
    h_ref[...] = h
    o_ref[0] = (jnp.concatenate(groups, axis=0) * _silu(gate_ref[0])).astype(o_ref.dtype)


def _lru_part(proj, conv_w, conv_b, w_r, b_r, w_i, b_i, lam):
    b, s, _ = proj.shape
    t = _token_block(s)
    full = lambda shape: pl.BlockSpec(shape, lambda bb, i: (0,) * len(shape))
    vec = lambda v: v.reshape(1, GROUP_W)
    return dict(
        body=functools.partial(_lru_kernel, T=t),
        init=_lru_init,
        in_specs=[_seg_spec(t, GROUP_W, OFF_CX), _seg_spec(t, GROUP_W, OFF_CG),
                  full((LRU_CONV, GROUP_W)), full((1, GROUP_W)),
                  full((N_HEADS, HEAD_DIM, HEAD_DIM)), full((1, GROUP_W)),
                  full((N_HEADS, HEAD_DIM, HEAD_DIM)), full((1, GROUP_W)), full((1, GROUP_W))],
        out_specs=[pl.BlockSpec((1, t, GROUP_W), lambda bb, i: (bb, i, 0))],
        out_shape=[jax.ShapeDtypeStruct((b, s, GROUP_W), BF16)],
        scratch=[pltpu.VMEM((t + 8, GROUP_W), F32), pltpu.VMEM((1, GROUP_W), F32)],
        args=(proj, proj, conv_w, vec(conv_b), w_r.astype(BF16), vec(b_r), w_i.astype(BF16),
              vec(b_i), vec(lam)))


def _lru(proj, *weights):
    b, s, _ = proj.shape
    return _token_block_call([_lru_part(proj, *weights)], b, s, _token_block(s), "rglru")[0][0]


def _inv_freq_row(d):
    i = np.arange(128) % (d // 2)
    return jnp.asarray((ROPE_THETA ** (-(2.0 * i) / d)).astype(np.float32).reshape(1, 128))


def _rope128(x, cos, sin_signed):
    return x * cos + pltpu.roll(x, 64, 1) * sin_signed


def _rope64(x, cos, sin_masked, lane):
    rot = jnp.where(lane < 32, -pltpu.roll(x, 96, 1), pltpu.roll(x, 32, 1))
    return x * cos + rot * sin_masked


def _rope_tables_kernel(pos_ref, invf_ref, o_ref):
    lane = lax.broadcasted_iota(jnp.int32, pos_ref.shape[1:], 1)
    ang = pos_ref[0] * invf_ref[...]
    c = jnp.cos(ang)
    s = jnp.sin(ang)
    c32, c96 = pltpu.roll(c, 32, 1), pltpu.roll(c, 96, 1)
    s32, s96 = pltpu.roll(s, 32, 1), pltpu.roll(s, 96, 1)
    o_ref[0, :, 0:128] = jnp.where(lane < 32, c, c32)
    o_ref[0, :, 128:256] = jnp.where(lane < 32, s, jnp.where(lane < MLA_ROPE, s32, 0.0))
    o_ref[0, :, 256:384] = jnp.where(lane < 64, c96, c32)
    o_ref[0, :, 384:512] = jnp.where(lane < 64, -s96, s32)


def _rope_tables(posb):
    b, s, _ = posb.shape
    t = min(512, s)
    full = lambda shape: pl.BlockSpec(shape, lambda bb, i: (0,) * len(shape))
    return pl.pallas_call(
        _rope_tables_kernel,
        grid=(b, s // t),
        in_specs=[pl.BlockSpec((1, t, 128), lambda bb, i: (bb, i, 0)), full((1, 128))],
        out_specs=pl.BlockSpec((1, t, 512), lambda bb, i: (bb, i, 0)),
        out_shape=jax.ShapeDtypeStruct((b, s, 512), F32),
        compiler_params=_cparams(("parallel", "parallel")),
        name="rope_tables",
    )(posb, jnp.concatenate([_inv_freq_row(MLA_ROPE)[:, :32], _inv_freq_row(HEAD_DIM)[:, :64],
                             jnp.zeros((1, 32), F32)], axis=1))


def _mla_prep_kernel(cq_ref, ckv_ref, misc_ref, rope_ref, qng_ref, wuq_ref, kvng_ref, wukv_ref,
                     qgn_ref, qgp_ref, kgn_ref, kgp_ref,
                     q_ref, k_ref, v_ref, *, T):
    scale = (MLA_NOPE + MLA_ROPE) ** -0.5 * LOG2E
    lane = lax.broadcasted_iota(jnp.int32, (T, 128), 1)
    cos = rope_ref[0, :, 0:128]
    sin = rope_ref[0, :, 128:256]

    cq = _rms(cq_ref[0], MLA_Q_RANK) * qng_ref[...]
    qf = _dot(cq.astype(BF16), wuq_ref[...])
    ckv = _rms(ckv_ref[0], MLA_KV_RANK) * kvng_ref[...]
    kvf = _dot(ckv.astype(BF16), wukv_ref[...])
    kpe = jnp.where(lane < MLA_ROPE, misc_ref[0], 0.0)
    kpe = _rope64(_rms(kpe, MLA_ROPE) * kgp_ref[...], cos, sin, lane)
    for h in range(N_HEADS):
        o = h * 256
        qn = _rms_mxu(qf[:, o:o + 128], MLA_NOPE) * qgn_ref[...]
        qp = _rope64(_rms_mxu(qf[:, o + 128:o + 256], MLA_ROPE) * qgp_ref[...], cos, sin, lane)
        q_ref[0, :, o:o + 128] = (qn * scale).astype(BF16)
        q_ref[0, :, o + 128:o + 256] = (qp * scale).astype(BF16)
        kn = _rms(kvf[:, o:o + 128], MLA_NOPE) * kgn_ref[...]
        k_ref[0, :, o:o + 128] = kn.astype(BF16)
        k_ref[0, :, o + 128:o + 256] = kpe.astype(BF16)
        v_ref[0, :, h * 128:(h + 1) * 128] = kvf[:, o + 128:o + 256].astype(BF16)


def _mla_prep_part(proj, rope, q_norm_g, w_uq, kv_norm_g, w_ukv, qk_norm_g):
    b, s, _ = proj.shape
    t = _token_block(s)
    wuq = jnp.pad(w_uq, ((0, 0), (0, 0), (0, 64))).reshape(MLA_Q_RANK, N_HEADS * 256).astype(BF16)
    wukv = w_ukv.reshape(MLA_KV_RANK, N_HEADS * 256).astype(BF16)
    pad64 = lambda v: jnp.pad(v, (0, 64)).reshape(1, 128)
    full = lambda shape: pl.BlockSpec(shape, lambda bb, i: (0,) * len(shape))
    tok = lambda w: pl.BlockSpec((1, t, w), lambda bb, i: (bb, i, 0))
    return dict(
        body=functools.partial(_mla_prep_kernel, T=t),
        in_specs=[_seg_spec(t, MLA_Q_RANK, OFF_BCQ), _seg_spec(t, MLA_KV_RANK, OFF_BCKV),
                  _seg_spec(t, 128, OFF_MISC), pl.BlockSpec((1, t, 256), lambda bb, i: (bb, i, 0)),
                  full((1, MLA_Q_RANK)), full((MLA_Q_RANK, 1024)),
                  full((1, MLA_KV_RANK)), full((MLA_KV_RANK, 1024)),
                  full((1, 128)), full((1, 128)), full((1, 128)), full((1, 128))],
        out_specs=[tok(1024), tok(1024), tok(512)],
        out_shape=[jax.ShapeDtypeStruct((b, s, 1024), BF16),
                   jax.ShapeDtypeStruct((b, s, 1024), BF16),
                   jax.ShapeDtypeStruct((b, s, 512), BF16)],
        scratch=[],
        args=(proj, proj, proj, rope, q_norm_g.reshape(1, -1), wuq, kv_norm_g.reshape(1, -1), wukv,
              qk_norm_g[0, :MLA_NOPE].reshape(1, 128), pad64(qk_norm_g[0, MLA_NOPE:]),
              qk_norm_g[1, :MLA_NOPE].reshape(1, 128), pad64(qk_norm_g[1, MLA_NOPE:])))


def _mla_prep(proj, *rest):
    b, s, _ = proj.shape
    return _token_block_call([_mla_prep_part(proj, *rest)], b, s, _token_block(s), "mla_prep")[0]


ROW_GROUPS = 2


def _attn_scratch(m, kc):
    return [pltpu.VMEM((m, 1), F32), pltpu.VMEM((m, 128), F32), pltpu.VMEM((m, 128), F32),
            pltpu.VMEM((m, kc), F32), pltpu.VMEM((m, kc), F32)]


def _attend_causal(q, k_of, v_of, n_past, kc, scratch, mask_fn):
    m_ref, l_ref, acc_ref, sa_ref, sb_ref = scratch
    m_ref[...] = jnp.full_like(m_ref, NEG)
    l_ref[...] = jnp.zeros_like(l_ref)
    acc_ref[...] = jnp.zeros_like(acc_ref)
    g_rows = q.shape[0] // ROW_GROUPS

    def scores(c, s_ref):
        k0 = pl.multiple_of(c * kc, kc)
        for g in range(ROW_GROUPS):
            rows = slice(g * g_rows, (g + 1) * g_rows)
            s_ref[rows, :] = _dot_nt(q[rows], k_of(k0, g))

    def consume(c, s_ref, masked):
        k0 = pl.multiple_of(c * kc, kc)
        for g in range(ROW_GROUPS):
            rows = slice(g * g_rows, (g + 1) * g_rows)
            v_blk = v_of(k0, g)
            s = s_ref[rows, :]
            if masked:
                s = mask_fn(s, g * g_rows, k0)
            m_old = m_ref[rows, :]
            m_new = jnp.maximum(m_old, jnp.max(s, axis=-1, keepdims=True))
            alpha = jnp.exp2(m_old - m_new)
            p = jnp.exp2(s - m_new)
            part = p[:, 0:128]
            for j in range(1, kc // 128):
                part = part + p[:, j * 128:(j + 1) * 128]
            l_ref[rows, :] = alpha * l_ref[rows, :] + part
            acc_ref[rows, :] = alpha * acc_ref[rows, :] + _dot(p.astype(BF16), v_blk)
            m_ref[rows, :] = m_new

    def past_pair(j, carry):
        c = 2 * j
        scores(c + 1, sb_ref)
        consume(c, sa_ref, False)
        scores(c + 2, sa_ref)
        consume(c + 1, sb_ref, False)
        return carry

    scores(0, sa_ref)
    lax.fori_loop(0, n_past // 2, past_pair, 0)

    @pl.when(n_past % 2 == 0)
    def _():
        consume(n_past, sa_ref, True)

    @pl.when(n_past % 2 == 1)
    def _():
        scores(n_past, sb_ref)
        consume(n_past - 1, sa_ref, False)
        consume(n_past, sb_ref, True)

    return acc_ref[...] / jnp.sum(l_ref[...], axis=-1, keepdims=True)


def _mla_flash_kernel(q_ref, k_ref, v_ref, gate_ref, o_ref, *scratch, tq, kc):
    i = pl.program_id(2)
    q = q_ref[0]

    def causal(s, first_row, k0):
        row = lax.broadcasted_iota(jnp.int32, s.shape, 0) + (i * tq + first_row)
        col = lax.broadcasted_iota(jnp.int32, s.shape, 1) + k0
        return jnp.where(col <= row, s, NEG)

    n_full = (i * tq) // kc
    o = _attend_causal(q, lambda k0, g: k_ref[0, pl.ds(k0, kc), :],
                       lambda k0, g: v_ref[0, pl.ds(k0, kc), :], n_full, kc, scratch, causal)
    o_ref[0] = (o * _silu(gate_ref[0])).astype(o_ref.dtype)


def _mla_flash(q, k, v, proj):
    b, s, _ = q.shape
    tq = min(1024, s)
    kc = min(1024, s)
    gate_blk = OFF_BG // 128
    return pl.pallas_call(
        functools.partial(_mla_flash_kernel, tq=tq, kc=kc),
        grid=(b, N_HEADS, s // tq),
        in_specs=[pl.BlockSpec((1, tq, 256), lambda bb, h, i: (bb, i, h)),
                  pl.BlockSpec((1, s, 256), lambda bb, h, i: (bb, 0, h)),
                  pl.BlockSpec((1, s, 128), lambda bb, h, i: (bb, 0, h)),
                  pl.BlockSpec((1, tq, 128), lambda bb, h, i: (bb, i, gate_blk + h))],
        out_specs=pl.BlockSpec((1, tq, 128), lambda bb, h, i: (bb, i, h)),
        out_shape=jax.ShapeDtypeStruct((b, s, GROUP_W), BF16),
        scratch_shapes=_attn_scratch(tq, kc),
        compiler_params=_cparams(("parallel", "parallel", "arbitrary")),
        name="mla_flash",
    )(q, k, v, proj)


def _nsa_prep_kernel(dq_ref, dkv_ref, rope_ref, qg_ref, kg_ref,
                     q_ref, ks_ref, vs_ref, kw_ref, vw_ref, kc_ref, vc_ref, *, T):
    scale = HEAD_DIM ** -0.5 * LOG2E
    lane = lax.broadcasted_iota(jnp.int32, (T, 128), 1)
    cos = rope_ref[0, :, 0:128]
    sin = rope_ref[0, :, 128:256]
    dq = dq_ref[0]
    for h in range(N_HEADS):
        qh = _rms_mxu(dq[:, h * 128:(h + 1) * 128], HEAD_DIM) * qg_ref[...]
        q_ref[0, :, h * 128:(h + 1) * 128] = _rope128(qh, cos, sin) * scale
    kv = dkv_ref[0]
    kc_ref[0] = kv[:, 0:128]
    vc_ref[0] = kv[:, 128:256]
    ks_ref[0, :, 0:128] = _rope128(_rms_mxu(kv[:, 256:384], HEAD_DIM) * kg_ref[1:2], cos,
                                   sin).astype(BF16)
    key_blk = (pl.program_id(1) * T + lax.broadcasted_iota(jnp.int32, (T, 128), 0)) >> 6
    ks_ref[0, :, 128:256] = (lane == key_blk).astype(BF16)
    vs_ref[0] = kv[:, 384:512].astype(BF16)
    kw_ref[0] = _rope128(_rms_mxu(kv[:, 512:640], HEAD_DIM) * kg_ref[2:3], cos, sin).astype(BF16)
    vw_ref[0] = kv[:, 640:768].astype(BF16)


def _nsa_prep_part(proj, rope, q_norm_g, k_norm_g):
    b, s, _ = proj.shape
    t = _token_block(s)
    full = lambda shape: pl.BlockSpec(shape, lambda bb, i: (0,) * len(shape))
    tok = lambda w: pl.BlockSpec((1, t, w), lambda bb, i: (bb, i, 0))
    sds = lambda w, dt: jax.ShapeDtypeStruct((b, s, w), dt)
    return dict(
        body=functools.partial(_nsa_prep_kernel, T=t),
        in_specs=[_seg_spec(t, GROUP_W, OFF_DQ), _seg_spec(t, 768, OFF_DKV),
                  pl.BlockSpec((1, t, 256), lambda bb, i: (bb, i, 1)),
                  full((1, 128)), full((3, 128))],
        out_specs=[tok(512), tok(256)] + [tok(128)] * 5,
        out_shape=[sds(512, F32), sds(256, BF16), sds(128, BF16), sds(128, BF16), sds(128, BF16),
                   sds(128, F32), sds(128, F32)],
        scratch=[],
        args=(proj, proj, rope, q_norm_g.reshape(1, HEAD_DIM), k_norm_g))


def _nsa_prep(proj, *rest):
    b, s, _ = proj.shape
    return _token_block_call([_nsa_prep_part(proj, *rest)], b, s, _token_block(s), "nsa_prep")[0]


def _nsa_cmp_kernel(kt_ref, vt_ref, pe_ref, w1_ref, b1_ref, w2_ref, b2_ref, kg_ref, pos_ref,
                    invf_ref, kc_ref, vc_ref, *, NC):
    half = CMP_STRIDE * HEAD_DIM
    outs = []
    for j, t_ref in enumerate((kt_ref, vt_ref)):
        t2 = t_ref[0].astype(BF16)
        first = _dot(t2, w1_ref[j, :half, :])
        second = pltpu.roll(_dot(t2, w1_ref[j, half:, :]), NC - 1, 0)
        pe8 = jnp.broadcast_to(pe_ref[j], (8, CMP_LEN * HEAD_DIM)).astype(BF16)
        bias = _dot(pe8, w1_ref[j])[0:1] + b1_ref[j]
        hid = _silu(first + second + bias)
        outs.append(_dot(hid.astype(BF16), w2_ref[j]) + b2_ref[j])
    k_c, v_c = outs
    lane = lax.broadcasted_iota(jnp.int32, (NC, 128), 1)
    ang = pos_ref[0] * invf_ref[...]
    sin = jnp.sin(ang)
    sin = jnp.where(lane < 64, -sin, sin)
    kc_ref[0] = _rope128(_rms(k_c, HEAD_DIM) * kg_ref[...], jnp.cos(ang), sin)
    vc_ref[0] = v_c


def _nsa_compress(kc_raw, vc_raw, posc, cmp_pe, cmp_w1, cmp_b1, cmp_w2, cmp_b2, kg0):
    b, s, _ = kc_raw.shape
    nc = s // CMP_STRIDE
    kt = kc_raw.reshape(b, nc, CMP_STRIDE * HEAD_DIM)
    vt = vc_raw.reshape(b, nc, CMP_STRIDE * HEAD_DIM)
    full = lambda shape: pl.BlockSpec(shape, lambda bb: (0,) * len(shape))
    per_b = lambda shape: pl.BlockSpec((1,) + shape, lambda bb: (bb, 0, 0))
    return pl.pallas_call(
        functools.partial(_nsa_cmp_kernel, NC=nc),
        grid=(b,),
        in_specs=[per_b((nc, CMP_STRIDE * HEAD_DIM)), per_b((nc, CMP_STRIDE * HEAD_DIM)),
                  full((2, 1, CMP_LEN * HEAD_DIM)), full((2, CMP_LEN * HEAD_DIM, CMP_HIDDEN)),
                  full((2, 1, CMP_HIDDEN)), full((2, CMP_HIDDEN, HEAD_DIM)),
                  full((2, 1, HEAD_DIM)), full((1, HEAD_DIM)), per_b((nc, 128)), full((1, 128))],
        out_specs=[per_b((nc, HEAD_DIM)), per_b((nc, HEAD_DIM))],
        out_shape=[jax.ShapeDtypeStruct((b, nc, HEAD_DIM), F32),
                   jax.ShapeDtypeStruct((b, nc, HEAD_DIM), F32)],
        compiler_params=_cparams(("parallel",)),
        name="nsa_compress",
    )(kt, vt, cmp_pe.reshape(2, 1, CMP_LEN * HEAD_DIM), cmp_w1.astype(BF16),
      cmp_b1.reshape(2, 1, CMP_HIDDEN), cmp_w2.astype(BF16), cmp_b2.reshape(2, 1, HEAD_DIM),
      kg0.reshape(1, HEAD_DIM), posc, _inv_freq_row(HEAD_DIM))


def _nsa_attn_kernel(q_ref, kc_ref, vc_ref, ks_ref, vs_ref, kw_ref, vw_ref, misc_ref, gate_ref,
                     o_ref, *scratch, S, NC, KC, WK):
    Q = Q_BLOCK
    R = N_HEADS
    i = pl.program_id(1)
    n_sel = S // SEL_LEN
    n_cmp = (S - CMP_LEN) // CMP_STRIDE + 1
    top_k = min(SEL_TOPK, n_sel)

    qf = q_ref[0]
    q4 = jnp.concatenate([qf[:, r * 128:(r + 1) * 128] for r in range(R)], axis=0)
    q4b = q4.astype(BF16)

    q_hi, q_lo = _split_bf16(q4)
    k_hi, k_lo = _split_bf16(kc_ref[0])
    s_c = (_dot_nt(jnp.concatenate([q_hi, q_lo], axis=1), jnp.concatenate([k_hi, k_hi], axis=1))
           + _dot_nt(q_hi, k_lo))
    t_c = i * Q + (lax.broadcasted_iota(jnp.int32, (R * Q, NC), 0) & (Q - 1))
    c_ix = lax.broadcasted_iota(jnp.int32, (R * Q, NC), 1)
    valid_c = (c_ix * CMP_STRIDE + (CMP_LEN - 1) <= t_c) & (c_ix < n_cmp)
    s_c = jnp.where(valid_c, s_c, NEG)
    p_c = jnp.where(valid_c, jnp.exp2(s_c - jnp.max(s_c, axis=-1, keepdims=True)), 0.0)
    p_c = p_c / jnp.maximum(jnp.sum(p_c, axis=-1, keepdims=True), 1e-30)
    o_c = _bdot(p_c, vc_ref[0])
    assert NC & (NC - 1) == 0
    c_o = (lax.broadcasted_iota(jnp.int32, (2 * NC, 128), 0) & (NC - 1)) * CMP_STRIDE
    n_o = lax.broadcasted_iota(jnp.int32, (2 * NC, 128), 1) * SEL_LEN
    overlap2 = ((c_o < n_o + SEL_LEN) & (c_o + (CMP_LEN - 1) >= n_o)).astype(BF16)
    imp4 = _dot(jnp.concatenate(_split_bf16(p_c), axis=1), overlap2)
    imp = (imp4[0:Q] + imp4[Q:2 * Q] + imp4[2 * Q:3 * Q] + imp4[3 * Q:4 * Q]).T

    NP = min(128, -(-n_sel // 8) * 8)
    t_q = i * Q + lax.broadcasted_iota(jnp.int32, (NP, Q), 1)
    n_ix = lax.broadcasted_iota(jnp.int32, (NP, Q), 0)
    cur = t_q >> 6
    valid_s = (n_ix * SEL_LEN <= t_q) & (n_ix < n_sel)
    forced = (n_ix == 0) | (n_ix == cur) | (n_ix == cur - 1)
    val = jnp.where(valid_s, imp[:NP], -1.0)
    val = jnp.where(forced & valid_s, -3.0, val)
    val = jnp.where(n_ix < n_sel, val, -2.0)
    sel_t = jnp.where(forced & valid_s, 1.0, 0.0)
    n_f = n_ix.astype(F32)
    for _ in range(max(top_k - 3, 0)):
        mx = jnp.max(val, axis=0, keepdims=True)
        first = jnp.min(jnp.where(val == mx, n_f, 1e9), axis=0, keepdims=True)
        hit = n_f == first
        sel_t = jnp.where(hit, 1.0, sel_t)
        val = jnp.where(hit, -3.0, val)
    sel_t = jnp.where(valid_s, sel_t, 0.0)
    if NP < 128:
        sel_t = jnp.concatenate([sel_t, jnp.zeros((128 - NP, Q), F32)], axis=0)
    w0 = pl.multiple_of(jnp.maximum(i * Q - WINDOW, 0), Q)
    hi_w = (i * Q - w0) + lax.broadcasted_iota(jnp.int32, (Q, WK), 0)
    col_w = lax.broadcasted_iota(jnp.int32, (Q, WK), 1)
    band = jnp.where((col_w <= hi_w) & (col_w > hi_w - WINDOW), 0.0, NEG)
    s_w = _dot_nt(q4b, kw_ref[0, pl.ds(w0, WK), :]) + jnp.concatenate([band] * R, axis=0)
    p_w = jnp.exp2(s_w - jnp.max(s_w, axis=-1, keepdims=True))
    o_w = (_dot(p_w.astype(BF16), vw_ref[0, pl.ds(w0, WK), :])
           / jnp.sum(p_w, axis=-1, keepdims=True))

    blk_bias = jnp.where(sel_t.T > 0.5, 0.0, NEG).astype(BF16)
    q_aug = jnp.concatenate([q4b, jnp.concatenate([blk_bias] * R, axis=0)], axis=1)

    def causal_mask(s, first_row, k0):
        t_k = i * Q + ((first_row + lax.broadcasted_iota(jnp.int32, s.shape, 0)) & (Q - 1))
        return jnp.where(k0 + lax.broadcasted_iota(jnp.int32, s.shape, 1) <= t_k, s, NEG)

    n_past = (i * Q) // KC
    o_s = _attend_causal(q_aug, lambda k0, g: ks_ref[0, pl.ds(k0, KC), :],
                         lambda k0, g: vs_ref[0, pl.ds(k0, KC), :], n_past, KC, scratch, causal_mask)

    gates = jax.nn.sigmoid(misc_ref[0])
    gate = gate_ref[0]
    for r in range(R):
        g0 = gates[:, MISC_GL + 3 * r:MISC_GL + 3 * r + 1]
        g1 = gates[:, MISC_GL + 3 * r + 1:MISC_GL + 3 * r + 2]
        g2 = gates[:, MISC_GL + 3 * r + 2:MISC_GL + 3 * r + 3]
        rows = slice(r * Q, (r + 1) * Q)
        o = g0 * o_c[rows] + g1 * o_s[rows] + g2 * o_w[rows]
        o_ref[0, :, r * 128:(r + 1) * 128] = (
            o * _silu(gate[:, r * 128:(r + 1) * 128])).astype(o_ref.dtype)


def _nsa_attn(qf, k_c, v_c, ks, vs, kw, vw, proj):
    b, s, _ = qf.shape
    nc = s // CMP_STRIDE
    kc_len = min(1024, s)
    wk = min(WINDOW + Q_BLOCK, s)
    per_b = lambda n, w: pl.BlockSpec((1, n, w), lambda bb, i: (bb, 0, 0))
    return pl.pallas_call(
        functools.partial(_nsa_attn_kernel, S=s, NC=nc, KC=kc_len, WK=wk),
        grid=(b, s // Q_BLOCK),
        in_specs=[pl.BlockSpec((1, Q_BLOCK, GROUP_W), lambda bb, i: (bb, i, 0)),
                  per_b(nc, 128), per_b(nc, 128),
                  per_b(s, 256), per_b(s, 128), per_b(s, 128), per_b(s, 128),
                  _seg_spec(Q_BLOCK, 128, OFF_MISC), _seg_spec(Q_BLOCK, GROUP_W, OFF_DG)],
        out_specs=pl.BlockSpec((1, Q_BLOCK, GROUP_W), lambda bb, i: (bb, i, 0)),
        out_shape=jax.ShapeDtypeStruct((b, s, GROUP_W), BF16),
        scratch_shapes=_attn_scratch(N_HEADS * Q_BLOCK, kc_len),
        compiler_params=_cparams(("parallel", "arbitrary")),
        name="nsa_attn",
    )(qf, k_c, v_c, ks, vs, kw, vw, proj, proj)


def _reorder_w_in_kernel(w_ref, o_ref):
    o = np.concatenate([[0], np.cumsum(IN_SIZES)]).tolist()
    (a_qkv, a_decay, a_beta, a_gate, b_cq, b_ckv, b_kpe, b_gate, c_x, c_gate,
     d_q, d_kv, d_gl, d_gate) = range(len(IN_SIZES))
    for l in range(w_ref.shape[1]):
        seg = lambda k: w_ref[o[k]:o[k + 1], l, :]
        off = 0
        for k in (a_qkv, a_gate, b_gate, c_x, c_gate, d_q, d_gate, d_kv, b_cq):
            o_ref[l, off:off + IN_SIZES[k], :] = seg(k).astype(BF16)
            off += IN_SIZES[k]
        assert off == OFF_MISC
        zeros = jnp.zeros((128 - MLA_ROPE - 2 * N_HEADS - 3 * N_HEADS, w_ref.shape[2]), F32)
        misc = jnp.concatenate([seg(b_kpe), seg(a_decay), seg(a_beta), seg(d_gl), zeros], axis=0)
        o_ref[l, OFF_MISC:OFF_MISC + 128, :] = misc.astype(BF16)
        o_ref[l, OFF_BCKV:OFF_BCKV + MLA_KV_RANK, :] = seg(b_ckv).astype(BF16)


def _reorder_w_in(w_in):
    depth, d, d_in = w_in.shape
    tc = 128
    w_t = jnp.transpose(w_in, (2, 0, 1))
    return pl.pallas_call(
        _reorder_w_in_kernel,
        grid=(d // tc,),
        in_specs=[pl.BlockSpec((d_in, depth, tc), lambda i: (0, 0, i))],
        out_specs=pl.BlockSpec((depth, D_PROJ, tc), lambda i: (0, 0, i)),
        out_shape=jax.ShapeDtypeStruct((depth, D_PROJ, d), BF16),
        compiler_params=_cparams(("parallel",)),
        name="reorder_w_in",
    )(w_t)


def kernel(x, positions, norm_g, w_in, w_out, gdn_conv_w, gdn_a_log, gdn_dt_bias, gdn_norm_g,
           mla_q_norm_g, mla_w_uq, mla_kv_norm_g, mla_w_ukv, mla_qk_norm_g,
           lru_conv_w, lru_conv_b, lru_w_r, lru_b_r, lru_w_i, lru_b_i, lru_lambda,
           nsa_q_norm_g, nsa_k_norm_g, nsa_cmp_pe, nsa_cmp_w1, nsa_cmp_b1, nsa_cmp_w2, nsa_cmp_b2):
    b, s, d = x.shape
    depth = w_in.shape[0]
    w_in_r = _reorder_w_in(w_in)
    w_out_b = w_out.astype(BF16)
    posf = positions.astype(F32)
    rope = _rope_tables(jnp.broadcast_to(posf[:, :, None], (b, s, 128)))
    nc = s // CMP_STRIDE
    pos_end = jnp.pad(posf.reshape(b, nc, CMP_STRIDE)[:, 1:, CMP_STRIDE - 1], ((0, 0), (0, 1)))
    posc = jnp.broadcast_to(pos_end[:, :, None], (b, nc, 128))

    x2d = x.reshape(b * s, d)
    for l in range(depth):
        proj = _inproj(x2d, norm_g[l].reshape(1, d), w_in_r, l).reshape(b, s, D_PROJ)
        (y_a,), (y_c,), (q_b, k_b, v_b), (q_d, ks, vs, kw, vw, kc_raw, vc_raw) = _token_block_call(
            [dict(_gdn_part(proj, gdn_conv_w[l], gdn_a_log[l], gdn_dt_bias[l], gdn_norm_g[l]),
                  host=True),
             _lru_part(proj, lru_conv_w[l], lru_conv_b[l], lru_w_r[l], lru_b_r[l], lru_w_i[l],
                       lru_b_i[l], lru_lambda[l]),
             _mla_prep_part(proj, rope, mla_q_norm_g[l], mla_w_uq[l], mla_kv_norm_g[l],
                            mla_w_ukv[l], mla_qk_norm_g[l]),
             _nsa_prep_part(proj, rope, nsa_q_norm_g[l], nsa_k_norm_g[l])],
            b, s, _token_block(s), "token_mixers")
        y_b = _mla_flash(q_b, k_b, v_b, proj)
        k_c, v_c = _nsa_compress(kc_raw, vc_raw, posc, nsa_cmp_pe[l], nsa_cmp_w1[l],
                                 nsa_cmp_b1[l], nsa_cmp_w2[l], nsa_cmp_b2[l], nsa_k_norm_g[l, 0])
        y_d = _nsa_attn(q_d, k_c, v_c, ks, vs, kw, vw, proj)
        ys = [y.reshape(b * s, GROUP_W) for y in (y_a, y_b, y_c, y_d)]
        x2d = _outproj(ys, w_out_b[l], x2d)
    return x2d.reshape(b, s, d)
```

```python
import functools
import math

import numpy as np
import jax
import jax.numpy as jnp
from jax import lax
from jax.experimental import pallas as pl
from jax.experimental.pallas import tpu as pltpu

F32 = jnp.float32
BF16 = jnp.bfloat16
HIGHEST = lax.Precision.HIGHEST

D_MODEL = 2048
GROUP_W = 512
HEAD_DIM = 128
N_HEADS = 4
RMS_EPS = 1e-6
ROPE_THETA = 10000.0
NEG = -1e30
LOG2E = math.log2(math.e)
GDN_CONV = 4
GDN_CHUNK = 128
MLA_Q_RANK = 384
MLA_KV_RANK = 256
MLA_NOPE = 128
MLA_ROPE = 64
LRU_CONV = 4
LRU_C = 8.0
CMP_LEN = 32
CMP_STRIDE = 16
CMP_HIDDEN = 256
SEL_LEN = 64
SEL_TOPK = 16
WINDOW = 512
FORCE = 1e9
Q_BLOCK = 256

IN_SIZES = (3 * GROUP_W, N_HEADS, N_HEADS, GROUP_W,
            MLA_Q_RANK, MLA_KV_RANK, MLA_ROPE, GROUP_W,
            GROUP_W, GROUP_W,
            GROUP_W, 6 * HEAD_DIM, 3 * N_HEADS, GROUP_W)

OFF_AQ, OFF_AK, OFF_AV, OFF_AG = 0, 512, 1024, 1536
OFF_BG, OFF_CX, OFF_CG, OFF_DQ, OFF_DG = 2048, 2560, 3072, 3584, 4096
OFF_DKV, OFF_BCQ, OFF_MISC, OFF_BCKV = 4608, 5376, 5760, 5888
D_PROJ = 6144
MISC_KPE, MISC_DECAY, MISC_BETA, MISC_GL = 0, 64, 68, 72

VMEM_LIMIT = 56 * 1024 * 1024


def _cparams(sem):
    return pltpu.CompilerParams(dimension_semantics=sem, vmem_limit_bytes=VMEM_LIMIT)


def _dot(a, b, precision=None):
    return lax.dot_general(a, b, (((1,), (0,)), ((), ())), precision=precision,
                           preferred_element_type=F32)


def _dot_nt(a, b, precision=None):
    return lax.dot_general(a, b, (((1,), (1,)), ((), ())), precision=precision,
                           preferred_element_type=F32)


def _bdot(a, b):
    return _dot(a.astype(BF16), b.astype(BF16))


def _bdot_nt(a, b):
    return _dot_nt(a.astype(BF16), b.astype(BF16))


def _split_bf16(x):
    hi = x.astype(BF16)
    return hi, (x - hi.astype(F32)).astype(BF16)


def _silu(x):
    h = 0.5 * x
    return h + h * jnp.tanh(h)


def _softplus(x):
    return jnp.maximum(x, 0.0) + jnp.log1p(jnp.exp(-jnp.abs(x)))


def _rms(x, n):
    return x * lax.rsqrt(jnp.sum(x * x, axis=-1, keepdims=True) * (1.0 / n) + RMS_EPS)


def _rms_mxu(x, n):
    w = x.shape[1]
    ssq = _dot((x * x).astype(BF16), jnp.ones((w, 128), BF16))
    r = lax.rsqrt(ssq * (1.0 / n) + RMS_EPS)
    if w == 128:
        return x * r
    return jnp.concatenate([x[:, j * 128:(j + 1) * 128] * r for j in range(w // 128)], axis=1)


def _sigmoid(x):
    return 0.5 + 0.5 * jnp.tanh(0.5 * x)


def _inproj_kernel(x_ref, g_ref, w_ref, o_ref, h_ref, *, tm):
    @pl.when(pl.program_id(1) == 0)
    def _():
        rows = min(256, tm)
        for r in range(tm // rows):
            x = x_ref[r * rows:(r + 1) * rows, :]
            h_ref[r * rows:(r + 1) * rows, :] = (_rms(x, D_MODEL) * g_ref[...]).astype(BF16)

    o_ref[...] = _dot_nt(h_ref[...], w_ref[...])


def _inproj(x2d, g, w_t, layer):
    m = x2d.shape[0]
    tm = min(1024, m)
    tn = 1024
    return pl.pallas_call(
        functools.partial(_inproj_kernel, tm=tm),
        grid=(m // tm, D_PROJ // tn),
        in_specs=[pl.BlockSpec((tm, D_MODEL), lambda i, j: (i, 0)),
                  pl.BlockSpec((1, D_MODEL), lambda i, j: (0, 0)),
                  pl.BlockSpec((None, tn, D_MODEL), lambda i, j: (layer, j, 0))],
        out_specs=pl.BlockSpec((tm, tn), lambda i, j: (i, j)),
        out_shape=jax.ShapeDtypeStruct((m, D_PROJ), F32),
        scratch_shapes=[pltpu.VMEM((tm, D_MODEL), BF16)],
        compiler_params=_cparams(("parallel", "arbitrary")),
        name="inproj",
    )(x2d, g, w_t)


def _outproj_kernel(ya_ref, yb_ref, yc_ref, yd_ref, wa_ref, wb_ref, wc_ref, wd_ref, x_ref, o_ref):
    acc = x_ref[...]
    for y_ref, w_ref in ((ya_ref, wa_ref), (yb_ref, wb_ref), (yc_ref, wc_ref), (yd_ref, wd_ref)):
        acc = acc + _dot(y_ref[...].astype(BF16), w_ref[...])
    o_ref[...] = acc


def _outproj(ys, w, x2d):
    m = x2d.shape[0]
    tm = min(1024, m)
    tn = 1024
    y_specs = [pl.BlockSpec((tm, GROUP_W), lambda i, j: (i, 0)) for _ in range(4)]
    w_specs = [pl.BlockSpec((GROUP_W, tn), functools.partial(lambda i, j, g: (g, j), g=g))
               for g in range(4)]
    return pl.pallas_call(
        _outproj_kernel,
        grid=(m // tm, D_MODEL // tn),
        in_specs=y_specs + w_specs + [pl.BlockSpec((tm, tn), lambda i, j: (i, j))],
        out_specs=pl.BlockSpec((tm, tn), lambda i, j: (i, j)),
        out_shape=jax.ShapeDtypeStruct((m, D_MODEL), F32),
        compiler_params=_cparams(("parallel", "arbitrary")),
        name="outproj",
    )(*ys, w, w, w, w, x2d)


def _causal_conv4(xin, ext_ref, w):
    t = xin.shape[0]
    ext_ref[8:8 + t, :] = xin
    ext = ext_ref[...]
    acc = ext * w[0:1]
    for j in range(1, 4):
        acc = pltpu.roll(acc, 1, 0) + ext * w[j:j + 1]
    ext_ref[0:8, :] = xin[t - 8:t]
    return acc[8:8 + t]


def _token_block_call(parts, b, s, t, name):
    n_in = [len(p["in_specs"]) for p in parts]
    n_out = [len(p["out_specs"]) for p in parts]
    n_scr = [len(p["scratch"]) for p in parts]

    def split(refs, counts):
        groups, pos = [], 0
        for n in counts:
            groups.append(refs[pos:pos + n])
            pos += n
        return groups

    def body(*refs):
        ins = split(refs[:sum(n_in)], n_in)
        outs = split(refs[sum(n_in):sum(n_in) + sum(n_out)], n_out)
        scrs = split(refs[sum(n_in) + sum(n_out):], n_scr)

        @pl.when(pl.program_id(1) == 0)
        def _():
            for p, scr in zip(parts, scrs):
                if p.get("init") is not None:
                    p["init"](*scr)

        calls = [functools.partial(p["body"], *i, *o, *scr)
                 for p, i, o, scr in zip(parts, ins, outs, scrs)]
        hosts = [c for p, c in zip(parts, calls) if p.get("host")]
        guests = [c for p, c in zip(parts, calls) if not p.get("host")]

        def run_guests():
            for c in guests:
                c()

        if hosts:
            (host,) = hosts
            host(interleave=run_guests)
        else:
            run_guests()

    res = pl.pallas_call(
        body,
        grid=(b, s // t),
        in_specs=[x for p in parts for x in p["in_specs"]],
        out_specs=[x for p in parts for x in p["out_specs"]],
        out_shape=[x for p in parts for x in p["out_shape"]],
        scratch_shapes=[x for p in parts for x in p["scratch"]],
        compiler_params=_cparams(("parallel", "arbitrary")),
        name=name,
    )(*[a for p in parts for a in p["args"]])
    return split(list(res), n_out)


def _gdn_init(hist_ref, state_ref):
    hist_ref[:, 0:8, :] = jnp.zeros((3, 8, GROUP_W), F32)
    state_ref[...] = jnp.zeros_like(state_ref)


def _gdn_kernel(q_ref, k_ref, v_ref, gate_ref, misc_ref, cw_ref, alog_ref, dtb_ref, ng_ref,
                o_ref, hist_ref, state_ref, *, T, interleave=None):
    C = GDN_CHUNK
    cw = cw_ref[...]
    conv = []
    for idx, ref in enumerate((q_ref, k_ref, v_ref)):
        xin = ref[0]
        acc = _causal_conv4(xin, hist_ref.at[idx], cw[:, idx * GROUP_W:(idx + 1) * GROUP_W])
        conv.append(_silu(acc))
    qc, kc, vc = conv

    misc = misc_ref[0]
    gfull = -jnp.exp(alog_ref[...]) * _softplus(misc + dtb_ref[...])
    betafull = jax.nn.sigmoid(misc)
    gate = gate_ref[0]

    row = lax.broadcasted_iota(jnp.int32, (C, C), 0)
    col = lax.broadcasted_iota(jnp.int32, (C, C), 1)
    lower = row >= col
    strict = row > col
    ltri = lower.astype(F32)
    row2 = lax.broadcasted_iota(jnp.int32, (C, 2 * C), 0)
    lane2 = lax.broadcasted_iota(jnp.int32, (C, 2 * C), 1)
    left = lane2 < C
    col2 = lane2 & (C - 1)
    leaf_bits = 4
    same_blk = {bits: (row2 >> bits) == (col2 >> bits) for bits in range(leaf_bits, 8)}
    assert C == HEAD_DIM == 1 << 7

    def pairdot(x, y):
        yb = y.astype(BF16)
        zero = jnp.zeros_like(yb)
        bd = jnp.concatenate([jnp.where(left, yb, zero), jnp.where(left, zero, yb)], axis=0)
        return _dot(x.astype(BF16), bd)

    n_chunks = T // C
    hds = []
    for c in range(n_chunks):
        r0 = c * C
        gall = _dot(ltri, gfull[r0:r0 + C], HIGHEST)
        hd = []
        hds.append(hd)
        for h in range(N_HEADS):
            l0 = h * HEAD_DIM
            q = qc[r0:r0 + C, l0:l0 + HEAD_DIM]
            k = kc[r0:r0 + C, l0:l0 + HEAD_DIM]
            v = vc[r0:r0 + C, l0:l0 + HEAD_DIM]
            ones = jnp.ones((HEAD_DIM, 128), BF16)
            q = q * (lax.rsqrt(_dot((q * q).astype(BF16), ones) + RMS_EPS) * (HEAD_DIM ** -0.5))
            k = k * lax.rsqrt(_dot((k * k).astype(BF16), ones) + RMS_EPS)
            beta = betafull[r0:r0 + C, MISC_BETA + h:MISC_BETA + h + 1]
            gc = jnp.broadcast_to(gall[:, MISC_DECAY + h:MISC_DECAY + h + 1], (C, C))
            decay = jnp.exp(jnp.where(lower, gc - gc.T, NEG))
            kb = k * beta
            a = jnp.where(strict, _bdot_nt(kb, k) * decay, 0.0)
            hd.append((q, k, v, beta, gc, decay, kb, a))

    if interleave is not None:
        interleave()

    a2s = [jnp.concatenate([hds[c][h0][7], hds[c][h0 + 1][7]], axis=1)
           for c in range(n_chunks) for h0 in range(0, N_HEADS, 2)]
    ps = [jnp.where(same_blk[leaf_bits], -a2, 0.0) for a2 in a2s]
    tm2 = list(ps)
    for _ in range(leaf_bits - 1):
        ps = [pairdot(p, p) for p in ps]
        tm2 = [tm + p + pairdot(tm, p) for tm, p in zip(tm2, ps)]
    for bits in range(leaf_bits + 1, 8):
        join = same_blk[bits] & jnp.logical_not(same_blk[bits - 1])
        tas = [jnp.where(join, a2, 0.0) for a2 in a2s]
        tas = [off + pairdot(tm, off) for tm, off in zip(tm2, tas)]
        tm2 = [tm - (ta + pairdot(ta, tm)) for tm, ta in zip(tm2, tas)]

    for c in range(n_chunks):
        r0 = c * C
        hd = hds[c]
        tms = []
        for tm in tm2[2 * c:2 * c + 2]:
            tms += [tm[:, :C], tm[:, C:]]
        for h in range(N_HEADS):
            l0 = h * HEAD_DIM
            q, k, v, beta, gc, decay, kb, _ = hd[h]
            eg = jnp.exp(gc)
            rhs = jnp.concatenate([v * beta, kb * eg], axis=-1)
            uw = rhs + _bdot(tms[h], rhs)
            u = uw[:, :HEAD_DIM]
            w = uw[:, HEAD_DIM:]
            qk = _bdot_nt(q, k) * decay
            q_dec = q * eg
            g_last = gc[C - 1:C, :]
            k_end = k * jnp.exp(g_last - gc)
            c_dec = jnp.exp(g_last)
            state = state_ref[h]
            v_new = u - _bdot(w, state)
            o = _bdot(q_dec, state) + _bdot(qk, v_new)
            state_ref[h] = state * c_dec + lax.dot_general(
                k_end.astype(BF16), v_new.astype(BF16), (((0,), (0,)), ((), ())),
                preferred_element_type=F32)
            o = _rms(o, HEAD_DIM) * ng_ref[...]
            o_ref[0, r0:r0 + C, l0:l0 + HEAD_DIM] = (
                o * _silu(gate[r0:r0 + C, l0:l0 + HEAD_DIM])).astype(o_ref.dtype)


def _seg_spec(t, width, off):
    blk = off // width
    assert blk * width == off
    return pl.BlockSpec((1, t, width), lambda b, i: (b, i, blk))


def _token_block(s):
    return min(512, s)


def _gdn_part(proj, conv_w, a_log, dt_bias, norm_g):
    b, s, _ = proj.shape
    t = _token_block(s)
    pad = lambda vec, off: jnp.zeros((1, 128), F32).at[0, off:off + N_HEADS].set(vec)
    full = lambda shape: pl.BlockSpec(shape, lambda bb, i: (0,) * len(shape))
    return dict(
        body=functools.partial(_gdn_kernel, T=t),
        init=_gdn_init,
        in_specs=[_seg_spec(t, GROUP_W, OFF_AQ), _seg_spec(t, GROUP_W, OFF_AK),
                  _seg_spec(t, GROUP_W, OFF_AV), _seg_spec(t, GROUP_W, OFF_AG),
                  _seg_spec(t, 128, OFF_MISC),
                  full((GDN_CONV, 3 * GROUP_W)), full((1, 128)), full((1, 128)), full((1, 128))],
        out_specs=[pl.BlockSpec((1, t, GROUP_W), lambda bb, i: (bb, i, 0))],
        out_shape=[jax.ShapeDtypeStruct((b, s, GROUP_W), BF16)],
        scratch=[pltpu.VMEM((3, t + 8, GROUP_W), F32),
                 pltpu.VMEM((N_HEADS, HEAD_DIM, HEAD_DIM), F32)],
        args=(proj, proj, proj, proj, proj, conv_w, pad(a_log, MISC_DECAY),
              pad(dt_bias, MISC_DECAY), norm_g.reshape(1, HEAD_DIM)))


def _gdn(proj, *weights):
    b, s, _ = proj.shape
    return _token_block_call([_gdn_part(proj, *weights)], b, s, _token_block(s), "gdn")[0][0]


def _lru_init(hist_ref, h_ref):
    hist_ref[0:8, :] = jnp.zeros((8, GROUP_W), F32)
    h_ref[...] = jnp.zeros_like(h_ref)


def _lru_kernel(x_ref, gate_ref, cw_ref, cb_ref, wr_ref, br_ref, wi_ref, bi_ref, lam_ref,
                o_ref, hist_ref, h_ref, *, T):
    xin = x_ref[0]
    xc = _causal_conv4(xin, hist_ref, cw_ref[...]) + cb_ref[...]
    rs, igs = [], []
    for h in range(N_HEADS):
        xh = xc[:, h * HEAD_DIM:(h + 1) * HEAD_DIM].astype(BF16)
        rs.append(_dot(xh, wr_ref[h]))
        igs.append(_dot(xh, wi_ref[h]))
    r = _sigmoid(jnp.concatenate(rs, axis=-1) + br_ref[...])
    ig = _sigmoid(jnp.concatenate(igs, axis=-1) + bi_ref[...])
    log_a = -LRU_C * r * _softplus(-lam_ref[...])
    a = jnp.exp(log_a)
    bb = jnp.sqrt(-jnp.tanh(log_a) * (a * a + 1.0)) * (ig * xc)
    a3 = a.reshape(T // 8, 8, GROUP_W)
    b3 = bb.reshape(T // 8, 8, GROUP_W)
    sub = lax.broadcasted_iota(jnp.int32, (T // 8, 8, GROUP_W), 1)
    for d in (1, 2, 4):
        keep = sub >= d
        a_sh = jnp.where(keep, pltpu.roll(a3, d, 1), 1.0)
        b_sh = jnp.where(keep, pltpu.roll(b3, d, 1), 0.0)
        b3 = a3 * b_sh + b3
        a3 = a3 * a_sh
    h = h_ref[...]
    groups = []
    for g in range(T // 8):
        hg = b3[g] + a3[g] * h
        h = hg[7:8]
        groups.append(hg)
    h_ref[...] = h
    o_ref[0] = (jnp.concatenate(groups, axis=0) * _silu(gate_ref[0])).astype(o_ref.dtype)


def _lru_part(proj, conv_w, conv_b, w_r, b_r, w_i, b_i, lam):
    b, s, _ = proj.shape
    t = _token_block(s)
    full = lambda shape: pl.BlockSpec(shape, lambda bb, i: (0,) * len(shape))
    vec = lambda v: v.reshape(1, GROUP_W)
    return dict(
        body=functools.partial(_lru_kernel, T=t),
        init=_lru_init,
        in_specs=[_seg_spec(t, GROUP_W, OFF_CX), _seg_spec(t, GROUP_W, OFF_CG),
                  full((LRU_CONV, GROUP_W)), full((1, GROUP_W)),
                  full((N_HEADS, HEAD_DIM, HEAD_DIM)), full((1, GROUP_W)),
                  full((N_HEADS, HEAD_DIM, HEAD_DIM)), full((1, GROUP_W)), full((1, GROUP_W))],
        out_specs=[pl.BlockSpec((1, t, GROUP_W), lambda bb, i: (bb, i, 0))],
        out_shape=[jax.ShapeDtypeStruct((b, s, GROUP_W), BF16)],
        scratch=[pltpu.VMEM((t + 8, GROUP_W), F32), pltpu.VMEM((1, GROUP_W), F32)],
        args=(proj, proj, conv_w, vec(conv_b), w_r.astype(BF16), vec(b_r), w_i.astype(BF16),
              vec(b_i), vec(lam)))


def _lru(proj, *weights):
    b, s, _ = proj.shape
    return _token_block_call([_lru_part(proj, *weights)], b, s, _token_block(s), "rglru")[0][0]


def _inv_freq_row(d):
    i = np.arange(128) % (d // 2)
    return jnp.asarray((ROPE_THETA ** (-(2.0 * i) / d)).astype(np.float32).reshape(1, 128))


def _rope128(x, cos, sin_signed):
    return x * cos + pltpu.roll(x, 64, 1) * sin_signed


def _rope64(x, cos, sin_masked, lane):
    rot = jnp.where(lane < 32, -pltpu.roll(x, 96, 1), pltpu.roll(x, 32, 1))
    return x * cos + rot * sin_masked


def _rope_tables_kernel(pos_ref, invf_ref, o_ref):
    lane = lax.broadcasted_iota(jnp.int32, pos_ref.shape[1:], 1)
    ang = pos_ref[0] * invf_ref[...]
    c = jnp.cos(ang)
    s = jnp.sin(ang)
    c32, c96 = pltpu.roll(c, 32, 1), pltpu.roll(c, 96, 1)
    s32, s96 = pltpu.roll(s, 32, 1), pltpu.roll(s, 96, 1)
    o_ref[0, :, 0:128] = jnp.where(lane < 32, c, c32)
    o_ref[0, :, 128:256] = jnp.where(lane < 32, s, jnp.where(lane < MLA_ROPE, s32, 0.0))
    o_ref[0, :, 256:384] = jnp.where(lane < 64, c96, c32)
    o_ref[0, :, 384:512] = jnp.where(lane < 64, -s96, s32)


def _rope_tables(posb):
    b, s, _ = posb.shape
    t = min(512, s)
    full = lambda shape: pl.BlockSpec(shape, lambda bb, i: (0,) * len(shape))
    return pl.pallas_call(
        _rope_tables_kernel,
        grid=(b, s // t),
        in_specs=[pl.BlockSpec((1, t, 128), lambda bb, i: (bb, i, 0)), full((1, 128))],
        out_specs=pl.BlockSpec((1, t, 512), lambda bb, i: (bb, i, 0)),
        out_shape=jax.ShapeDtypeStruct((b, s, 512), F32),
        compiler_params=_cparams(("parallel", "parallel")),
        name="rope_tables",
    )(posb, jnp.concatenate([_inv_freq_row(MLA_ROPE)[:, :32], _inv_freq_row(HEAD_DIM)[:, :64],
                             jnp.zeros((1, 32), F32)], axis=1))


def _mla_prep_kernel(cq_ref, ckv_ref, misc_ref, rope_ref, qng_ref, wuq_ref, kvng_ref, wukv_ref,
                     qgn_ref, qgp_ref, kgn_ref, kgp_ref,
                     q_ref, k_ref, v_ref, *, T):
    scale = (MLA_NOPE + MLA_ROPE) ** -0.5 * LOG2E
    lane = lax.broadcasted_iota(jnp.int32, (T, 128), 1)
    cos = rope_ref[0, :, 0:128]
    sin = rope_ref[0, :, 128:256]

    cq = _rms(cq_ref[0], MLA_Q_RANK) * qng_ref[...]
    qf = _dot(cq.astype(BF16), wuq_ref[...])
    ckv = _rms(ckv_ref[0], MLA_KV_RANK) * kvng_ref[...]
    kvf = _dot(ckv.astype(BF16), wukv_ref[...])
    kpe = jnp.where(lane < MLA_ROPE, misc_ref[0], 0.0)
    kpe = _rope64(_rms(kpe, MLA_ROPE) * kgp_ref[...], cos, sin, lane)
    for h in range(N_HEADS):
        o = h * 256
        qn = _rms_mxu(qf[:, o:o + 128], MLA_NOPE) * qgn_ref[...]
        qp = _rope64(_rms_mxu(qf[:, o + 128:o + 256], MLA_ROPE) * qgp_ref[...], cos, sin, lane)
        q_ref[0, :, o:o + 128] = (qn * scale).astype(BF16)
        q_ref[0, :, o + 128:o + 256] = (qp * scale).astype(BF16)
        kn = _rms(kvf[:, o:o + 128], MLA_NOPE) * kgn_ref[...]
        k_ref[0, :, o:o + 128] = kn.astype(BF16)
        k_ref[0, :, o + 128:o + 256] = kpe.astype(BF16)
        v_ref[0, :, h * 128:(h + 1) * 128] = kvf[:, o + 128:o + 256].astype(BF16)


def _mla_prep_part(proj, rope, q_norm_g, w_uq, kv_norm_g, w_ukv, qk_norm_g):
    b, s, _ = proj.shape
    t = _token_block(s)
    wuq = jnp.pad(w_uq, ((0, 0), (0, 0), (0, 64))).reshape(MLA_Q_RANK, N_HEADS * 256).astype(BF16)
    wukv = w_ukv.reshape(MLA_KV_RANK, N_HEADS * 256).astype(BF16)
    pad64 = lambda v: jnp.pad(v, (0, 64)).reshape(1, 128)
    full = lambda shape: pl.BlockSpec(shape, lambda bb, i: (0,) * len(shape))
    tok = lambda w: pl.BlockSpec((1, t, w), lambda bb, i: (bb, i, 0))
    return dict(
        body=functools.partial(_mla_prep_kernel, T=t),
        in_specs=[_seg_spec(t, MLA_Q_RANK, OFF_BCQ), _seg_spec(t, MLA_KV_RANK, OFF_BCKV),
                  _seg_spec(t, 128, OFF_MISC), pl.BlockSpec((1, t, 256), lambda bb, i: (bb, i, 0)),
                  full((1, MLA_Q_RANK)), full((MLA_Q_RANK, 1024)),
                  full((1, MLA_KV_RANK)), full((MLA_KV_RANK, 1024)),
                  full((1, 128)), full((1, 128)), full((1, 128)), full((1, 128))],
        out_specs=[tok(1024), tok(1024), tok(512)],
        out_shape=[jax.ShapeDtypeStruct((b, s, 1024), BF16),
                   jax.ShapeDtypeStruct((b, s, 1024), BF16),
                   jax.ShapeDtypeStruct((b, s, 512), BF16)],
        scratch=[],
        args=(proj, proj, proj, rope, q_norm_g.reshape(1, -1), wuq, kv_norm_g.reshape(1, -1), wukv,
              qk_norm_g[0, :MLA_NOPE].reshape(1, 128), pad64(qk_norm_g[0, MLA_NOPE:]),
              qk_norm_g[1, :MLA_NOPE].reshape(1, 128), pad64(qk_norm_g[1, MLA_NOPE:])))


def _mla_prep(proj, *rest):
    b, s, _ = proj.shape
    return _token_block_call([_mla_prep_part(proj, *rest)], b, s, _token_block(s), "mla_prep")[0]


ROW_GROUPS = 2


def _attn_scratch(m, kc):
    return [pltpu.VMEM((m, 1), F32), pltpu.VMEM((m, 128), F32), pltpu.VMEM((m, 128), F32),
            pltpu.VMEM((m, kc), F32), pltpu.VMEM((m, kc), F32)]


def _attend_causal(q, k_ref, v_ref, n_past, kc, scratch, mask_fn):
    m_ref, l_ref, acc_ref, sa_ref, sb_ref = scratch
    m_ref[...] = jnp.full_like(m_ref, NEG)
    l_ref[...] = jnp.zeros_like(l_ref)
    acc_ref[...] = jnp.zeros_like(acc_ref)
    g_rows = q.shape[0] // ROW_GROUPS

    def scores(c, s_ref):
        k_blk = k_ref[0, pl.ds(pl.multiple_of(c * kc, kc), kc), :]
        for g in range(ROW_GROUPS):
            rows = slice(g * g_rows, (g + 1) * g_rows)
            s_ref[rows, :] = _dot_nt(q[rows], k_blk)

    def consume(c, s_ref, masked):
        k0 = pl.multiple_of(c * kc, kc)
        v_blk = v_ref[0, pl.ds(k0, kc), :]
        for g in range(ROW_GROUPS):
            rows = slice(g * g_rows, (g + 1) * g_rows)
            s = s_ref[rows, :]
            if masked:
                s = mask_fn(s, g * g_rows, k0)
            m_old = m_ref[rows, :]
            m_new = jnp.maximum(m_old, jnp.max(s, axis=-1, keepdims=True))
            alpha = jnp.exp2(m_old - m_new)
            p = jnp.exp2(s - m_new)
            part = p[:, 0:128]
            for j in range(1, kc // 128):
                part = part + p[:, j * 128:(j + 1) * 128]
            l_ref[rows, :] = alpha * l_ref[rows, :] + part
            acc_ref[rows, :] = alpha * acc_ref[rows, :] + _dot(p.astype(BF16), v_blk)
            m_ref[rows, :] = m_new

    def past_pair(j, carry):
        c = 2 * j
        scores(c + 1, sb_ref)
        consume(c, sa_ref, False)
        scores(c + 2, sa_ref)
        consume(c + 1, sb_ref, False)
        return carry

    scores(0, sa_ref)
    lax.fori_loop(0, n_past // 2, past_pair, 0)

    @pl.when(n_past % 2 == 0)
    def _():
        consume(n_past, sa_ref, True)

    @pl.when(n_past % 2 == 1)
    def _():
        scores(n_past, sb_ref)
        consume(n_past - 1, sa_ref, False)
        consume(n_past, sb_ref, True)

    return acc_ref[...] / jnp.sum(l_ref[...], axis=-1, keepdims=True)


def _mla_flash_kernel(q_ref, k_ref, v_ref, gate_ref, o_ref, *scratch, tq, kc):
    i = pl.program_id(2)
    q = q_ref[0]

    def causal(s, first_row, k0):
        row = lax.broadcasted_iota(jnp.int32, s.shape, 0) + (i * tq + first_row)
        col = lax.broadcasted_iota(jnp.int32, s.shape, 1) + k0
        return jnp.where(col <= row, s, NEG)

    n_full = (i * tq) // kc
    o = _attend_causal(q, k_ref, v_ref, n_full, kc, scratch, causal)
    o_ref[0] = (o * _silu(gate_ref[0])).astype(o_ref.dtype)


def _mla_flash(q, k, v, proj):
    b, s, _ = q.shape
    tq = min(1024, s)
    kc = min(1024, s)
    gate_blk = OFF_BG // 128
    return pl.pallas_call(
        functools.partial(_mla_flash_kernel, tq=tq, kc=kc),
        grid=(b, N_HEADS, s // tq),
        in_specs=[pl.BlockSpec((1, tq, 256), lambda bb, h, i: (bb, i, h)),
                  pl.BlockSpec((1, s, 256), lambda bb, h, i: (bb, 0, h)),
                  pl.BlockSpec((1, s, 128), lambda bb, h, i: (bb, 0, h)),
                  pl.BlockSpec((1, tq, 128), lambda bb, h, i: (bb, i, gate_blk + h))],
        out_specs=pl.BlockSpec((1, tq, 128), lambda bb, h, i: (bb, i, h)),
        out_shape=jax.ShapeDtypeStruct((b, s, GROUP_W), BF16),
        scratch_shapes=_attn_scratch(tq, kc),
        compiler_params=_cparams(("parallel", "parallel", "arbitrary")),
        name="mla_flash",
    )(q, k, v, proj)


def _nsa_prep_kernel(dq_ref, dkv_ref, rope_ref, qg_ref, kg_ref,
                     q_ref, ks_ref, vs_ref, kw_ref, vw_ref, kc_ref, vc_ref, *, T):
    scale = HEAD_DIM ** -0.5 * LOG2E
    lane = lax.broadcasted_iota(jnp.int32, (T, 128), 1)
    cos = rope_ref[0, :, 0:128]
    sin = rope_ref[0, :, 128:256]
    dq = dq_ref[0]
    for h in range(N_HEADS):
        qh = _rms_mxu(dq[:, h * 128:(h + 1) * 128], HEAD_DIM) * qg_ref[...]
        q_ref[0, :, h * 128:(h + 1) * 128] = _rope128(qh, cos, sin) * scale
    kv = dkv_ref[0]
    kc_ref[0] = kv[:, 0:128]
    vc_ref[0] = kv[:, 128:256]
    ks_ref[0, :, 0:128] = _rope128(_rms_mxu(kv[:, 256:384], HEAD_DIM) * kg_ref[1:2], cos,
                                   sin).astype(BF16)
    key_blk = (pl.program_id(1) * T + lax.broadcasted_iota(jnp.int32, (T, 128), 0)) >> 6
    ks_ref[0, :, 128:256] = (lane == key_blk).astype(BF16)
    vs_ref[0] = kv[:, 384:512].astype(BF16)
    kw_ref[0] = _rope128(_rms_mxu(kv[:, 512:640], HEAD_DIM) * kg_ref[2:3], cos, sin).astype(BF16)
    vw_ref[0] = kv[:, 640:768].astype(BF16)


def _nsa_prep_part(proj, rope, q_norm_g, k_norm_g):
    b, s, _ = proj.shape
    t = _token_block(s)
    full = lambda shape: pl.BlockSpec(shape, lambda bb, i: (0,) * len(shape))
    tok = lambda w: pl.BlockSpec((1, t, w), lambda bb, i: (bb, i, 0))
    sds = lambda w, dt: jax.ShapeDtypeStruct((b, s, w), dt)
    return dict(
        body=functools.partial(_nsa_prep_kernel, T=t),
        in_specs=[_seg_spec(t, GROUP_W, OFF_DQ), _seg_spec(t, 768, OFF_DKV),
                  pl.BlockSpec((1, t, 256), lambda bb, i: (bb, i, 1)),
                  full((1, 128)), full((3, 128))],
        out_specs=[tok(512), tok(256)] + [tok(128)] * 5,
        out_shape=[sds(512, F32), sds(256, BF16), sds(128, BF16), sds(128, BF16), sds(128, BF16),
                   sds(128, F32), sds(128, F32)],
        scratch=[],
        args=(proj, proj, rope, q_norm_g.reshape(1, HEAD_DIM), k_norm_g))


def _nsa_prep(proj, *rest):
    b, s, _ = proj.shape
    return _token_block_call([_nsa_prep_part(proj, *rest)], b, s, _token_block(s), "nsa_prep")[0]


def _nsa_cmp_kernel(kt_ref, vt_ref, pe_ref, w1_ref, b1_ref, w2_ref, b2_ref, kg_ref, pos_ref,
                    invf_ref, kc_ref, vc_ref, *, NC):
    half = CMP_STRIDE * HEAD_DIM
    outs = []
    for j, t_ref in enumerate((kt_ref, vt_ref)):
        t2 = t_ref[0].astype(BF16)
        first = _dot(t2, w1_ref[j, :half, :])
        second = pltpu.roll(_dot(t2, w1_ref[j, half:, :]), NC - 1, 0)
        pe8 = jnp.broadcast_to(pe_ref[j], (8, CMP_LEN * HEAD_DIM)).astype(BF16)
        bias = _dot(pe8, w1_ref[j])[0:1] + b1_ref[j]
        hid = _silu(first + second + bias)
        outs.append(_dot(hid.astype(BF16), w2_ref[j]) + b2_ref[j])
    k_c, v_c = outs
    lane = lax.broadcasted_iota(jnp.int32, (NC, 128), 1)
    ang = pos_ref[0] * invf_ref[...]
    sin = jnp.sin(ang)
    sin = jnp.where(lane < 64, -sin, sin)
    kc_ref[0] = _rope128(_rms(k_c, HEAD_DIM) * kg_ref[...], jnp.cos(ang), sin)
    vc_ref[0] = v_c


def _nsa_compress(kc_raw, vc_raw, posc, cmp_pe, cmp_w1, cmp_b1, cmp_w2, cmp_b2, kg0):
    b, s, _ = kc_raw.shape
    nc = s // CMP_STRIDE
    kt = kc_raw.reshape(b, nc, CMP_STRIDE * HEAD_DIM)
    vt = vc_raw.reshape(b, nc, CMP_STRIDE * HEAD_DIM)
    full = lambda shape: pl.BlockSpec(shape, lambda bb: (0,) * len(shape))
    per_b = lambda shape: pl.BlockSpec((1,) + shape, lambda bb: (bb, 0, 0))
    return pl.pallas_call(
        functools.partial(_nsa_cmp_kernel, NC=nc),
        grid=(b,),
        in_specs=[per_b((nc, CMP_STRIDE * HEAD_DIM)), per_b((nc, CMP_STRIDE * HEAD_DIM)),
                  full((2, 1, CMP_LEN * HEAD_DIM)), full((2, CMP_LEN * HEAD_DIM, CMP_HIDDEN)),
                  full((2, 1, CMP_HIDDEN)), full((2, CMP_HIDDEN, HEAD_DIM)),
                  full((2, 1, HEAD_DIM)), full((1, HEAD_DIM)), per_b((nc, 128)), full((1, 128))],
        out_specs=[per_b((nc, HEAD_DIM)), per_b((nc, HEAD_DIM))],
        out_shape=[jax.ShapeDtypeStruct((b, nc, HEAD_DIM), F32),
                   jax.ShapeDtypeStruct((b, nc, HEAD_DIM), F32)],
        compiler_params=_cparams(("parallel",)),
        name="nsa_compress",
    )(kt, vt, cmp_pe.reshape(2, 1, CMP_LEN * HEAD_DIM), cmp_w1.astype(BF16),
      cmp_b1.reshape(2, 1, CMP_HIDDEN), cmp_w2.astype(BF16), cmp_b2.reshape(2, 1, HEAD_DIM),
      kg0.reshape(1, HEAD_DIM), posc, _inv_freq_row(HEAD_DIM))


def _nsa_attn_kernel(q_ref, kc_ref, vc_ref, ks_ref, vs_ref, kw_ref, vw_ref, misc_ref, gate_ref,
                     o_ref, *scratch, S, NC, KC, WK):
    Q = Q_BLOCK
    R = N_HEADS
    i = pl.program_id(1)
    n_sel = S // SEL_LEN
    n_cmp = (S - CMP_LEN) // CMP_STRIDE + 1
    top_k = min(SEL_TOPK, n_sel)

    qf = q_ref[0]
    q4 = jnp.concatenate([qf[:, r * 128:(r + 1) * 128] for r in range(R)], axis=0)
    q4b = q4.astype(BF16)

    q_hi, q_lo = _split_bf16(q4)
    k_hi, k_lo = _split_bf16(kc_ref[0])
    s_c = (_dot_nt(jnp.concatenate([q_hi, q_lo], axis=1), jnp.concatenate([k_hi, k_hi], axis=1))
           + _dot_nt(q_hi, k_lo))
    t_c = i * Q + (lax.broadcasted_iota(jnp.int32, (R * Q, NC), 0) & (Q - 1))
    c_ix = lax.broadcasted_iota(jnp.int32, (R * Q, NC), 1)
    valid_c = (c_ix * CMP_STRIDE + (CMP_LEN - 1) <= t_c) & (c_ix < n_cmp)
    s_c = jnp.where(valid_c, s_c, NEG)
    p_c = jnp.where(valid_c, jnp.exp2(s_c - jnp.max(s_c, axis=-1, keepdims=True)), 0.0)
    p_c = p_c / jnp.maximum(jnp.sum(p_c, axis=-1, keepdims=True), 1e-30)
    o_c = _bdot(p_c, vc_ref[0])
    assert NC & (NC - 1) == 0
    c_o = (lax.broadcasted_iota(jnp.int32, (2 * NC, 128), 0) & (NC - 1)) * CMP_STRIDE
    n_o = lax.broadcasted_iota(jnp.int32, (2 * NC, 128), 1) * SEL_LEN
    overlap2 = ((c_o < n_o + SEL_LEN) & (c_o + (CMP_LEN - 1) >= n_o)).astype(BF16)
    imp4 = _dot(jnp.concatenate(_split_bf16(p_c), axis=1), overlap2)
    imp = (imp4[0:Q] + imp4[Q:2 * Q] + imp4[2 * Q:3 * Q] + imp4[3 * Q:4 * Q]).T

    NP = min(128, -(-n_sel // 8) * 8)
    t_q = i * Q + lax.broadcasted_iota(jnp.int32, (NP, Q), 1)
    n_ix = lax.broadcasted_iota(jnp.int32, (NP, Q), 0)
    cur = t_q >> 6
    valid_s = (n_ix * SEL_LEN <= t_q) & (n_ix < n_sel)
    forced = (n_ix == 0) | (n_ix == cur) | (n_ix == cur - 1)
    val = jnp.where(valid_s, imp[:NP], -1.0)
    val = jnp.where(forced & valid_s, -3.0, val)
    val = jnp.where(n_ix < n_sel, val, -2.0)
    sel_t = jnp.where(forced & valid_s, 1.0, 0.0)
    n_f = n_ix.astype(F32)
    for _ in range(max(top_k - 3, 0)):
        mx = jnp.max(val, axis=0, keepdims=True)
        first = jnp.min(jnp.where(val == mx, n_f, 1e9), axis=0, keepdims=True)
        hit = n_f == first
        sel_t = jnp.where(hit, 1.0, sel_t)
        val = jnp.where(hit, -3.0, val)
    sel_t = jnp.where(valid_s, sel_t, 0.0)
    if NP < 128:
        sel_t = jnp.concatenate([sel_t, jnp.zeros((128 - NP, Q), F32)], axis=0)
    w0 = pl.multiple_of(jnp.maximum(i * Q - WINDOW, 0), Q)
    hi_w = (i * Q - w0) + lax.broadcasted_iota(jnp.int32, (Q, WK), 0)
    col_w = lax.broadcasted_iota(jnp.int32, (Q, WK), 1)
    band = jnp.where((col_w <= hi_w) & (col_w > hi_w - WINDOW), 0.0, NEG)
    s_w = _dot_nt(q4b, kw_ref[0, pl.ds(w0, WK), :]) + jnp.concatenate([band] * R, axis=0)
    p_w = jnp.exp2(s_w - jnp.max(s_w, axis=-1, keepdims=True))
    o_w = (_dot(p_w.astype(BF16), vw_ref[0, pl.ds(w0, WK), :])
           / jnp.sum(p_w, axis=-1, keepdims=True))

    blk_bias = jnp.where(sel_t.T > 0.5, 0.0, NEG).astype(BF16)
    q_aug = jnp.concatenate([q4b, jnp.concatenate([blk_bias] * R, axis=0)], axis=1)

    def causal_mask(s, first_row, k0):
        t_k = i * Q + ((first_row + lax.broadcasted_iota(jnp.int32, s.shape, 0)) & (Q - 1))
        return jnp.where(k0 + lax.broadcasted_iota(jnp.int32, s.shape, 1) <= t_k, s, NEG)

    n_past = (i * Q) // KC
    o_s = _attend_causal(q_aug, ks_ref, vs_ref, n_past, KC, scratch, causal_mask)

    gates = jax.nn.sigmoid(misc_ref[0])
    gate = gate_ref[0]
    for r in range(R):
        g0 = gates[:, MISC_GL + 3 * r:MISC_GL + 3 * r + 1]
        g1 = gates[:, MISC_GL + 3 * r + 1:MISC_GL + 3 * r + 2]
        g2 = gates[:, MISC_GL + 3 * r + 2:MISC_GL + 3 * r + 3]
        rows = slice(r * Q, (r + 1) * Q)
        o = g0 * o_c[rows] + g1 * o_s[rows] + g2 * o_w[rows]
        o_ref[0, :, r * 128:(r + 1) * 128] = (
            o * _silu(gate[:, r * 128:(r + 1) * 128])).astype(o_ref.dtype)


def _nsa_attn(qf, k_c, v_c, ks, vs, kw, vw, proj):
    b, s, _ = qf.shape
    nc = s // CMP_STRIDE
    kc_len = min(1024, s)
    wk = min(WINDOW + Q_BLOCK, s)
    per_b = lambda n, w: pl.BlockSpec((1, n, w), lambda bb, i: (bb, 0, 0))
    return pl.pallas_call(
        functools.partial(_nsa_attn_kernel, S=s, NC=nc, KC=kc_len, WK=wk),
        grid=(b, s // Q_BLOCK),
        in_specs=[pl.BlockSpec((1, Q_BLOCK, GROUP_W), lambda bb, i: (bb, i, 0)),
                  per_b(nc, 128), per_b(nc, 128),
                  per_b(s, 256), per_b(s, 128), per_b(s, 128), per_b(s, 128),
                  _seg_spec(Q_BLOCK, 128, OFF_MISC), _seg_spec(Q_BLOCK, GROUP_W, OFF_DG)],
        out_specs=pl.BlockSpec((1, Q_BLOCK, GROUP_W), lambda bb, i: (bb, i, 0)),
        out_shape=jax.ShapeDtypeStruct((b, s, GROUP_W), BF16),
        scratch_shapes=_attn_scratch(N_HEADS * Q_BLOCK, kc_len),
        compiler_params=_cparams(("parallel", "arbitrary")),
        name="nsa_attn",
    )(qf, k_c, v_c, ks, vs, kw, vw, proj, proj)


def _reorder_w_in_kernel(w_ref, o_ref):
    o = np.concatenate([[0], np.cumsum(IN_SIZES)]).tolist()
    (a_qkv, a_decay, a_beta, a_gate, b_cq, b_ckv, b_kpe, b_gate, c_x, c_gate,
     d_q, d_kv, d_gl, d_gate) = range(len(IN_SIZES))
    for l in range(w_ref.shape[1]):
        seg = lambda k: w_ref[o[k]:o[k + 1], l, :]
        off = 0
        for k in (a_qkv, a_gate, b_gate, c_x, c_gate, d_q, d_gate, d_kv, b_cq):
            o_ref[l, off:off + IN_SIZES[k], :] = seg(k).astype(BF16)
            off += IN_SIZES[k]
        assert off == OFF_MISC
        zeros = jnp.zeros((128 - MLA_ROPE - 2 * N_HEADS - 3 * N_HEADS, w_ref.shape[2]), F32)
        misc = jnp.concatenate([seg(b_kpe), seg(a_decay), seg(a_beta), seg(d_gl), zeros], axis=0)
        o_ref[l, OFF_MISC:OFF_MISC + 128, :] = misc.astype(BF16)
        o_ref[l, OFF_BCKV:OFF_BCKV + MLA_KV_RANK, :] = seg(b_ckv).astype(BF16)


def _reorder_w_in(w_in):
    depth, d, d_in = w_in.shape
    tc = 128
    w_t = jnp.transpose(w_in, (2, 0, 1))
    return pl.pallas_call(
        _reorder_w_in_kernel,
        grid=(d // tc,),
        in_specs=[pl.BlockSpec((d_in, depth, tc), lambda i: (0, 0, i))],
        out_specs=pl.BlockSpec((depth, D_PROJ, tc), lambda i: (0, 0, i)),
        out_shape=jax.ShapeDtypeStruct((depth, D_PROJ, d), BF16),
        compiler_params=_cparams(("parallel",)),
        name="reorder_w_in",
    )(w_t)


def kernel(x, positions, norm_g, w_in, w_out, gdn_conv_w, gdn_a_log, gdn_dt_bias, gdn_norm_g,
           mla_q_norm_g, mla_w_uq, mla_kv_norm_g, mla_w_ukv, mla_qk_norm_g,
           lru_conv_w, lru_conv_b, lru_w_r, lru_b_r, lru_w_i, lru_b_i, lru_lambda,
           nsa_q_norm_g, nsa_k_norm_g, nsa_cmp_pe, nsa_cmp_w1, nsa_cmp_b1, nsa_cmp_w2, nsa_cmp_b2):
    b, s, d = x.shape
    depth = w_in.shape[0]
    w_in_r = _reorder_w_in(w_in)
    w_out_b = w_out.astype(BF16)
    posf = positions.astype(F32)
    rope = _rope_tables(jnp.broadcast_to(posf[:, :, None], (b, s, 128)))
    nc = s // CMP_STRIDE
    pos_end = jnp.pad(posf.reshape(b, nc, CMP_STRIDE)[:, 1:, CMP_STRIDE - 1], ((0, 0), (0, 1)))
    posc = jnp.broadcast_to(pos_end[:, :, None], (b, nc, 128))

    x2d = x.reshape(b * s, d)
    for l in range(depth):
        proj = _inproj(x2d, norm_g[l].reshape(1, d), w_in_r, l).reshape(b, s, D_PROJ)
        (y_a,), (y_c,), (q_b, k_b, v_b), (q_d, ks, vs, kw, vw, kc_raw, vc_raw) = _token_block_call(
            [dict(_gdn_part(proj, gdn_conv_w[l], gdn_a_log[l], gdn_dt_bias[l], gdn_norm_g[l]),
                  host=True),
             _lru_part(proj, lru_conv_w[l], lru_conv_b[l], lru_w_r[l], lru_b_r[l], lru_w_i[l],
                       lru_b_i[l], lru_lambda[l]),
             _mla_prep_part(proj, rope, mla_q_norm_g[l], mla_w_uq[l], mla_kv_norm_g[l],
                            mla_w_ukv[l], mla_qk_norm_g[l]),
             _nsa_prep_part(proj, rope, nsa_q_norm_g[l], nsa_k_norm_g[l])],
            b, s, _token_block(s), "token_mixers")
        y_b = _mla_flash(q_b, k_b, v_b, proj)
        k_c, v_c = _nsa_compress(kc_raw, vc_raw, posc, nsa_cmp_pe[l], nsa_cmp_w1[l],
                                 nsa_cmp_b1[l], nsa_cmp_w2[l], nsa_cmp_b2[l], nsa_k_norm_g[l, 0])
        y_d = _nsa_attn(q_d, k_c, v_c, ks, vs, kw, vw, proj)
        ys = [y.reshape(b * s, GROUP_W) for y in (y_a, y_b, y_c, y_d)]
        x2d = _outproj(ys, w_out_b[l], x2d)
    return x2d.reshape(b, s, d)
```

```python
import functools
import math

import numpy as np
import jax
import jax.numpy as jnp
from jax import lax
from jax.experimental import pallas as pl
from jax.experimental.pallas import tpu as pltpu

F32 = jnp.float32
BF16 = jnp.bfloat16
HIGHEST = lax.Precision.HIGHEST

D_MODEL = 2048
GROUP_W = 512
HEAD_DIM = 128
N_HEADS = 4
RMS_EPS = 1e-6
ROPE_THETA = 10000.0
NEG = -1e30
LOG2E = math.log2(math.e)
GDN_CONV = 4
GDN_CHUNK = 128
MLA_Q_RANK = 384
MLA_KV_RANK = 256
MLA_NOPE = 128
MLA_ROPE = 64
LRU_CONV = 4
LRU_C = 8.0
CMP_LEN = 32
CMP_STRIDE = 16
CMP_HIDDEN = 256
SEL_LEN = 64
SEL_TOPK = 16
WINDOW = 512
FORCE = 1e9
Q_BLOCK = 256

IN_SIZES = (3 * GROUP_W, N_HEADS, N_HEADS, GROUP_W,
            MLA_Q_RANK, MLA_KV_RANK, MLA_ROPE, GROUP_W,
            GROUP_W, GROUP_W,
            GROUP_W, 6 * HEAD_DIM, 3 * N_HEADS, GROUP_W)

OFF_AQ, OFF_AK, OFF_AV, OFF_AG = 0, 512, 1024, 1536
OFF_BG, OFF_CX, OFF_CG, OFF_DQ, OFF_DG = 2048, 2560, 3072, 3584, 4096
OFF_DKV, OFF_BCQ, OFF_MISC, OFF_BCKV = 4608, 5376, 5760, 5888
D_PROJ = 6144
MISC_KPE, MISC_DECAY, MISC_BETA, MISC_GL = 0, 64, 68, 72

VMEM_LIMIT = 56 * 1024 * 1024


def _cparams(sem):
    return pltpu.CompilerParams(dimension_semantics=sem, vmem_limit_bytes=VMEM_LIMIT)


def _dot(a, b, precision=None):
    return lax.dot_general(a, b, (((1,), (0,)), ((), ())), precision=precision,
                           preferred_element_type=F32)


def _dot_nt(a, b, precision=None):
    return lax.dot_general(a, b, (((1,), (1,)), ((), ())), precision=precision,
                           preferred_element_type=F32)


def _bdot(a, b):
    return _dot(a.astype(BF16), b.astype(BF16))


def _bdot_nt(a, b):
    return _dot_nt(a.astype(BF16), b.astype(BF16))


def _split_bf16(x):
    hi = x.astype(BF16)
    return hi, (x - hi.astype(F32)).astype(BF16)


def _silu(x):
    h = 0.5 * x
    return h + h * jnp.tanh(h)


def _softplus(x):
    return jnp.maximum(x, 0.0) + jnp.log1p(jnp.exp(-jnp.abs(x)))


def _rms(x, n):
    return x * lax.rsqrt(jnp.sum(x * x, axis=-1, keepdims=True) * (1.0 / n) + RMS_EPS)


def _rms_mxu(x, n):
    w = x.shape[1]
    ssq = _dot((x * x).astype(BF16), jnp.ones((w, 128), BF16))
    r = lax.rsqrt(ssq * (1.0 / n) + RMS_EPS)
    if w == 128:
        return x * r
    return jnp.concatenate([x[:, j * 128:(j + 1) * 128] * r for j in range(w // 128)], axis=1)


def _sigmoid(x):
    return 0.5 + 0.5 * jnp.tanh(0.5 * x)


def _inproj_kernel(x_ref, g_ref, w_ref, o_ref, h_ref, *, tm):
    @pl.when(pl.program_id(1) == 0)
    def _():
        rows = min(256, tm)
        for r in range(tm // rows):
            x = x_ref[r * rows:(r + 1) * rows, :]
            h_ref[r * rows:(r + 1) * rows, :] = (_rms(x, D_MODEL) * g_ref[...]).astype(BF16)

    o_ref[...] = _dot_nt(h_ref[...], w_ref[...])


def _inproj(x2d, g, w_t, layer):
    m = x2d.shape[0]
    tm = min(1024, m)
    tn = 1536
    return pl.pallas_call(
        functools.partial(_inproj_kernel, tm=tm),
        grid=(m // tm, D_PROJ // tn),
        in_specs=[pl.BlockSpec((tm, D_MODEL), lambda i, j: (i, 0)),
                  pl.BlockSpec((1, D_MODEL), lambda i, j: (0, 0)),
                  pl.BlockSpec((None, tn, D_MODEL), lambda i, j: (layer, j, 0))],
        out_specs=pl.BlockSpec((tm, tn), lambda i, j: (i, j)),
        out_shape=jax.ShapeDtypeStruct((m, D_PROJ), F32),
        scratch_shapes=[pltpu.VMEM((tm, D_MODEL), BF16)],
        compiler_params=_cparams(("parallel", "arbitrary")),
        name="inproj",
    )(x2d, g, w_t)


def _outproj_kernel(ya_ref, yb_ref, yc_ref, yd_ref, wa_ref, wb_ref, wc_ref, wd_ref, x_ref, o_ref):
    acc = x_ref[...]
    for y_ref, w_ref in ((ya_ref, wa_ref), (yb_ref, wb_ref), (yc_ref, wc_ref), (yd_ref, wd_ref)):
        acc = acc + _dot(y_ref[...].astype(BF16), w_ref[...])
    o_ref[...] = acc


def _outproj(ys, w, x2d):
    m = x2d.shape[0]
    tm = min(1024, m)
    tn = 1024
    y_specs = [pl.BlockSpec((tm, GROUP_W), lambda i, j: (i, 0)) for _ in range(4)]
    w_specs = [pl.BlockSpec((GROUP_W, tn), functools.partial(lambda i, j, g: (g, j), g=g))
               for g in range(4)]
    return pl.pallas_call(
        _outproj_kernel,
        grid=(m // tm, D_MODEL // tn),
        in_specs=y_specs + w_specs + [pl.BlockSpec((tm, tn), lambda i, j: (i, j))],
        out_specs=pl.BlockSpec((tm, tn), lambda i, j: (i, j)),
        out_shape=jax.ShapeDtypeStruct((m, D_MODEL), F32),
        compiler_params=_cparams(("parallel", "arbitrary")),
        name="outproj",
    )(*ys, w, w, w, w, x2d)


def _causal_conv4(xin, ext_ref, w):
    t = xin.shape[0]
    ext_ref[8:8 + t, :] = xin
    ext = ext_ref[...]
    acc = ext * w[0:1]
    for j in range(1, 4):
        acc = pltpu.roll(acc, 1, 0) + ext * w[j:j + 1]
    ext_ref[0:8, :] = xin[t - 8:t]
    return acc[8:8 + t]


def _token_block_call(parts, b, s, t, name):
    n_in = [len(p["in_specs"]) for p in parts]
    n_out = [len(p["out_specs"]) for p in parts]
    n_scr = [len(p["scratch"]) for p in parts]

    def split(refs, counts):
        groups, pos = [], 0
        for n in counts:
            groups.append(refs[pos:pos + n])
            pos += n
        return groups

    def body(*refs):
        ins = split(refs[:sum(n_in)], n_in)
        outs = split(refs[sum(n_in):sum(n_in) + sum(n_out)], n_out)
        scrs = split(refs[sum(n_in) + sum(n_out):], n_scr)

        @pl.when(pl.program_id(1) == 0)
        def _():
            for p, scr in zip(parts, scrs):
                if p.get("init") is not None:
                    p["init"](*scr)

        calls = [functools.partial(p["body"], *i, *o, *scr)
                 for p, i, o, scr in zip(parts, ins, outs, scrs)]
        hosts = [c for p, c in zip(parts, calls) if p.get("host")]
        guests = [c for p, c in zip(parts, calls) if not p.get("host")]

        def run_guests():
            for c in guests:
                c()

        if hosts:
            (host,) = hosts
            host(interleave=run_guests)
        else:
            run_guests()

    res = pl.pallas_call(
        body,
        grid=(b, s // t),
        in_specs=[x for p in parts for x in p["in_specs"]],
        out_specs=[x for p in parts for x in p["out_specs"]],
        out_shape=[x for p in parts for x in p["out_shape"]],
        scratch_shapes=[x for p in parts for x in p["scratch"]],
        compiler_params=_cparams(("parallel", "arbitrary")),
        name=name,
    )(*[a for p in parts for a in p["args"]])
    return split(list(res), n_out)


def _gdn_init(hist_ref, state_ref):
    hist_ref[:, 0:8, :] = jnp.zeros((3, 8, GROUP_W), F32)
    state_ref[...] = jnp.zeros_like(state_ref)


def _gdn_kernel(q_ref, k_ref, v_ref, gate_ref, misc_ref, cw_ref, alog_ref, dtb_ref, ng_ref,
                o_ref, hist_ref, state_ref, *, T, interleave=None):
    C = GDN_CHUNK
    cw = cw_ref[...]
    conv = []
    for idx, ref in enumerate((q_ref, k_ref, v_ref)):
        xin = ref[0]
        acc = _causal_conv4(xin, hist_ref.at[idx], cw[:, idx * GROUP_W:(idx + 1) * GROUP_W])
        conv.append(_silu(acc))
    qc, kc, vc = conv

    misc = misc_ref[0]
    gfull = -jnp.exp(alog_ref[...]) * _softplus(misc + dtb_ref[...])
    betafull = jax.nn.sigmoid(misc)
    gate = gate_ref[0]

    row = lax.broadcasted_iota(jnp.int32, (C, C), 0)
    col = lax.broadcasted_iota(jnp.int32, (C, C), 1)
    lower = row >= col
    strict = row > col
    ltri = lower.astype(F32)
    row2 = lax.broadcasted_iota(jnp.int32, (C, 2 * C), 0)
    lane2 = lax.broadcasted_iota(jnp.int32, (C, 2 * C), 1)
    left = lane2 < C
    col2 = lane2 & (C - 1)
    leaf_bits = 4
    same_blk = {bits: (row2 >> bits) == (col2 >> bits) for bits in range(leaf_bits, 8)}
    assert C == HEAD_DIM == 1 << 7

    def pairdot(x, y):
        yb = y.astype(BF16)
        zero = jnp.zeros_like(yb)
        bd = jnp.concatenate([jnp.where(left, yb, zero), jnp.where(left, zero, yb)], axis=0)
        return _dot(x.astype(BF16), bd)

    n_chunks = T // C
    hds = []
    for c in range(n_chunks):
        r0 = c * C
        gall = _dot(ltri, gfull[r0:r0 + C], HIGHEST)
        hd = []
        hds.append(hd)
        for h in range(N_HEADS):
            l0 = h * HEAD_DIM
            q = qc[r0:r0 + C, l0:l0 + HEAD_DIM]
            k = kc[r0:r0 + C, l0:l0 + HEAD_DIM]
            v = vc[r0:r0 + C, l0:l0 + HEAD_DIM]
            ones = jnp.ones((HEAD_DIM, 128), BF16)
            q = q * (lax.rsqrt(_dot((q * q).astype(BF16), ones) + RMS_EPS) * (HEAD_DIM ** -0.5))
            k = k * lax.rsqrt(_dot((k * k).astype(BF16), ones) + RMS_EPS)
            beta = betafull[r0:r0 + C, MISC_BETA + h:MISC_BETA + h + 1]
            gc = jnp.broadcast_to(gall[:, MISC_DECAY + h:MISC_DECAY + h + 1], (C, C))
            decay = jnp.exp(jnp.where(lower, gc - gc.T, NEG))
            kb = k * beta
            a = jnp.where(strict, _bdot_nt(kb, k) * decay, 0.0)
            hd.append((q, k, v, beta, gc, decay, kb, a))

    if interleave is not None:
        interleave()

    a2s = [jnp.concatenate([hds[c][h0][7], hds[c][h0 + 1][7]], axis=1)
           for c in range(n_chunks) for h0 in range(0, N_HEADS, 2)]
    ps = [jnp.where(same_blk[leaf_bits], -a2, 0.0) for a2 in a2s]
    tm2 = list(ps)
    for _ in range(leaf_bits - 1):
        ps = [pairdot(p, p) for p in ps]
        tm2 = [tm + p + pairdot(tm, p) for tm, p in zip(tm2, ps)]
    for bits in range(leaf_bits + 1, 8):
        join = same_blk[bits] & jnp.logical_not(same_blk[bits - 1])
        tas = [jnp.where(join, a2, 0.0) for a2 in a2s]
        tas = [off + pairdot(tm, off) for tm, off in zip(tm2, tas)]
        tm2 = [tm - (ta + pairdot(ta, tm)) for tm, ta in zip(tm2, tas)]

    for c in range(n_chunks):
        r0 = c * C
        hd = hds[c]
        tms = []
        for tm in tm2[2 * c:2 * c + 2]:
            tms += [tm[:, :C], tm[:, C:]]
        for h in range(N_HEADS):
            l0 = h * HEAD_DIM
            q, k, v, beta, gc, decay, kb, _ = hd[h]
            eg = jnp.exp(gc)
            rhs = jnp.concatenate([v * beta, kb * eg], axis=-1)
            uw = rhs + _bdot(tms[h], rhs)
            u = uw[:, :HEAD_DIM]
            w = uw[:, HEAD_DIM:]
            qk = _bdot_nt(q, k) * decay
            q_dec = q * eg
            g_last = gc[C - 1:C, :]
            k_end = k * jnp.exp(g_last - gc)
            c_dec = jnp.exp(g_last)
            state = state_ref[h]
            v_new = u - _bdot(w, state)
            o = _bdot(q_dec, state) + _bdot(qk, v_new)
            state_ref[h] = state * c_dec + lax.dot_general(
                k_end.astype(BF16), v_new.astype(BF16), (((0,), (0,)), ((), ())),
                preferred_element_type=F32)
            o = _rms(o, HEAD_DIM) * ng_ref[...]
            o_ref[0, r0:r0 + C, l0:l0 + HEAD_DIM] = (
                o * _silu(gate[r0:r0 + C, l0:l0 + HEAD_DIM])).astype(o_ref.dtype)


def _seg_spec(t, width, off):
    blk = off // width
    assert blk * width == off
    return pl.BlockSpec((1, t, width), lambda b, i: (b, i, blk))


def _token_block(s):
    return min(512, s)


def _gdn_part(proj, conv_w, a_log, dt_bias, norm_g):
    b, s, _ = proj.shape
    t = _token_block(s)
    pad = lambda vec, off: jnp.zeros((1, 128), F32).at[0, off:off + N_HEADS].set(vec)
    full = lambda shape: pl.BlockSpec(shape, lambda bb, i: (0,) * len(shape))
    return dict(
        body=functools.partial(_gdn_kernel, T=t),
        init=_gdn_init,
        in_specs=[_seg_spec(t, GROUP_W, OFF_AQ), _seg_spec(t, GROUP_W, OFF_AK),
                  _seg_spec(t, GROUP_W, OFF_AV), _seg_spec(t, GROUP_W, OFF_AG),
                  _seg_spec(t, 128, OFF_MISC),
                  full((GDN_CONV, 3 * GROUP_W)), full((1, 128)), full((1, 128)), full((1, 128))],
        out_specs=[pl.BlockSpec((1, t, GROUP_W), lambda bb, i: (bb, i, 0))],
        out_shape=[jax.ShapeDtypeStruct((b, s, GROUP_W), BF16)],
        scratch=[pltpu.VMEM((3, t + 8, GROUP_W), F32),
                 pltpu.VMEM((N_HEADS, HEAD_DIM, HEAD_DIM), F32)],
        args=(proj, proj, proj, proj, proj, conv_w, pad(a_log, MISC_DECAY),
              pad(dt_bias, MISC_DECAY), norm_g.reshape(1, HEAD_DIM)))


def _gdn(proj, *weights):
    b, s, _ = proj.shape
    return _token_block_call([_gdn_part(proj, *weights)], b, s, _token_block(s), "gdn")[0][0]


def _lru_init(hist_ref, h_ref):
    hist_ref[0:8, :] = jnp.zeros((8, GROUP_W), F32)
    h_ref[...] = jnp.zeros_like(h_ref)


def _lru_kernel(x_ref, gate_ref, cw_ref, cb_ref, wr_ref, br_ref, wi_ref, bi_ref, lam_ref,
                o_ref, hist_ref, h_ref, *, T):
    xin = x_ref[0]
    xc = _causal_conv4(xin, hist_ref, cw_ref[...]) + cb_ref[...]
    rs, igs = [], []
    for h in range(N_HEADS):
        xh = xc[:, h * HEAD_DIM:(h + 1) * HEAD_DIM].astype(BF16)
        rs.append(_dot(xh, wr_ref[h]))
        igs.append(_dot(xh, wi_ref[h]))
    r = _sigmoid(jnp.concatenate(rs, axis=-1) + br_ref[...])
    ig = _sigmoid(jnp.concatenate(igs, axis=-1) + bi_ref[...])
    log_a = -LRU_C * r * _softplus(-lam_ref[...])
    a = jnp.exp(log_a)
    bb = jnp.sqrt(-jnp.tanh(log_a) * (a * a + 1.0)) * (ig * xc)
    a3 = a.reshape(T // 8, 8, GROUP_W)
    b3 = bb.reshape(T // 8, 8, GROUP_W)
    sub = lax.broadcasted_iota(jnp.int32, (T // 8, 8, GROUP_W), 1)
    for d in (1, 2, 4):
        keep = sub >= d
        a_sh = jnp.where(keep, pltpu.roll(a3, d, 1), 1.0)
        b_sh = jnp.where(keep, pltpu.roll(b3, d, 1), 0.0)
        b3 = a3 * b_sh + b3
        a3 = a3 * a_sh
    h = h_ref[...]
    groups = []
    for g in range(T // 8):
        hg = b3[g] + a3[g] * h
        h = hg[7:8]
        groups.append(hg)
    h_ref[...] = h
    o_ref[0] = (jnp.concatenate(groups, axis=0) * _silu(gate_ref[0])).astype(o_ref.dtype)


def _lru_part(proj, conv_w, conv_b, w_r, b_r, w_i, b_i, lam):
    b, s, _ = proj.shape
    t = _token_block(s)
    full = lambda shape: pl.BlockSpec(shape, lambda bb, i: (0,) * len(shape))
    vec = lambda v: v.reshape(1, GROUP_W)
    return dict(
        body=functools.partial(_lru_kernel, T=t),
        init=_lru_init,
        in_specs=[_seg_spec(t, GROUP_W, OFF_CX), _seg_spec(t, GROUP_W, OFF_CG),
                  full((LRU_CONV, GROUP_W)), full((1, GROUP_W)),
                  full((N_HEADS, HEAD_DIM, HEAD_DIM)), full((1, GROUP_W)),
                  full((N_HEADS, HEAD_DIM, HEAD_DIM)), full((1, GROUP_W)), full((1, GROUP_W))],
        out_specs=[pl.BlockSpec((1, t, GROUP_W), lambda bb, i: (bb, i, 0))],
        out_shape=[jax.ShapeDtypeStruct((b, s, GROUP_W), BF16)],
        scratch=[pltpu.VMEM((t + 8, GROUP_W), F32), pltpu.VMEM((1, GROUP_W), F32)],
        args=(proj, proj, conv_w, vec(conv_b), w_r.astype(BF16), vec(b_r), w_i.astype(BF16),
              vec(b_i), vec(lam)))


def _lru(proj, *weights):
    b, s, _ = proj.shape
    return _token_block_call([_lru_part(proj, *weights)], b, s, _token_block(s), "rglru")[0][0]


def _inv_freq_row(d):
    i = np.arange(128) % (d // 2)
    return jnp.asarray((ROPE_THETA ** (-(2.0 * i) / d)).astype(np.float32).reshape(1, 128))


def _rope128(x, cos, sin_signed):
    return x * cos + pltpu.roll(x, 64, 1) * sin_signed


def _rope64(x, cos, sin_masked, lane):
    rot = jnp.where(lane < 32, -pltpu.roll(x, 96, 1), pltpu.roll(x, 32, 1))
    return x * cos + rot * sin_masked


def _rope_tables_kernel(pos_ref, invf_ref, o_ref):
    lane = lax.broadcasted_iota(jnp.int32, pos_ref.shape[1:], 1)
    ang = pos_ref[0] * invf_ref[...]
    c = jnp.cos(ang)
    s = jnp.sin(ang)
    c32, c96 = pltpu.roll(c, 32, 1), pltpu.roll(c, 96, 1)
    s32, s96 = pltpu.roll(s, 32, 1), pltpu.roll(s, 96, 1)
    o_ref[0, :, 0:128] = jnp.where(lane < 32, c, c32)
    o_ref[0, :, 128:256] = jnp.where(lane < 32, s, jnp.where(lane < MLA_ROPE, s32, 0.0))
    o_ref[0, :, 256:384] = jnp.where(lane < 64, c96, c32)
    o_ref[0, :, 384:512] = jnp.where(lane < 64, -s96, s32)


def _rope_tables(posb):
    b, s, _ = posb.shape
    t = min(512, s)
    full = lambda shape: pl.BlockSpec(shape, lambda bb, i: (0,) * len(shape))
    return pl.pallas_call(
        _rope_tables_kernel,
        grid=(b, s // t),
        in_specs=[pl.BlockSpec((1, t, 128), lambda bb, i: (bb, i, 0)), full((1, 128))],
        out_specs=pl.BlockSpec((1, t, 512), lambda bb, i: (bb, i, 0)),
        out_shape=jax.ShapeDtypeStruct((b, s, 512), F32),
        compiler_params=_cparams(("parallel", "parallel")),
        name="rope_tables",
    )(posb, jnp.concatenate([_inv_freq_row(MLA_ROPE)[:, :32], _inv_freq_row(HEAD_DIM)[:, :64],
                             jnp.zeros((1, 32), F32)], axis=1))


def _mla_prep_kernel(cq_ref, ckv_ref, misc_ref, rope_ref, qng_ref, wuq_ref, kvng_ref, wukv_ref,
                     qgn_ref, qgp_ref, kgn_ref, kgp_ref,
                     q_ref, k_ref, v_ref, *, T):
    scale = (MLA_NOPE + MLA_ROPE) ** -0.5 * LOG2E
    lane = lax.broadcasted_iota(jnp.int32, (T, 128), 1)
    cos = rope_ref[0, :, 0:128]
    sin = rope_ref[0, :, 128:256]

    cq = _rms(cq_ref[0], MLA_Q_RANK) * qng_ref[...]
    qf = _dot(cq.astype(BF16), wuq_ref[...])
    ckv = _rms(ckv_ref[0], MLA_KV_RANK) * kvng_ref[...]
    kvf = _dot(ckv.astype(BF16), wukv_ref[...])
    kpe = jnp.where(lane < MLA_ROPE, misc_ref[0], 0.0)
    kpe = _rope64(_rms(kpe, MLA_ROPE) * kgp_ref[...], cos, sin, lane)
    for h in range(N_HEADS):
        o = h * 256
        qn = _rms_mxu(qf[:, o:o + 128], MLA_NOPE) * qgn_ref[...]
        qp = _rope64(_rms_mxu(qf[:, o + 128:o + 256], MLA_ROPE) * qgp_ref[...], cos, sin, lane)
        q_ref[0, :, o:o + 128] = (qn * scale).astype(BF16)
        q_ref[0, :, o + 128:o + 256] = (qp * scale).astype(BF16)
        kn = _rms(kvf[:, o:o + 128], MLA_NOPE) * kgn_ref[...]
        k_ref[0, :, o:o + 128] = kn.astype(BF16)
        k_ref[0, :, o + 128:o + 256] = kpe.astype(BF16)
        v_ref[0, :, h * 128:(h + 1) * 128] = kvf[:, o + 128:o + 256].astype(BF16)


def _mla_prep_part(proj, rope, q_norm_g, w_uq, kv_norm_g, w_ukv, qk_norm_g):
    b, s, _ = proj.shape
    t = _token_block(s)
    wuq = jnp.pad(w_uq, ((0, 0), (0, 0), (0, 64))).reshape(MLA_Q_RANK, N_HEADS * 256).astype(BF16)
    wukv = w_ukv.reshape(MLA_KV_RANK, N_HEADS * 256).astype(BF16)
    pad64 = lambda v: jnp.pad(v, (0, 64)).reshape(1, 128)
    full = lambda shape: pl.BlockSpec(shape, lambda bb, i: (0,) * len(shape))
    tok = lambda w: pl.BlockSpec((1, t, w), lambda bb, i: (bb, i, 0))
    return dict(
        body=functools.partial(_mla_prep_kernel, T=t),
        in_specs=[_seg_spec(t, MLA_Q_RANK, OFF_BCQ), _seg_spec(t, MLA_KV_RANK, OFF_BCKV),
                  _seg_spec(t, 128, OFF_MISC), pl.BlockSpec((1, t, 256), lambda bb, i: (bb, i, 0)),
                  full((1, MLA_Q_RANK)), full((MLA_Q_RANK, 1024)),
                  full((1, MLA_KV_RANK)), full((MLA_KV_RANK, 1024)),
                  full((1, 128)), full((1, 128)), full((1, 128)), full((1, 128))],
        out_specs=[tok(1024), tok(1024), tok(512)],
        out_shape=[jax.ShapeDtypeStruct((b, s, 1024), BF16),
                   jax.ShapeDtypeStruct((b, s, 1024), BF16),
                   jax.ShapeDtypeStruct((b, s, 512), BF16)],
        scratch=[],
        args=(proj, proj, proj, rope, q_norm_g.reshape(1, -1), wuq, kv_norm_g.reshape(1, -1), wukv,
              qk_norm_g[0, :MLA_NOPE].reshape(1, 128), pad64(qk_norm_g[0, MLA_NOPE:]),
              qk_norm_g[1, :MLA_NOPE].reshape(1, 128), pad64(qk_norm_g[1, MLA_NOPE:])))


def _mla_prep(proj, *rest):
    b, s, _ = proj.shape
    return _token_block_call([_mla_prep_part(proj, *rest)], b, s, _token_block(s), "mla_prep")[0]


ROW_GROUPS = 2


def _attn_scratch(m, kc):
    return [pltpu.VMEM((m, 1), F32), pltpu.VMEM((m, 128), F32), pltpu.VMEM((m, 128), F32),
            pltpu.VMEM((m, kc), F32), pltpu.VMEM((m, kc), F32)]


def _attend_causal(q, k_ref, v_ref, n_past, kc, scratch, mask_fn):
    m_ref, l_ref, acc_ref, sa_ref, sb_ref = scratch
    m_ref[...] = jnp.full_like(m_ref, NEG)
    l_ref[...] = jnp.zeros_like(l_ref)
    acc_ref[...] = jnp.zeros_like(acc_ref)
    g_rows = q.shape[0] // ROW_GROUPS

    def scores(c, s_ref):
        k_blk = k_ref[0, pl.ds(pl.multiple_of(c * kc, kc), kc), :]
        for g in range(ROW_GROUPS):
            rows = slice(g * g_rows, (g + 1) * g_rows)
            s_ref[rows, :] = _dot_nt(q[rows], k_blk)

    def consume(c, s_ref, masked):
        k0 = pl.multiple_of(c * kc, kc)
        v_blk = v_ref[0, pl.ds(k0, kc), :]
        for g in range(ROW_GROUPS):
            rows = slice(g * g_rows, (g + 1) * g_rows)
            s = s_ref[rows, :]
            if masked:
                s = mask_fn(s, g * g_rows, k0)
            m_old = m_ref[rows, :]
            m_new = jnp.maximum(m_old, jnp.max(s, axis=-1, keepdims=True))
            alpha = jnp.exp2(m_old - m_new)
            p = jnp.exp2(s - m_new)
            part = p[:, 0:128]
            for j in range(1, kc // 128):
                part = part + p[:, j * 128:(j + 1) * 128]
            l_ref[rows, :] = alpha * l_ref[rows, :] + part
            acc_ref[rows, :] = alpha * acc_ref[rows, :] + _dot(p.astype(BF16), v_blk)
            m_ref[rows, :] = m_new

    def past_pair(j, carry):
        c = 2 * j
        scores(c + 1, sb_ref)
        consume(c, sa_ref, False)
        scores(c + 2, sa_ref)
        consume(c + 1, sb_ref, False)
        return carry

    scores(0, sa_ref)
    lax.fori_loop(0, n_past // 2, past_pair, 0)

    @pl.when(n_past % 2 == 0)
    def _():
        consume(n_past, sa_ref, True)

    @pl.when(n_past % 2 == 1)
    def _():
        scores(n_past, sb_ref)
        consume(n_past - 1, sa_ref, False)
        consume(n_past, sb_ref, True)

    return acc_ref[...] / jnp.sum(l_ref[...], axis=-1, keepdims=True)


def _mla_flash_kernel(q_ref, k_ref, v_ref, gate_ref, o_ref, *scratch, tq, kc):
    i = pl.program_id(2)
    q = q_ref[0]

    def causal(s, first_row, k0):
        row = lax.broadcasted_iota(jnp.int32, s.shape, 0) + (i * tq + first_row)
        col = lax.broadcasted_iota(jnp.int32, s.shape, 1) + k0
        return jnp.where(col <= row, s, NEG)

    n_full = (i * tq) // kc
    o = _attend_causal(q, k_ref, v_ref, n_full, kc, scratch, causal)
    o_ref[0] = (o * _silu(gate_ref[0])).astype(o_ref.dtype)


def _mla_flash(q, k, v, proj):
    b, s, _ = q.shape
    tq = min(1024, s)
    kc = min(1024, s)
    gate_blk = OFF_BG // 128
    return pl.pallas_call(
        functools.partial(_mla_flash_kernel, tq=tq, kc=kc),
        grid=(b, N_HEADS, s // tq),
        in_specs=[pl.BlockSpec((1, tq, 256), lambda bb, h, i: (bb, i, h)),
                  pl.BlockSpec((1, s, 256), lambda bb, h, i: (bb, 0, h)),
                  pl.BlockSpec((1, s, 128), lambda bb, h, i: (bb, 0, h)),
                  pl.BlockSpec((1, tq, 128), lambda bb, h, i: (bb, i, gate_blk + h))],
        out_specs=pl.BlockSpec((1, tq, 128), lambda bb, h, i: (bb, i, h)),
        out_shape=jax.ShapeDtypeStruct((b, s, GROUP_W), BF16),
        scratch_shapes=_attn_scratch(tq, kc),
        compiler_params=_cparams(("parallel", "parallel", "arbitrary")),
        name="mla_flash",
    )(q, k, v, proj)


def _nsa_prep_kernel(dq_ref, dkv_ref, rope_ref, qg_ref, kg_ref,
                     q_ref, ks_ref, vs_ref, kw_ref, vw_ref, kc_ref, vc_ref, *, T):
    scale = HEAD_DIM ** -0.5 * LOG2E
    lane = lax.broadcasted_iota(jnp.int32, (T, 128), 1)
    cos = rope_ref[0, :, 0:128]
    sin = rope_ref[0, :, 128:256]
    dq = dq_ref[0]
    for h in range(N_HEADS):
        qh = _rms_mxu(dq[:, h * 128:(h + 1) * 128], HEAD_DIM) * qg_ref[...]
        q_ref[0, :, h * 128:(h + 1) * 128] = _rope128(qh, cos, sin) * scale
    kv = dkv_ref[0]
    kc_ref[0] = kv[:, 0:128]
    vc_ref[0] = kv[:, 128:256]
    ks_ref[0, :, 0:128] = _rope128(_rms_mxu(kv[:, 256:384], HEAD_DIM) * kg_ref[1:2], cos,
                                   sin).astype(BF16)
    key_blk = (pl.program_id(1) * T + lax.broadcasted_iota(jnp.int32, (T, 128), 0)) >> 6
    ks_ref[0, :, 128:256] = (lane == key_blk).astype(BF16)
    vs_ref[0] = kv[:, 384:512].astype(BF16)
    kw_ref[0] = _rope128(_rms_mxu(kv[:, 512:640], HEAD_DIM) * kg_ref[2:3], cos, sin).astype(BF16)
    vw_ref[0] = kv[:, 640:768].astype(BF16)


def _nsa_prep_part(proj, rope, q_norm_g, k_norm_g):
    b, s, _ = proj.shape
    t = _token_block(s)
    full = lambda shape: pl.BlockSpec(shape, lambda bb, i: (0,) * len(shape))
    tok = lambda w: pl.BlockSpec((1, t, w), lambda bb, i: (bb, i, 0))
    sds = lambda w, dt: jax.ShapeDtypeStruct((b, s, w), dt)
    return dict(
        body=functools.partial(_nsa_prep_kernel, T=t),
        in_specs=[_seg_spec(t, GROUP_W, OFF_DQ), _seg_spec(t, 768, OFF_DKV),
                  pl.BlockSpec((1, t, 256), lambda bb, i: (bb, i, 1)),
                  full((1, 128)), full((3, 128))],
        out_specs=[tok(512), tok(256)] + [tok(128)] * 5,
        out_shape=[sds(512, F32), sds(256, BF16), sds(128, BF16), sds(128, BF16), sds(128, BF16),
                   sds(128, F32), sds(128, F32)],
        scratch=[],
        args=(proj, proj, rope, q_norm_g.reshape(1, HEAD_DIM), k_norm_g))


def _nsa_prep(proj, *rest):
    b, s, _ = proj.shape
    return _token_block_call([_nsa_prep_part(proj, *rest)], b, s, _token_block(s), "nsa_prep")[0]


def _nsa_cmp_kernel(kt_ref, vt_ref, pe_ref, w1_ref, b1_ref, w2_ref, b2_ref, kg_ref, pos_ref,
                    invf_ref, kc_ref, vc_ref, *, NC):
    half = CMP_STRIDE * HEAD_DIM
    outs = []
    for j, t_ref in enumerate((kt_ref, vt_ref)):
        t2 = t_ref[0].astype(BF16)
        first = _dot(t2, w1_ref[j, :half, :])
        second = pltpu.roll(_dot(t2, w1_ref[j, half:, :]), NC - 1, 0)
        pe8 = jnp.broadcast_to(pe_ref[j], (8, CMP_LEN * HEAD_DIM)).astype(BF16)
        bias = _dot(pe8, w1_ref[j])[0:1] + b1_ref[j]
        hid = _silu(first + second + bias)
        outs.append(_dot(hid.astype(BF16), w2_ref[j]) + b2_ref[j])
    k_c, v_c = outs
    lane = lax.broadcasted_iota(jnp.int32, (NC, 128), 1)
    ang = pos_ref[0] * invf_ref[...]
    sin = jnp.sin(ang)
    sin = jnp.where(lane < 64, -sin, sin)
    kc_ref[0] = _rope128(_rms(k_c, HEAD_DIM) * kg_ref[...], jnp.cos(ang), sin)
    vc_ref[0] = v_c


def _nsa_compress(kc_raw, vc_raw, posc, cmp_pe, cmp_w1, cmp_b1, cmp_w2, cmp_b2, kg0):
    b, s, _ = kc_raw.shape
    nc = s // CMP_STRIDE
    kt = kc_raw.reshape(b, nc, CMP_STRIDE * HEAD_DIM)
    vt = vc_raw.reshape(b, nc, CMP_STRIDE * HEAD_DIM)
    full = lambda shape: pl.BlockSpec(shape, lambda bb: (0,) * len(shape))
    per_b = lambda shape: pl.BlockSpec((1,) + shape, lambda bb: (bb, 0, 0))
    return pl.pallas_call(
        functools.partial(_nsa_cmp_kernel, NC=nc),
        grid=(b,),
        in_specs=[per_b((nc, CMP_STRIDE * HEAD_DIM)), per_b((nc, CMP_STRIDE * HEAD_DIM)),
                  full((2, 1, CMP_LEN * HEAD_DIM)), full((2, CMP_LEN * HEAD_DIM, CMP_HIDDEN)),
                  full((2, 1, CMP_HIDDEN)), full((2, CMP_HIDDEN, HEAD_DIM)),
                  full((2, 1, HEAD_DIM)), full((1, HEAD_DIM)), per_b((nc, 128)), full((1, 128))],
        out_specs=[per_b((nc, HEAD_DIM)), per_b((nc, HEAD_DIM))],
        out_shape=[jax.ShapeDtypeStruct((b, nc, HEAD_DIM), F32),
                   jax.ShapeDtypeStruct((b, nc, HEAD_DIM), F32)],
        compiler_params=_cparams(("parallel",)),
        name="nsa_compress",
    )(kt, vt, cmp_pe.reshape(2, 1, CMP_LEN * HEAD_DIM), cmp_w1.astype(BF16),
      cmp_b1.reshape(2, 1, CMP_HIDDEN), cmp_w2.astype(BF16), cmp_b2.reshape(2, 1, HEAD_DIM),
      kg0.reshape(1, HEAD_DIM), posc, _inv_freq_row(HEAD_DIM))


def _nsa_attn_kernel(q_ref, kc_ref, vc_ref, ks_ref, vs_ref, kw_ref, vw_ref, misc_ref, gate_ref,
                     o_ref, *scratch, S, NC, KC, WK):
    Q = Q_BLOCK
    R = N_HEADS
    i = pl.program_id(1)
    n_sel = S // SEL_LEN
    n_cmp = (S - CMP_LEN) // CMP_STRIDE + 1
    top_k = min(SEL_TOPK, n_sel)

    qf = q_ref[0]
    q4 = jnp.concatenate([qf[:, r * 128:(r + 1) * 128] for r in range(R)], axis=0)
    q4b = q4.astype(BF16)

    q_hi, q_lo = _split_bf16(q4)
    k_hi, k_lo = _split_bf16(kc_ref[0])
    s_c = (_dot_nt(jnp.concatenate([q_hi, q_lo], axis=1), jnp.concatenate([k_hi, k_hi], axis=1))
           + _dot_nt(q_hi, k_lo))
    t_c = i * Q + (lax.broadcasted_iota(jnp.int32, (R * Q, NC), 0) & (Q - 1))
    c_ix = lax.broadcasted_iota(jnp.int32, (R * Q, NC), 1)
    valid_c = (c_ix * CMP_STRIDE + (CMP_LEN - 1) <= t_c) & (c_ix < n_cmp)
    s_c = jnp.where(valid_c, s_c, NEG)
    p_c = jnp.where(valid_c, jnp.exp2(s_c - jnp.max(s_c, axis=-1, keepdims=True)), 0.0)
    p_c = p_c / jnp.maximum(jnp.sum(p_c, axis=-1, keepdims=True), 1e-30)
    o_c = _bdot(p_c, vc_ref[0])
    assert NC & (NC - 1) == 0
    c_o = (lax.broadcasted_iota(jnp.int32, (2 * NC, 128), 0) & (NC - 1)) * CMP_STRIDE
    n_o = lax.broadcasted_iota(jnp.int32, (2 * NC, 128), 1) * SEL_LEN
    overlap2 = ((c_o < n_o + SEL_LEN) & (c_o + (CMP_LEN - 1) >= n_o)).astype(BF16)
    imp4 = _dot(jnp.concatenate(_split_bf16(p_c), axis=1), overlap2)
    imp = (imp4[0:Q] + imp4[Q:2 * Q] + imp4[2 * Q:3 * Q] + imp4[3 * Q:4 * Q]).T

    NP = min(128, -(-n_sel // 8) * 8)
    t_q = i * Q + lax.broadcasted_iota(jnp.int32, (NP, Q), 1)
    n_ix = lax.broadcasted_iota(jnp.int32, (NP, Q), 0)
    cur = t_q >> 6
    valid_s = (n_ix * SEL_LEN <= t_q) & (n_ix < n_sel)
    forced = (n_ix == 0) | (n_ix == cur) | (n_ix == cur - 1)
    val = jnp.where(valid_s, imp[:NP], -1.0)
    val = jnp.where(forced & valid_s, -3.0, val)
    val = jnp.where(n_ix < n_sel, val, -2.0)
    sel_t = jnp.where(forced & valid_s, 1.0, 0.0)
    n_f = n_ix.astype(F32)
    for _ in range(max(top_k - 3, 0)):
        mx = jnp.max(val, axis=0, keepdims=True)
        first = jnp.min(jnp.where(val == mx, n_f, 1e9), axis=0, keepdims=True)
        hit = n_f == first
        sel_t = jnp.where(hit, 1.0, sel_t)
        val = jnp.where(hit, -3.0, val)
    sel_t = jnp.where(valid_s, sel_t, 0.0)
    if NP < 128:
        sel_t = jnp.concatenate([sel_t, jnp.zeros((128 - NP, Q), F32)], axis=0)
    w0 = pl.multiple_of(jnp.maximum(i * Q - WINDOW, 0), Q)
    hi_w = (i * Q - w0) + lax.broadcasted_iota(jnp.int32, (Q, WK), 0)
    col_w = lax.broadcasted_iota(jnp.int32, (Q, WK), 1)
    band = jnp.where((col_w <= hi_w) & (col_w > hi_w - WINDOW), 0.0, NEG)
    s_w = _dot_nt(q4b, kw_ref[0, pl.ds(w0, WK), :]) + jnp.concatenate([band] * R, axis=0)
    p_w = jnp.exp2(s_w - jnp.max(s_w, axis=-1, keepdims=True))
    o_w = (_dot(p_w.astype(BF16), vw_ref[0, pl.ds(w0, WK), :])
           / jnp.sum(p_w, axis=-1, keepdims=True))

    blk_bias = jnp.where(sel_t.T > 0.5, 0.0, NEG).astype(BF16)
    q_aug = jnp.concatenate([q4b, jnp.concatenate([blk_bias] * R, axis=0)], axis=1)

    def causal_mask(s, first_row, k0):
        t_k = i * Q + ((first_row + lax.broadcasted_iota(jnp.int32, s.shape, 0)) & (Q - 1))
        return jnp.where(k0 + lax.broadcasted_iota(jnp.int32, s.shape, 1) <= t_k, s, NEG)

    n_past = (i * Q) // KC
    o_s = _attend_causal(q_aug, ks_ref, vs_ref, n_past, KC, scratch, causal_mask)

    gates = jax.nn.sigmoid(misc_ref[0])
    gate = gate_ref[0]
    for r in range(R):
        g0 = gates[:, MISC_GL + 3 * r:MISC_GL + 3 * r + 1]
        g1 = gates[:, MISC_GL + 3 * r + 1:MISC_GL + 3 * r + 2]
        g2 = gates[:, MISC_GL + 3 * r + 2:MISC_GL + 3 * r + 3]
        rows = slice(r * Q, (r + 1) * Q)
        o = g0 * o_c[rows] + g1 * o_s[rows] + g2 * o_w[rows]
        o_ref[0, :, r * 128:(r + 1) * 128] = (
            o * _silu(gate[:, r * 128:(r + 1) * 128])).astype(o_ref.dtype)


def _nsa_attn(qf, k_c, v_c, ks, vs, kw, vw, proj):
    b, s, _ = qf.shape
    nc = s // CMP_STRIDE
    kc_len = min(1024, s)
    wk = min(WINDOW + Q_BLOCK, s)
    per_b = lambda n, w: pl.BlockSpec((1, n, w), lambda bb, i: (bb, 0, 0))
    return pl.pallas_call(
        functools.partial(_nsa_attn_kernel, S=s, NC=nc, KC=kc_len, WK=wk),
        grid=(b, s // Q_BLOCK),
        in_specs=[pl.BlockSpec((1, Q_BLOCK, GROUP_W), lambda bb, i: (bb, i, 0)),
                  per_b(nc, 128), per_b(nc, 128),
                  per_b(s, 256), per_b(s, 128), per_b(s, 128), per_b(s, 128),
                  _seg_spec(Q_BLOCK, 128, OFF_MISC), _seg_spec(Q_BLOCK, GROUP_W, OFF_DG)],
        out_specs=pl.BlockSpec((1, Q_BLOCK, GROUP_W), lambda bb, i: (bb, i, 0)),
        out_shape=jax.ShapeDtypeStruct((b, s, GROUP_W), BF16),
        scratch_shapes=_attn_scratch(N_HEADS * Q_BLOCK, kc_len),
        compiler_params=_cparams(("parallel", "arbitrary")),
        name="nsa_attn",
    )(qf, k_c, v_c, ks, vs, kw, vw, proj, proj)


def _reorder_w_in_kernel(w_ref, o_ref):
    o = np.concatenate([[0], np.cumsum(IN_SIZES)]).tolist()
    (a_qkv, a_decay, a_beta, a_gate, b_cq, b_ckv, b_kpe, b_gate, c_x, c_gate,
     d_q, d_kv, d_gl, d_gate) = range(len(IN_SIZES))
    for l in range(w_ref.shape[1]):
        seg = lambda k: w_ref[o[k]:o[k + 1], l, :]
        off = 0
        for k in (a_qkv, a_gate, b_gate, c_x, c_gate, d_q, d_gate, d_kv, b_cq):
            o_ref[l, off:off + IN_SIZES[k], :] = seg(k).astype(BF16)
            off += IN_SIZES[k]
        assert off == OFF_MISC
        zeros = jnp.zeros((128 - MLA_ROPE - 2 * N_HEADS - 3 * N_HEADS, w_ref.shape[2]), F32)
        misc = jnp.concatenate([seg(b_kpe), seg(a_decay), seg(a_beta), seg(d_gl), zeros], axis=0)
        o_ref[l, OFF_MISC:OFF_MISC + 128, :] = misc.astype(BF16)
        o_ref[l, OFF_BCKV:OFF_BCKV + MLA_KV_RANK, :] = seg(b_ckv).astype(BF16)


def _reorder_w_in(w_in):
    depth, d, d_in = w_in.shape
    tc = 128
    w_t = jnp.transpose(w_in, (2, 0, 1))
    return pl.pallas_call(
        _reorder_w_in_kernel,
        grid=(d // tc,),
        in_specs=[pl.BlockSpec((d_in, depth, tc), lambda i: (0, 0, i))],
        out_specs=pl.BlockSpec((depth, D_PROJ, tc), lambda i: (0, 0, i)),
        out_shape=jax.ShapeDtypeStruct((depth, D_PROJ, d), BF16),
        compiler_params=_cparams(("parallel",)),
        name="reorder_w_in",
    )(w_t)


def kernel(x, positions, norm_g, w_in, w_out, gdn_conv_w, gdn_a_log, gdn_dt_bias, gdn_norm_g,
           mla_q_norm_g, mla_w_uq, mla_kv_norm_g, mla_w_ukv, mla_qk_norm_g,
           lru_conv_w, lru_conv_b, lru_w_r, lru_b_r, lru_w_i, lru_b_i, lru_lambda,
           nsa_q_norm_g, nsa_k_norm_g, nsa_cmp_pe, nsa_cmp_w1, nsa_cmp_b1, nsa_cmp_w2, nsa_cmp_b2):
    b, s, d = x.shape
    depth = w_in.shape[0]
    w_in_r = _reorder_w_in(w_in)
    w_out_b = w_out.astype(BF16)
    posf = positions.astype(F32)
    rope = _rope_tables(jnp.broadcast_to(posf[:, :, None], (b, s, 128)))
    nc = s // CMP_STRIDE
    pos_end = jnp.pad(posf.reshape(b, nc, CMP_STRIDE)[:, 1:, CMP_STRIDE - 1], ((0, 0), (0, 1)))
    posc = jnp.broadcast_to(pos_end[:, :, None], (b, nc, 128))

    x2d = x.reshape(b * s, d)
    for l in range(depth):
        proj = _inproj(x2d, norm_g[l].reshape(1, d), w_in_r, l).reshape(b, s, D_PROJ)
        (y_a,), (y_c,), (q_b, k_b, v_b), (q_d, ks, vs, kw, vw, kc_raw, vc_raw) = _token_block_call(
            [dict(_gdn_part(proj, gdn_conv_w[l], gdn_a_log[l], gdn_dt_bias[l], gdn_norm_g[l]),
                  host=True),
             _lru_part(proj, lru_conv_w[l], lru_conv_b[l], lru_w_r[l], lru_b_r[l], lru_w_i[l],
                       lru_b_i[l], lru_lambda[l]),
             _mla_prep_part(proj, rope, mla_q_norm_g[l], mla_w_uq[l], mla_kv_norm_g[l],
                            mla_w_ukv[l], mla_qk_norm_g[l]),
             _nsa_prep_part(proj, rope, nsa_q_norm_g[l], nsa_k_norm_g[l])],
            b, s, _token_block(s), "token_mixers")
        y_b = _mla_flash(q_b, k_b, v_b, proj)
        k_c, v_c = _nsa_compress(kc_raw, vc_raw, posc, nsa_cmp_pe[l], nsa_cmp_w1[l],
                                 nsa_cmp_b1[l], nsa_cmp_w2[l], nsa_cmp_b2[l], nsa_k_norm_g[l, 0])
        y_d = _nsa_attn(q_d, k_c, v_c, ks, vs, kw, vw, proj)
        ys = [y.reshape(b * s, GROUP_W) for y in (y_a, y_b, y_c, y_d)]
        x2d = _outproj(ys, w_out_b[l], x2d)
    return x2d.reshape(b, s, d)
```
